```python
import jax, jax.numpy as jnp
from jax import lax
import numpy as np

D_MODEL = 4096
BATCH = 4
SEQ = 2048
DEPTH = 4
DEC_BATCH = 32
DEC_SEQ = 4
PAST_LEN = 8192
PAGE_SIZE = 128

N_MIXERS = 3
N_A = len(range(0, DEPTH, N_MIXERS))
N_B = len(range(1, DEPTH, N_MIXERS))
N_C = len(range(2, DEPTH, N_MIXERS))
CONV_W = 3
SW_HEAD_DIM = 64
SW_HEADS = D_MODEL // SW_HEAD_DIM
SW_KV_HEADS = 8
SW_GROUP = SW_HEADS // SW_KV_HEADS
WINDOW = 128
SW_BLOCK = WINDOW
FOX_HEAD_DIM = 128
FOX_HEADS = D_MODEL // FOX_HEAD_DIM
FOX_KV_HEADS = 8
FOX_GROUP = FOX_HEADS // FOX_KV_HEADS
FOX_BLOCK = 128
FORGET_BIAS = 8.0
D_FF = ((8 * D_MODEL // 3 + 255) // 256) * 256
ALPHA = (2 * DEPTH) ** 0.25
BETA = (8 * DEPTH) ** -0.25
LN_EPS = 1e-5
NEG_INF = -1e30

kernel_name = 'hybrid_conv_swa_fox_deepnorm_step'


def layer_norm(x, g, b):
    xf = x.astype(jnp.float32)
    mu = jnp.mean(xf, axis=-1, keepdims=True)
    var = jnp.mean(jnp.square(xf - mu), axis=-1, keepdims=True)
    return ((xf - mu) * lax.rsqrt(var + LN_EPS) * g.astype(jnp.float32) + b.astype(jnp.float32)).astype(x.dtype)


def causal_dwconv(u, prev, w):
    full = jnp.concatenate([prev.astype(u.dtype), u], axis=1)
    t = u.shape[1]
    y = sum(w[k] * full[:, k:k + t] for k in range(CONV_W))
    return y, full[:, -(CONV_W - 1):]


def zeros_prev(x, c):
    return jnp.zeros((x.shape[0], CONV_W - 1, c), x.dtype)


def attend(q, k, v, bias, mask, sinks):
    kvh, g, d = q.shape[2], q.shape[3], q.shape[4]
    s = jnp.einsum('bqkgd,bskd->bkgqs', q, k, preferred_element_type=jnp.float32) * (d ** -0.5)
    if bias is not None:
        s = s + bias
    s = jnp.where(mask, s, NEG_INF)
    if sinks is not None:
        sk = jnp.broadcast_to(sinks.astype(jnp.float32).reshape(1, kvh, g, 1, 1), s.shape[:-1] + (1,))
        p = jax.nn.softmax(jnp.concatenate([s, sk], axis=-1), axis=-1)[..., :-1]
    else:
        p = jax.nn.softmax(s, axis=-1)
    o = jnp.einsum('bkgqs,bskd->bqkgd', p.astype(v.dtype), v)
    return o.reshape(o.shape[0], o.shape[1], kvh * g * d)


def short_conv_mixer(x, prev, w_in, w_conv, w_out):
    b_gate, c_gate, h = jnp.split(x @ w_in, 3, axis=-1)
    y, new_prev = causal_dwconv(c_gate * h, prev, w_conv)
    return (b_gate * y) @ w_out, new_prev


def swa_project(x, w_qkv):
    b, t, _ = x.shape
    q, k, v = jnp.split(x @ w_qkv, [SW_HEADS * SW_HEAD_DIM, (SW_HEADS + SW_KV_HEADS) * SW_HEAD_DIM], axis=-1)
    return (q.reshape(b, t, SW_KV_HEADS, SW_GROUP, SW_HEAD_DIM),
            k.reshape(b, t, SW_KV_HEADS, SW_HEAD_DIM),
            v.reshape(b, t, SW_KV_HEADS, SW_HEAD_DIM))


def swa_prompt(x, w_qkv, sinks, w_o):
    b, s, _ = x.shape
    q, k, v = swa_project(x, w_qkv)
    nb = s // SW_BLOCK

    def band(z):
        zp = jnp.pad(z, ((0, 0), (SW_BLOCK, 0), (0, 0), (0, 0))).reshape(b, nb + 1, SW_BLOCK, SW_KV_HEADS, SW_HEAD_DIM)
        return jnp.concatenate([zp[:, :-1], zp[:, 1:]], axis=2).reshape(b * nb, 2 * SW_BLOCK, SW_KV_HEADS, SW_HEAD_DIM)

    qpos = jnp.arange(s).reshape(nb, SW_BLOCK)
    kpos = (jnp.arange(nb)[:, None] - 1) * SW_BLOCK + jnp.arange(2 * SW_BLOCK)[None, :]
    rel = qpos[:, :, None] - kpos[:, None, :]
    mask = (rel >= 0) & (rel <= WINDOW) & (kpos[:, None, :] >= 0)
    mask = jnp.tile(mask, (b, 1, 1))[:, None, None]
    qb = q.reshape(b * nb, SW_BLOCK, SW_KV_HEADS, SW_GROUP, SW_HEAD_DIM)
    o = attend(qb, band(k), band(v), None, mask, sinks).reshape(b, s, SW_HEADS * SW_HEAD_DIM)
    keep = min(WINDOW, s)
    return o @ w_o, k[:, -keep:], v[:, -keep:]


def swa_sample(x, buf_k, buf_v, w_qkv, sinks, w_o):
    b, t, _ = x.shape
    q, k, v = swa_project(x, w_qkv)
    wb = buf_k.shape[1]
    kk = jnp.concatenate([buf_k.astype(k.dtype), k], axis=1)
    vv = jnp.concatenate([buf_v.astype(v.dtype), v], axis=1)
    qpos = PAST_LEN + jnp.arange(t)
    kpos = PAST_LEN - wb + jnp.arange(wb + t)
    rel = qpos[:, None] - kpos[None, :]
    mask = (rel >= 0) & (rel <= WINDOW)
    o = attend(q, kk, vv, None, mask, sinks)
    return o @ w_o, kk[:, -wb:], vv[:, -wb:]


def fox_project(x, w_qkvf, b_f):
    b, t, _ = x.shape
    hq = FOX_HEADS * FOX_HEAD_DIM
    hk = FOX_KV_HEADS * FOX_HEAD_DIM
    q, k, v, fz = jnp.split(x @ w_qkvf, [hq, hq + hk, hq + 2 * hk], axis=-1)
    logf = jax.nn.log_sigmoid(fz.astype(jnp.float32) + b_f.astype(jnp.float32))
    return (q.reshape(b, t, FOX_KV_HEADS, FOX_GROUP, FOX_HEAD_DIM),
            k.reshape(b, t, FOX_KV_HEADS, FOX_HEAD_DIM),
            v.reshape(b, t, FOX_KV_HEADS, FOX_HEAD_DIM), logf)


def head_major(c):
    b, t, _ = c.shape
    return c.reshape(b, t, FOX_KV_HEADS, FOX_GROUP).transpose(0, 2, 3, 1)


def fox_prompt(x, w_qkvf, b_f, w_o):
    b, s, _ = x.shape
    q, k, v, logf = fox_project(x, w_qkvf, b_f)
    c = jnp.cumsum(logf, axis=1)
    nb = s // FOX_BLOCK
    qb = q.reshape(b, nb, FOX_BLOCK, FOX_KV_HEADS, FOX_GROUP, FOX_HEAD_DIM).swapaxes(0, 1)
    cb = c.reshape(b, nb, FOX_BLOCK, FOX_HEADS).swapaxes(0, 1)
    c_k = head_major(c)[..., None, :]
    kpos = jnp.arange(s)

    def block(args):
        qi, ci, bi = args
        qpos = bi * FOX_BLOCK + jnp.arange(FOX_BLOCK)
        mask = kpos[None, :] <= qpos[:, None]
        bias = head_major(ci)[..., :, None] - c_k
        return attend(qi, k, v, bias, mask, None)

    o = lax.map(block, (qb, cb, jnp.arange(nb)))
    o = o.swapaxes(0, 1).reshape(b, s, FOX_HEADS * FOX_HEAD_DIM)
    return o @ w_o, k, v, logf


def fox_sample(x, page_table, pool_k, pool_v, pool_logf, w_qkvf, b_f, w_o):
    b, t, _ = x.shape
    q, k, v, logf = fox_project(x, w_qkvf, b_f)
    past = page_table.shape[1] * PAGE_SIZE
    gather = lambda pool: pool[page_table].reshape((b, past) + pool.shape[2:])
    kk = jnp.concatenate([gather(pool_k).astype(k.dtype), k], axis=1)
    vv = jnp.concatenate([gather(pool_v).astype(v.dtype), v], axis=1)
    lf_past = gather(pool_logf).astype(jnp.float32)
    suffix = lax.cumsum(lf_past, axis=1, reverse=True) - lf_past
    c_new = jnp.cumsum(logf, axis=1)
    c_k = jnp.concatenate([-suffix, c_new], axis=1)
    bias = head_major(c_new)[..., :, None] - head_major(c_k)[..., None, :]
    kpos = jnp.arange(past + t)
    qpos = past + jnp.arange(t)
    mask = kpos[None, :] <= qpos[:, None]
    o = attend(q, kk, vv, bias, mask, None)
    return o @ w_o, k, v, logf


def conv_ffn(x, prev, w_up, w_conv, w_down):
    h, new_prev = causal_dwconv(x @ w_up, prev, w_conv)
    g, u = jnp.split(h, 2, axis=-1)
    return (jax.nn.silu(g) * u) @ w_down, new_prev


def setup_inputs(seed: int = 0) -> dict:
    key = jax.random.key(seed)
    ks = iter(jax.random.split(key, 32))
    nrm = lambda shape, scale=1.0: jax.random.normal(next(ks), shape, jnp.float32) * scale
    n_pages = PAST_LEN // PAGE_SIZE
    n_used = DEC_BATCH * n_pages
    n_pool = n_used + max(1, n_used // 4)
    wb = min(WINDOW, PAST_LEN)
    inv_d = D_MODEL ** -0.5
    page_table = jax.random.permutation(next(ks), n_pool)[:n_used].reshape(DEC_BATCH, n_pages).astype(jnp.int32)
    return {
        'x_prompt': nrm((BATCH, SEQ, D_MODEL)),
        'x_sample': nrm((DEC_BATCH, DEC_SEQ, D_MODEL)),
        'state_conv': nrm((N_A, DEC_BATCH, CONV_W - 1, D_MODEL)),
        'cache_win_k': nrm((N_B, DEC_BATCH, wb, SW_KV_HEADS, SW_HEAD_DIM)),
        'cache_win_v': nrm((N_B, DEC_BATCH, wb, SW_KV_HEADS, SW_HEAD_DIM)),
        'cache_k': nrm((N_C, n_pool, PAGE_SIZE, FOX_KV_HEADS, FOX_HEAD_DIM)),
        'cache_v': nrm((N_C, n_pool, PAGE_SIZE, FOX_KV_HEADS, FOX_HEAD_DIM)),
        'cache_logf': jax.nn.log_sigmoid(FORGET_BIAS + nrm((N_C, n_pool, PAGE_SIZE, FOX_HEADS))),
        'state_ffn': nrm((DEPTH, DEC_BATCH, CONV_W - 1, 2 * D_FF)),
        'page_table': page_table,
        'w_in_a': nrm((N_A, D_MODEL, 3 * D_MODEL), inv_d),
        'conv_a': nrm((N_A, CONV_W, D_MODEL), CONV_W ** -0.5),
        'w_out_a': nrm((N_A, D_MODEL, D_MODEL), BETA * inv_d),
        'w_qkv_b': nrm((N_B, D_MODEL, (SW_HEADS + 2 * SW_KV_HEADS) * SW_HEAD_DIM), inv_d),
        'sinks_b': nrm((N_B, SW_HEADS), 0.5),
        'w_o_b': nrm((N_B, SW_HEADS * SW_HEAD_DIM, D_MODEL), BETA * inv_d),
        'w_qkvf_c': nrm((N_C, D_MODEL, (FOX_HEADS + 2 * FOX_KV_HEADS) * FOX_HEAD_DIM + FOX_HEADS), inv_d),
        'b_f_c': FORGET_BIAS + nrm((N_C, FOX_HEADS), 0.1),
        'w_o_c': nrm((N_C, FOX_HEADS * FOX_HEAD_DIM, D_MODEL), BETA * inv_d),
        'ln1_g': 1.0 + nrm((DEPTH, D_MODEL), 0.05),
        'ln1_b': nrm((DEPTH, D_MODEL), 0.02),
        'w_up': nrm((DEPTH, D_MODEL, 2 * D_FF), inv_d),
        'conv_f': nrm((DEPTH, CONV_W, 2 * D_FF), CONV_W ** -0.5),
        'w_down': nrm((DEPTH, D_FF, D_MODEL), BETA * D_FF ** -0.5),
        'ln2_g': 1.0 + nrm((DEPTH, D_MODEL), 0.05),
        'ln2_b': nrm((DEPTH, D_MODEL), 0.02),
    }


def reference(x_prompt, x_sample, state_conv, cache_win_k, cache_win_v, cache_k, cache_v, cache_logf,
              state_ffn, page_table, w_in_a, conv_a, w_out_a, w_qkv_b, sinks_b, w_o_b, w_qkvf_c, b_f_c,
              w_o_c, ln1_g, ln1_b, w_up, conv_f, w_down, ln2_g, ln2_b):
    xp, xs = x_prompt, x_sample
    conv_p, conv_s, wk_p, wv_p, wk_s, wv_s = [], [], [], [], [], []
    fk_p, fv_p, fl_p, fk_s, fv_s, fl_s = [], [], [], [], [], []
    ffn_p, ffn_s = [], []
    for i in range(DEPTH):
        m, j = i % N_MIXERS, i // N_MIXERS
        if m == 0:
            mp, st_p = short_conv_mixer(xp, zeros_prev(xp, D_MODEL), w_in_a[j], conv_a[j], w_out_a[j])
            ms, st_s = short_conv_mixer(xs, state_conv[j], w_in_a[j], conv_a[j], w_out_a[j])
            conv_p.append(st_p)
            conv_s.append(st_s)
        elif m == 1:
            mp, kp, vp = swa_prompt(xp, w_qkv_b[j], sinks_b[j], w_o_b[j])
            ms, ks_, vs_ = swa_sample(xs, cache_win_k[j], cache_win_v[j], w_qkv_b[j], sinks_b[j], w_o_b[j])
            wk_p.append(kp)
            wv_p.append(vp)
            wk_s.append(ks_)
            wv_s.append(vs_)
        else:
            mp, kp, vp, lp = fox_prompt(xp, w_qkvf_c[j], b_f_c[j], w_o_c[j])
            ms, ks_, vs_, ls_ = fox_sample(xs, page_table, cache_k[j], cache_v[j], cache_logf[j],
                                           w_qkvf_c[j], b_f_c[j], w_o_c[j])
            fk_p.append(kp)
            fv_p.append(vp)
            fl_p.append(lp)
            fk_s.append(ks_)
            fv_s.append(vs_)
            fl_s.append(ls_)
        xp = layer_norm(ALPHA * xp + mp, ln1_g[i], ln1_b[i])
        xs = layer_norm(ALPHA * xs + ms, ln1_g[i], ln1_b[i])
        fp, st_p = conv_ffn(xp, zeros_prev(xp, 2 * D_FF), w_up[i], conv_f[i], w_down[i])
        fs, st_s = conv_ffn(xs, state_ffn[i], w_up[i], conv_f[i], w_down[i])
        ffn_p.append(st_p)
        ffn_s.append(st_s)
        xp = layer_norm(ALPHA * xp + fp, ln2_g[i], ln2_b[i])
        xs = layer_norm(ALPHA * xs + fs, ln2_g[i], ln2_b[i])
    return (xp, xs, jnp.stack(conv_p), jnp.stack(conv_s), jnp.stack(wk_p), jnp.stack(wv_p),
            jnp.stack(wk_s), jnp.stack(wv_s), jnp.stack(fk_p), jnp.stack(fv_p), jnp.stack(fl_p),
            jnp.stack(fk_s), jnp.stack(fv_s), jnp.stack(fl_s), jnp.stack(ffn_p), jnp.stack(ffn_s))
```

```python
import functools

import jax
import jax.numpy as jnp
from jax import lax
from jax.experimental import pallas as pl
from jax.experimental.pallas import tpu as pltpu

F32 = jnp.float32
BF16 = jnp.bfloat16

CONV_TAPS = 3
CONV_PREV = CONV_TAPS - 1
SW_HEAD_DIM = 64
SW_KV_HEADS = 8
WINDOW = 128
FOX_HEAD_DIM = 128
FOX_KV_HEADS = 8
PAGE_SIZE = 128
LN_EPS = 1e-5
NEG_INF = -1e30

V7X_VMEM_BYTES = 64 * 1024 * 1024
VMEM_LIMIT = V7X_VMEM_BYTES - 8 * 1024 * 1024
LANE = 128


def _params(semantics):
    return pltpu.CompilerParams(dimension_semantics=semantics, vmem_limit_bytes=VMEM_LIMIT)


def _dot(a, b):
    return jnp.dot(a, b, preferred_element_type=F32)


def _dot_nt(a, b):
    return lax.dot_general(a, b, (((1,), (1,)), ((), ())), preferred_element_type=F32)


def _pick(n, *cands):
    for c in cands:
        if n % c == 0:
            return c
    raise ValueError(f"no tile for {n} among {cands}")


def _conv_rows(u, prev, cw, shift):
    rows = u.shape[0]
    if prev is None:
        assert shift == 1
        row = lax.broadcasted_iota(jnp.int32, u.shape, 0)
        u1 = jnp.where(row >= 1, pltpu.roll(u, 1, 0), 0.0)
        u2 = jnp.where(row >= 2, pltpu.roll(u, 2, 0), 0.0)
        tail = u[rows - CONV_PREV:rows]
    else:
        full = jnp.concatenate([prev, u], axis=0)
        u2 = full[0:rows]
        u1 = full[shift:shift + rows]
        tail = full[rows:rows + CONV_PREV * shift]
    y = cw[0:1, :] * u2 + cw[1:2, :] * u1 + cw[2:3, :] * u
    return y, tail


def _gate_conv_core(x, wb, wc, wh, cw, prev, shift):
    bg = _dot(x, wb)
    c = _dot(x, wc)
    h = _dot(x, wh)
    y, tail = _conv_rows(c * h, prev, cw, shift)
    return (bg * y).astype(BF16), tail


def _gate_conv_p_kernel(x_ref, wb_ref, wc_ref, wh_ref, cw_ref, z_ref, st_ref):
    z, tail = _gate_conv_core(x_ref[...], wb_ref[...], wc_ref[...], wh_ref[...], cw_ref[...], None, 1)
    z_ref[...] = z
    st_ref[0] = tail


def _gate_conv_s_kernel(x_ref, wb_ref, wc_ref, wh_ref, cw_ref, prev_ref,
                        z_ref, st_ref, wb16_ref, wc16_ref, wh16_ref, *, shift):
    wb = wb_ref[...].astype(BF16)
    wc = wc_ref[...].astype(BF16)
    wh = wh_ref[...].astype(BF16)
    wb16_ref[...] = wb
    wc16_ref[...] = wc
    wh16_ref[...] = wh
    z, tail = _gate_conv_core(x_ref[...], wb, wc, wh, cw_ref[...], prev_ref[...], shift)
    z_ref[...] = z
    st_ref[...] = tail


def gate_conv_sample(x16, w_in, cw, prev, shift):
    rows, d = x16.shape
    c = w_in.shape[1] // 3
    tn = _pick(c, 256, 128)
    nc = c // tn
    w_spec = lambda sec: pl.BlockSpec((d, tn), lambda n, sec=sec: (0, sec * nc + n))
    col = lambda r: pl.BlockSpec((r, tn), lambda n: (0, n))
    return pl.pallas_call(
        functools.partial(_gate_conv_s_kernel, shift=shift),
        grid=(nc,),
        in_specs=[pl.BlockSpec((rows, d), lambda n: (0, 0)), w_spec(0), w_spec(1), w_spec(2),
                  col(CONV_TAPS), col(CONV_PREV * shift)],
        out_specs=[col(rows), col(CONV_PREV * shift), col(d), col(d), col(d)],
        out_shape=[jax.ShapeDtypeStruct((rows, c), BF16),
                   jax.ShapeDtypeStruct((CONV_PREV * shift, c), F32),
                   jax.ShapeDtypeStruct((d, c), BF16),
                   jax.ShapeDtypeStruct((d, c), BF16),
                   jax.ShapeDtypeStruct((d, c), BF16)],
        compiler_params=_params(("arbitrary",)),
        name="gate_conv_sample",
    )(x16, w_in, w_in, w_in, cw, prev)


def gate_conv_prompt(x16, wb16, wc16, wh16, cw, batch):
    m, d = x16.shape
    s = m // batch
    c = wb16.shape[1]
    tn = _pick(c, 256, 128)
    w_spec = pl.BlockSpec((d, tn), lambda b, n: (0, n))
    return pl.pallas_call(
        _gate_conv_p_kernel,
        grid=(batch, c // tn),
        in_specs=[pl.BlockSpec((s, d), lambda b, n: (b, 0), pipeline_mode=pl.Buffered(1)),
                  w_spec, w_spec, w_spec,
                  pl.BlockSpec((CONV_TAPS, tn), lambda b, n: (0, n))],
        out_specs=[pl.BlockSpec((s, tn), lambda b, n: (b, n)),
                   pl.BlockSpec((1, CONV_PREV, tn), lambda b, n: (b, 0, n))],
        out_shape=[jax.ShapeDtypeStruct((m, c), BF16),
                   jax.ShapeDtypeStruct((batch, CONV_PREV, c), F32)],
        compiler_params=_params(("arbitrary", "arbitrary")),
        name="gate_conv_prompt",
    )(x16, wb16, wc16, wh16, cw)


def _ffn_up_core(x, wg, wu, cwg, cwu, prev_g, prev_u, shift):
    cg, tail_g = _conv_rows(_dot(x, wg), prev_g, cwg, shift)
    cu, tail_u = _conv_rows(_dot(x, wu), prev_u, cwu, shift)
    act = (cg * (1.0 / (1.0 + jnp.exp(-cg)))) * cu
    return act.astype(BF16), tail_g, tail_u


def _ffn_up_p_kernel(x_ref, wg_ref, wu_ref, cwg_ref, cwu_ref, a_ref, sg_ref, su_ref):
    act, tg, tu = _ffn_up_core(x_ref[...], wg_ref[...], wu_ref[...], cwg_ref[...], cwu_ref[...],
                               None, None, 1)
    a_ref[...] = act
    sg_ref[0] = tg
    su_ref[0] = tu


def _ffn_up_s_kernel(x_ref, wg_ref, wu_ref, cwg_ref, cwu_ref, pg_ref, pu_ref,
                     a_ref, sg_ref, su_ref, wg16_ref, wu16_ref, *, shift):
    wg = wg_ref[...].astype(BF16)
    wu = wu_ref[...].astype(BF16)
    wg16_ref[...] = wg
    wu16_ref[...] = wu
    act, tg, tu = _ffn_up_core(x_ref[...], wg, wu, cwg_ref[...], cwu_ref[...],
                               pg_ref[...], pu_ref[...], shift)
    a_ref[...] = act
    sg_ref[...] = tg
    su_ref[...] = tu


def ffn_up_sample(x16, w_up, cw, prev, shift):
    rows, d = x16.shape
    f = w_up.shape[1] // 2
    tn = _pick(f, 256, 128)
    nf = f // tn
    sec = lambda r, k: pl.BlockSpec((r, tn), lambda n, k=k: (0, k * nf + n))
    col = lambda r: pl.BlockSpec((r, tn), lambda n: (0, n))
    pr = CONV_PREV * shift
    return pl.pallas_call(
        functools.partial(_ffn_up_s_kernel, shift=shift),
        grid=(nf,),
        in_specs=[pl.BlockSpec((rows, d), lambda n: (0, 0)), sec(d, 0), sec(d, 1),
                  sec(CONV_TAPS, 0), sec(CONV_TAPS, 1), sec(pr, 0), sec(pr, 1)],
        out_specs=[col(rows), col(pr), col(pr), col(d), col(d)],
        out_shape=[jax.ShapeDtypeStruct((rows, f), BF16),
                   jax.ShapeDtypeStruct((pr, f), F32),
                   jax.ShapeDtypeStruct((pr, f), F32),
                   jax.ShapeDtypeStruct((d, f), BF16),
                   jax.ShapeDtypeStruct((d, f), BF16)],
        compiler_params=_params(("arbitrary",)),
        name="ffn_up_sample",
    )(x16, w_up, w_up, cw, cw, prev, prev)


def ffn_up_prompt(x16, wg16, wu16, cw, batch):
    m, d = x16.shape
    s = m // batch
    f = wg16.shape[1]
    tn = _pick(f, 256, 128)
    nf = f // tn
    w_spec = pl.BlockSpec((d, tn), lambda b, n: (0, n))
    st_spec = pl.BlockSpec((1, CONV_PREV, tn), lambda b, n: (b, 0, n))
    return pl.pallas_call(
        _ffn_up_p_kernel,
        grid=(batch, nf),
        in_specs=[pl.BlockSpec((s, d), lambda b, n: (b, 0), pipeline_mode=pl.Buffered(1)),
                  w_spec, w_spec,
                  pl.BlockSpec((CONV_TAPS, tn), lambda b, n: (0, n)),
                  pl.BlockSpec((CONV_TAPS, tn), lambda b, n: (0, nf + n))],
        out_specs=[pl.BlockSpec((s, tn), lambda b, n: (b, n)), st_spec, st_spec],
        out_shape=[jax.ShapeDtypeStruct((m, f), BF16),
                   jax.ShapeDtypeStruct((batch, CONV_PREV, f), F32),
                   jax.ShapeDtypeStruct((batch, CONV_PREV, f), F32)],
        compiler_params=_params(("arbitrary", "arbitrary")),
        name="ffn_up_prompt",
    )(x16, wg16, wu16, cw, cw)


def _mm_p_kernel(*refs, alpha, has_res):
    if has_res:
        x_ref, w_ref, r_ref, o_ref = refs
    else:
        x_ref, w_ref, o_ref = refs
    acc = _dot(x_ref[...], w_ref[...])
    if has_res:
        acc = alpha * r_ref[...] + acc
    o_ref[...] = acc.astype(o_ref.dtype)


def _mm_s_kernel(*refs, alpha, has_res):
    if has_res:
        x_ref, w_ref, r_ref, o_ref, w16_ref = refs
    else:
        x_ref, w_ref, o_ref, w16_ref = refs
    w = w_ref[...].astype(BF16)
    w16_ref[...] = w
    acc = _dot(x_ref[...], w)
    if has_res:
        acc = alpha * r_ref[...] + acc
    o_ref[...] = acc.astype(o_ref.dtype)


def mm_sample(x16, w, n_out, out_dtype, res=None, alpha=None):
    rows, k = x16.shape
    tn = _pick(n_out, 512, 256, 128) if k <= 4096 else _pick(n_out, 256, 128)
    col = lambda r: pl.BlockSpec((r, tn), lambda n: (0, n))
    in_specs = [pl.BlockSpec((rows, k), lambda n: (0, 0)), col(k)]
    args = [x16, w]
    if res is not None:
        in_specs.append(col(rows))
        args.append(res)
    return pl.pallas_call(
        functools.partial(_mm_s_kernel, alpha=alpha, has_res=res is not None),
        grid=(n_out // tn,),
        in_specs=in_specs,
        out_specs=[col(rows), col(k)],
        out_shape=[jax.ShapeDtypeStruct((rows, n_out), out_dtype),
                   jax.ShapeDtypeStruct((k, n_out), BF16)],
        compiler_params=_params(("arbitrary",)),
        name="mm_sample",
    )(*args)


def mm_prompt(x16, w16, n_out, col_off, out_dtype, res=None, alpha=None):
    m, k = x16.shape
    if k <= 4096:
        tm, tn = _pick(m, 1024, 512, 256), _pick(n_out, 512, 256, 128)
    else:
        tm, tn = _pick(m, 512, 256), _pick(n_out, 512, 256, 128)
    assert col_off % tn == 0
    off = col_off // tn
    in_specs = [pl.BlockSpec((tm, k), lambda i, n: (i, 0)),
                pl.BlockSpec((k, tn), lambda i, n: (0, off + n))]
    args = [x16, w16]
    if res is not None:
        in_specs.append(pl.BlockSpec((tm, tn), lambda i, n: (i, n)))
        args.append(res)
    return pl.pallas_call(
        functools.partial(_mm_p_kernel, alpha=alpha, has_res=res is not None),
        grid=(m // tm, n_out // tn),
        in_specs=in_specs,
        out_specs=pl.BlockSpec((tm, tn), lambda i, n: (i, n)),
        out_shape=jax.ShapeDtypeStruct((m, n_out), out_dtype),
        compiler_params=_params(("arbitrary", "arbitrary")),
        name="mm_prompt",
    )(*args)


def _ln_kernel(y_ref, g_ref, b_ref, xf_ref, xb_ref):
    y = y_ref[...]
    mu = jnp.mean(y, axis=-1, keepdims=True)
    dev = y - mu
    var = jnp.mean(dev * dev, axis=-1, keepdims=True)
    out = dev * lax.rsqrt(var + LN_EPS) * g_ref[...] + b_ref[...]
    xf_ref[...] = out
    xb_ref[...] = out.astype(BF16)


def layer_norm_rows(y, g, b):
    m, d = y.shape
    tr = _pick(m, 256, 128)
    row = pl.BlockSpec((tr, d), lambda i: (i, 0))
    vec = pl.BlockSpec((1, d), lambda i: (0, 0))
    return pl.pallas_call(
        _ln_kernel,
        grid=(m // tr,),
        in_specs=[row, vec, vec],
        out_specs=[row, row],
        out_shape=[jax.ShapeDtypeStruct((m, d), F32), jax.ShapeDtypeStruct((m, d), BF16)],
        compiler_params=_params(("arbitrary",)),
        name="layer_norm",
    )(y, g.reshape(1, d), b.reshape(1, d))


def _sink_softmax_pv(s, sink, v16):
    m = jnp.maximum(jnp.max(s, axis=-1, keepdims=True), sink)
    p = jnp.exp(s - m)
    den = jnp.sum(p, axis=-1, keepdims=True) + jnp.exp(sink - m)
    return _dot((p / den).astype(BF16), v16)


def _swa_p_kernel(q_ref, kp_ref, ko_ref, vp_ref, vo_ref, sink_ref, o_ref, *, grp, scale):
    blk = q_ref.shape[0]
    first = pl.program_id(1) == 0
    kcat = jnp.concatenate([kp_ref[...], ko_ref[...]], axis=0).astype(BF16)
    vcat = jnp.concatenate([vp_ref[...], vo_ref[...]], axis=0).astype(BF16)
    tl = lax.broadcasted_iota(jnp.int32, (blk, 2 * blk), 0)
    j = lax.broadcasted_iota(jnp.int32, (blk, 2 * blk), 1)
    mask = (j >= tl) & (j <= tl + WINDOW) & ((j >= blk) | jnp.logical_not(first))
    hd = SW_HEAD_DIM
    for h in range(SW_KV_HEADS):
        kh = kcat[:, h * hd:(h + 1) * hd]
        vh = vcat[:, h * hd:(h + 1) * hd]
        for g in range(grp):
            head = h * grp + g
            s = _dot_nt(q_ref[:, head * hd:(head + 1) * hd], kh) * scale
            s = jnp.where(mask, s, NEG_INF)
            o = _sink_softmax_pv(s, sink_ref[head], vh)
            o_ref[:, head * hd:(head + 1) * hd] = o.astype(BF16)


def swa_prompt(q16, kv, sinks, batch):
    m, qw = q16.shape
    s = m // batch
    blk = WINDOW
    nb = s // blk
    kw = SW_KV_HEADS * SW_HEAD_DIM
    grp = qw // kw
    own = lambda c: pl.BlockSpec((blk, kw), lambda b, i, c=c: (b * nb + i, c))
    prev = lambda c: pl.BlockSpec((blk, kw), lambda b, i, c=c: (b * nb + jnp.maximum(i - 1, 0), c))
    return pl.pallas_call(
        functools.partial(_swa_p_kernel, grp=grp, scale=SW_HEAD_DIM ** -0.5),
        grid=(batch, nb),
        in_specs=[pl.BlockSpec((blk, qw), lambda b, i: (b * nb + i, 0)),
                  prev(0), own(0), prev(1), own(1),
                  pl.BlockSpec(memory_space=pltpu.SMEM)],
        out_specs=pl.BlockSpec((blk, qw), lambda b, i: (b * nb + i, 0)),
        out_shape=jax.ShapeDtypeStruct((m, qw), BF16),
        compiler_params=_params(("arbitrary", "arbitrary")),
        name="swa_prompt",
    )(q16, kv, kv, kv, kv, sinks)


def _swa_s_kernel(q_ref, ck_ref, cv_ref, kn_ref, vn_ref, sink_ref, o_ref, *, steps, scale):
    wb = ck_ref.shape[1]
    pad = jnp.zeros((wb - kn_ref.shape[1], ck_ref.shape[2]), F32)
    kcat = jnp.concatenate([ck_ref[0], kn_ref[0], pad], axis=0).astype(BF16)
    vcat = jnp.concatenate([cv_ref[0], vn_ref[0], pad], axis=0).astype(BF16)
    rows = q_ref.shape[2]
    t = lax.broadcasted_iota(jnp.int32, (rows, 2 * wb), 0) % steps
    j = lax.broadcasted_iota(jnp.int32, (rows, 2 * wb), 1)
    mask = (j >= t + wb - WINDOW) & (j <= t + wb)
    hd = SW_HEAD_DIM
    for h in range(SW_KV_HEADS):
        s = _dot_nt(q_ref[0, h], kcat[:, h * hd:(h + 1) * hd]) * scale
        s = jnp.where(mask, s, NEG_INF)
        o_ref[0, h] = _sink_softmax_pv(s, sink_ref[h], vcat[:, h * hd:(h + 1) * hd])


def swa_sample(q16, cache_k, cache_v, k_new, v_new, sink_col, steps):
    bd, kvh, rows, hd = q16.shape
    wb, kw = cache_k.shape[1], cache_k.shape[2]
    per_b = lambda a: pl.BlockSpec((1,) + a.shape[1:], lambda b: (b,) + (0,) * (a.ndim - 1))
    return pl.pallas_call(
        functools.partial(_swa_s_kernel, steps=steps, scale=SW_HEAD_DIM ** -0.5),
        grid=(bd,),
        in_specs=[per_b(q16), per_b(cache_k), per_b(cache_v), per_b(k_new), per_b(v_new),
                  pl.BlockSpec(sink_col.shape, lambda b: (0, 0, 0))],
        out_specs=pl.BlockSpec((1, kvh, rows, hd), lambda b: (b, 0, 0, 0)),
        out_shape=jax.ShapeDtypeStruct((bd, kvh, rows, hd), F32),
        compiler_params=_params(("arbitrary",)),
        name="swa_sample",
    )(q16, cache_k, cache_v, k_new, v_new, sink_col)


def _log_sigmoid(z):
    return jnp.minimum(z, 0.0) - jnp.log1p(jnp.exp(-jnp.abs(z)))


def _fox_logf_p_kernel(x_ref, wf_ref, bf_ref, lf_ref, c_ref, *, blk):
    lf = _log_sigmoid(_dot(x_ref[...], wf_ref[...].astype(BF16)) + bf_ref[...])
    lf_ref[...] = lf
    r = lax.broadcasted_iota(jnp.int32, (blk, blk), 0)
    c = lax.broadcasted_iota(jnp.int32, (blk, blk), 1)
    tri = (c <= r).astype(F32)
    carry = jnp.zeros((1, lf.shape[1]), F32)
    for i in range(lf.shape[0] // blk):
        cb = jnp.dot(tri, lf[i * blk:(i + 1) * blk], precision=lax.Precision.HIGHEST,
                     preferred_element_type=F32) + carry
        c_ref[i * blk:(i + 1) * blk, :] = cb
        carry = cb[blk - 1:blk, :]


def fox_logf_prompt(x16, wf, bf, batch):
    m, d = x16.shape
    s = m // batch
    h = wf.shape[1]
    row = pl.BlockSpec((s, h), lambda b: (b, 0))
    return pl.pallas_call(
        functools.partial(_fox_logf_p_kernel, blk=_pick(s, 256, 128)),
        grid=(batch,),
        in_specs=[pl.BlockSpec((s, d), lambda b: (b, 0)),
                  pl.BlockSpec((d, h), lambda b: (0, 0)),
                  pl.BlockSpec((1, h), lambda b: (0, 0))],
        out_specs=[row, row],
        out_shape=[jax.ShapeDtypeStruct((m, h), F32), jax.ShapeDtypeStruct((m, h), F32)],
        compiler_params=_params(("arbitrary",)),
        name="fox_logf_prompt",
    )(x16, wf, bf.reshape(1, h))


def _fox_logf_s_kernel(x_ref, wf_ref, bf_ref, lf_ref, c_ref, *, shift):
    lf = _log_sigmoid(_dot(x_ref[...], wf_ref[...].astype(BF16)) + bf_ref[...])
    lf_ref[...] = lf
    run = lf[0:shift]
    c_ref[0:shift, :] = run
    for t in range(1, lf.shape[0] // shift):
        run = run + lf[t * shift:(t + 1) * shift]
        c_ref[t * shift:(t + 1) * shift, :] = run


def fox_logf_sample(x16, wf, bf, shift):
    rows, d = x16.shape
    h = wf.shape[1]
    full = lambda a, b: pl.BlockSpec((a, b), lambda i: (0, 0))
    return pl.pallas_call(
        functools.partial(_fox_logf_s_kernel, shift=shift),
        grid=(1,),
        in_specs=[full(rows, d), full(d, h), full(1, h)],
        out_specs=[full(rows, h), full(rows, h)],
        out_shape=[jax.ShapeDtypeStruct((rows, h), F32), jax.ShapeDtypeStruct((rows, h), F32)],
        compiler_params=_params(("arbitrary",)),
        name="fox_logf_sample",
    )(x16, wf, bf.reshape(1, h))


def _fox_p_kernel(q_ref, k_ref, v_ref, cq_ref, ck_ref, o_ref, m_ref, l_ref, acc_ref, *, grp, scale):
    qi = pl.program_id(2)
    ki = pl.program_id(3)
    tq = q_ref.shape[0]
    tk = k_ref.shape[0]
    hd = FOX_HEAD_DIM

    @pl.when(ki == 0)
    def _():
        m_ref[...] = jnp.full(m_ref.shape, NEG_INF, F32)
        l_ref[...] = jnp.zeros(l_ref.shape, F32)
        acc_ref[...] = jnp.zeros(acc_ref.shape, F32)

    @pl.when(ki <= qi)
    def _():
        k16 = k_ref[...].astype(BF16)
        v16 = v_ref[...].astype(BF16)
        row = qi * tq + lax.broadcasted_iota(jnp.int32, (tq, tk), 0)
        col = ki * tk + lax.broadcasted_iota(jnp.int32, (tq, tk), 1)
        mask = col <= row
        for g in range(grp):
            s = _dot_nt(q_ref[:, g * hd:(g + 1) * hd], k16) * scale
            s = s + (cq_ref[0, 0, :, g:g + 1] - ck_ref[0, 0, g:g + 1, :])
            s = jnp.where(mask, s, NEG_INF)
            m_old = m_ref[g]
            m_new = jnp.maximum(m_old, jnp.max(s, axis=-1, keepdims=True))
            a = jnp.exp(m_old - m_new)
            p = jnp.exp(s - m_new)
            l_ref[g] = a * l_ref[g] + jnp.sum(p, axis=-1, keepdims=True)
            acc_ref[g] = a * acc_ref[g] + _dot(p.astype(BF16), v16)
            m_ref[g] = m_new

    @pl.when(ki == qi)
    def _():
        for g in range(grp):
            o_ref[:, g * hd:(g + 1) * hd] = (acc_ref[g] / l_ref[g]).astype(BF16)


def fox_prompt(q16, kv, c_q, c_k, batch):
    m, qw = q16.shape
    s = m // batch
    hd, kvh = FOX_HEAD_DIM, FOX_KV_HEADS
    grp = qw // (kvh * hd)
    tq = tk = _pick(s, 512, 256, 128)
    nq, nk = s // tq, s // tk
    kv_spec = lambda part: pl.BlockSpec(
        (tk, hd), lambda b, h, qi, ki, part=part: (b * nk + jnp.minimum(ki, qi), part * kvh + h))
    return pl.pallas_call(
        functools.partial(_fox_p_kernel, grp=grp, scale=hd ** -0.5),
        grid=(batch, kvh, nq, nk),
        in_specs=[pl.BlockSpec((tq, grp * hd), lambda b, h, qi, ki: (b * nq + qi, h)),
                  kv_spec(0), kv_spec(1),
                  pl.BlockSpec((1, 1, tq, grp), lambda b, h, qi, ki: (b, h, qi, 0)),
                  pl.BlockSpec((1, 1, grp, tk), lambda b, h, qi, ki: (b, h, 0, jnp.minimum(ki, qi)))],
        out_specs=pl.BlockSpec((tq, grp * hd), lambda b, h, qi, ki: (b * nq + qi, h)),
        out_shape=jax.ShapeDtypeStruct((m, qw), BF16),
        scratch_shapes=[pltpu.VMEM((grp, tq, 1), F32), pltpu.VMEM((grp, tq, 1), F32),
                        pltpu.VMEM((grp, tq, hd), F32)],
        compiler_params=_params(("arbitrary", "arbitrary", "arbitrary", "arbitrary")),
        name="fox_prompt",
    )(q16, kv, kv, c_q, c_k)


def _fox_s_kernel(pt_ref, q_ref, ccol_ref, cnt_ref, kn_ref, vn_ref, *rest, pages, steps, grp, scale):
    k_refs = rest[0:pages]
    v_refs = rest[pages:2 * pages]
    lf_refs = rest[2 * pages:3 * pages]
    o_ref, qbd_ref, m_ref, l_ref, acc_ref, carry_ref = rest[3 * pages:]
    p = pl.program_id(1)
    rows, hd = q_ref.shape[1], q_ref.shape[2]
    heads = rows // steps
    kvh = heads // grp
    psz = cnt_ref.shape[2]
    row = lax.broadcasted_iota(jnp.int32, (rows, hd), 0)
    row_kvh = (row % heads) // grp

    def page_update(k16, v16, neg_ck, mask):
        s = _dot_nt(qbd_ref[...], k16) * scale
        s = s + (ccol_ref[0] + jnp.concatenate([neg_ck] * steps, axis=0))
        if mask is not None:
            s = jnp.where(mask, s, NEG_INF)
        m_old = m_ref[...]
        m_new = jnp.maximum(m_old, jnp.max(s, axis=-1, keepdims=True))
        a = jnp.exp(m_old - m_new)
        pr = jnp.exp(s - m_new)
        l_ref[...] = a * l_ref[...] + jnp.sum(pr, axis=-1, keepdims=True)
        acc_ref[...] = a * acc_ref[...] + _dot(pr.astype(BF16), v16)
        m_ref[...] = m_new

    @pl.when(p == 0)
    def _():
        q = q_ref[0]
        for h in range(kvh):
            qbd_ref[:, h * hd:(h + 1) * hd] = jnp.where(row_kvh == h, q, jnp.zeros_like(q))
        m_ref[...] = jnp.full(m_ref.shape, NEG_INF, F32)
        l_ref[...] = jnp.zeros(l_ref.shape, F32)
        acc_ref[...] = jnp.zeros(acc_ref.shape, F32)
        carry_ref[...] = jnp.zeros(carry_ref.shape, F32)
        pad = jnp.zeros((psz - kn_ref.shape[1], kn_ref.shape[2]), F32)
        k16 = jnp.concatenate([kn_ref[0], pad], axis=0).astype(BF16)
        v16 = jnp.concatenate([vn_ref[0], pad], axis=0).astype(BF16)
        t = lax.broadcasted_iota(jnp.int32, (rows, psz), 0) // heads
        j = lax.broadcasted_iota(jnp.int32, (rows, psz), 1)
        page_update(k16, v16, -cnt_ref[0], j <= t)

    @pl.when(p > 0)
    def _():
        r = lax.broadcasted_iota(jnp.int32, (psz, psz), 0)
        c = lax.broadcasted_iota(jnp.int32, (psz, psz), 1)
        later = (r > c).astype(F32)
        for i in range(pages):
            lft = lf_refs[i][0]
            neg_ck = jnp.dot(lft, later, precision=lax.Precision.HIGHEST,
                             preferred_element_type=F32) + carry_ref[...]
            carry_ref[...] = carry_ref[...] + jnp.sum(lft, axis=1, keepdims=True)
            page_update(k_refs[i][0].astype(BF16), v_refs[i][0].astype(BF16), neg_ck, None)

    @pl.when(p == pl.num_programs(1) - 1)
    def _():
        out = jnp.zeros((rows, hd), F32)
        for h in range(kvh):
            out = out + jnp.where(row_kvh == h, acc_ref[:, h * hd:(h + 1) * hd], 0.0)
        o_ref[0] = (out / l_ref[...]).astype(BF16)


def fox_sample(page_table, q16, c_col, c_new_t, k_new, v_new, pool_k, pool_v, pool_lft, steps):
    bd, rows, hd = q16.shape
    heads = rows // steps
    kw = pool_k.shape[2]
    n_pages = page_table.shape[1]
    pages = _pick(n_pages, 4, 2, 1)
    n_steps = n_pages // pages + 1

    def page_idx(i):
        return lambda b, p, pt: (pt[b, n_pages - 1 - (jnp.maximum(p - 1, 0) * pages + i)], 0, 0)

    per_b = lambda a: pl.BlockSpec((1,) + a.shape[1:], lambda b, p, pt: (b,) + (0,) * (a.ndim - 1))
    kv_specs = [pl.BlockSpec((1, PAGE_SIZE, kw), page_idx(i)) for i in range(pages)]
    lf_specs = [pl.BlockSpec((1, heads, PAGE_SIZE), page_idx(i)) for i in range(pages)]
    grid_spec = pltpu.PrefetchScalarGridSpec(
        num_scalar_prefetch=1,
        grid=(bd, n_steps),
        in_specs=[per_b(q16), per_b(c_col), per_b(c_new_t), per_b(k_new), per_b(v_new)]
        + kv_specs + kv_specs + lf_specs,
        out_specs=pl.BlockSpec((1, rows, hd), lambda b, p, pt: (b, 0, 0)),
        scratch_shapes=[pltpu.VMEM((rows, kw), BF16), pltpu.VMEM((rows, 1), F32),
                        pltpu.VMEM((rows, 1), F32), pltpu.VMEM((rows, kw), F32),
                        pltpu.VMEM((heads, 1), F32)],
    )
    return pl.pallas_call(
        functools.partial(_fox_s_kernel, pages=pages, steps=steps, grp=heads // FOX_KV_HEADS,
                          scale=FOX_HEAD_DIM ** -0.5),
        grid_spec=grid_spec,
        out_shape=jax.ShapeDtypeStruct((bd, rows, hd), BF16),
        compiler_params=_params(("arbitrary", "arbitrary")),
        name="fox_sample",
    )(page_table, q16, c_col, c_new_t, k_new, v_new,
      *([pool_k] * pages), *([pool_v] * pages), *([pool_lft] * pages))


def _time_major(a):
    return jnp.swapaxes(a, 0, 1).reshape((a.shape[0] * a.shape[1],) + a.shape[2:])


def _batch_major(a, bd):
    return jnp.swapaxes(a.reshape((a.shape[0] // bd, bd) + a.shape[1:]), 0, 1)


def _pad_rows(a, rows):
    return jnp.pad(a, ((0, 0), (0, rows - a.shape[1]), (0, 0)))


def kernel(x_prompt, x_sample, state_conv, cache_win_k, cache_win_v, cache_k, cache_v, cache_logf,
           state_ffn, page_table, w_in_a, conv_a, w_out_a, w_qkv_b, sinks_b, w_o_b, w_qkvf_c, b_f_c,
           w_o_c, ln1_g, ln1_b, w_up, conv_f, w_down, ln2_g, ln2_b):
    batch, seq, d = x_prompt.shape
    bd, steps, _ = x_sample.shape
    depth = ln1_g.shape[0]
    alpha = (2 * depth) ** 0.25
    n_mixers = 3
    f = w_down.shape[1]

    xp = x_prompt.reshape(batch * seq, d)
    xp16 = xp.astype(BF16)
    xs = _time_major(x_sample)
    xs16 = xs.astype(BF16)

    conv_p, conv_s, wk_p, wv_p, wk_s, wv_s = [], [], [], [], [], []
    fk_p, fv_p, fl_p, fk_s, fv_s, fl_s = [], [], [], [], [], []
    ffn_p, ffn_s = [], []

    for i in range(depth):
        mix, j = i % n_mixers, i // n_mixers
        if mix == 0:
            zs, st_s, wb16, wc16, wh16 = gate_conv_sample(xs16, w_in_a[j], conv_a[j],
                                                          _time_major(state_conv[j]), bd)
            ys, wo16 = mm_sample(zs, w_out_a[j], d, F32, res=xs, alpha=alpha)
            zp, st_p = gate_conv_prompt(xp16, wb16, wc16, wh16, conv_a[j], batch)
            yp = mm_prompt(zp, wo16, d, 0, F32, res=xp, alpha=alpha)
            conv_p.append(st_p)
            conv_s.append(_batch_major(st_s, bd))
        elif mix == 1:
            hd, kvh = SW_HEAD_DIM, SW_KV_HEADS
            kw = kvh * hd
            grp = d // kw
            qkv_s, w16 = mm_sample(xs16, w_qkv_b[j], d + 2 * kw, F32)
            qp = mm_prompt(xp16, w16, d, 0, BF16)
            kvp = mm_prompt(xp16, w16, 2 * kw, d, F32)
            op = swa_prompt(qp, kvp, sinks_b[j], batch)
            qs = qkv_s[:, :d].astype(BF16).reshape(steps, bd, kvh, grp, hd)
            qs = qs.transpose(1, 2, 3, 0, 4).reshape(bd, kvh, grp * steps, hd)
            kn = _batch_major(qkv_s[:, d:d + kw], bd)
            vn = _batch_major(qkv_s[:, d + kw:], bd)
            ck = cache_win_k[j].reshape(bd, -1, kw)
            cv = cache_win_v[j].reshape(bd, -1, kw)
            sink_col = jnp.repeat(sinks_b[j].reshape(kvh, grp), steps, axis=1)[..., None]
            os_ = swa_sample(qs, ck, cv, _pad_rows(kn, 8), _pad_rows(vn, 8), sink_col, steps)
            os_ = os_.reshape(bd, kvh, grp, steps, hd).transpose(3, 0, 1, 2, 4).reshape(steps * bd, d)
            ys, wo16 = mm_sample(os_.astype(BF16), w_o_b[j], d, F32, res=xs, alpha=alpha)
            yp = mm_prompt(op, wo16, d, 0, F32, res=xp, alpha=alpha)
            keep = min(WINDOW, seq)
            wk_p.append(kvp[:, :kw].reshape(batch, seq, kvh, hd)[:, seq - keep:])
            wv_p.append(kvp[:, kw:].reshape(batch, seq, kvh, hd)[:, seq - keep:])
            wb = ck.shape[1]
            wk_s.append(jnp.concatenate([ck, kn], axis=1)[:, -wb:].reshape(bd, wb, kvh, hd))
            wv_s.append(jnp.concatenate([cv, vn], axis=1)[:, -wb:].reshape(bd, wb, kvh, hd))
        else:
            hd, kvh = FOX_HEAD_DIM, FOX_KV_HEADS
            kw = kvh * hd
            heads = d // hd
            grp = heads // kvh
            w = w_qkvf_c[j]
            wf = w[:, d + 2 * kw:]
            qkv_s, w16 = mm_sample(xs16, w, d + 2 * kw, F32)
            lf_s, c_s = fox_logf_sample(xs16, wf, b_f_c[j], bd)
            qp = mm_prompt(xp16, w16, d, 0, BF16)
            kvp = mm_prompt(xp16, w16, 2 * kw, d, F32)
            lf_p, c_p = fox_logf_prompt(xp16, wf, b_f_c[j], batch)
            c4 = c_p.reshape(batch, seq, kvh, grp)
            op = fox_prompt(qp, kvp, c4.transpose(0, 2, 1, 3), c4.transpose(0, 2, 3, 1), batch)
            qs = _batch_major(qkv_s[:, :d].astype(BF16), bd).reshape(bd, steps * heads, hd)
            kn = _batch_major(qkv_s[:, d:d + kw], bd)
            vn = _batch_major(qkv_s[:, d + kw:], bd)
            c_b = _batch_major(c_s, bd)
            c_col = c_b.reshape(bd, steps * heads, 1)
            c_new_t = jnp.pad(c_b.transpose(0, 2, 1), ((0, 0), (0, 0), (0, PAGE_SIZE - steps)))
            n_pool = cache_k.shape[1]
            os_ = fox_sample(page_table, qs, c_col, c_new_t, _pad_rows(kn, 8), _pad_rows(vn, 8),
                             cache_k[j].reshape(n_pool, PAGE_SIZE, kw),
                             cache_v[j].reshape(n_pool, PAGE_SIZE, kw),
                             cache_logf[j].transpose(0, 2, 1), steps)
            os_ = _time_major(os_.reshape(bd, steps, d))
            ys, wo16 = mm_sample(os_, w_o_c[j], d, F32, res=xs, alpha=alpha)
            yp = mm_prompt(op, wo16, d, 0, F32, res=xp, alpha=alpha)
            fk_p.append(kvp[:, :kw].reshape(batch, seq, kvh, hd))
            fv_p.append(kvp[:, kw:].reshape(batch, seq, kvh, hd))
            fl_p.append(lf_p.reshape(batch, seq, heads))
            fk_s.append(kn.reshape(bd, steps, kvh, hd))
            fv_s.append(vn.reshape(bd, steps, kvh, hd))
            fl_s.append(_batch_major(lf_s, bd))
        xs, xs16 = layer_norm_rows(ys, ln1_g[i], ln1_b[i])
        xp, xp16 = layer_norm_rows(yp, ln1_g[i], ln1_b[i])

        a_s, sg_s, su_s, wg16, wu16 = ffn_up_sample(xs16, w_up[i], conv_f[i],
                                                    _time_major(state_ffn[i]), bd)
        ys, wd16 = mm_sample(a_s, w_down[i], d, F32, res=xs, alpha=alpha)
        a_p, sg_p, su_p = ffn_up_prompt(xp16, wg16, wu16, conv_f[i], batch)
        yp = mm_prompt(a_p, wd16, d, 0, F32, res=xp, alpha=alpha)
        ffn_p.append(jnp.concatenate([sg_p, su_p], axis=-1))
        ffn_s.append(_batch_major(jnp.concatenate([sg_s, su_s], axis=-1), bd))
        xs, xs16 = layer_norm_rows(ys, ln2_g[i], ln2_b[i])
        xp, xp16 = layer_norm_rows(yp, ln2_g[i], ln2_b[i])

    return (xp.reshape(batch, seq, d), _batch_major(xs, bd),
            jnp.stack(conv_p), jnp.stack(conv_s), jnp.stack(wk_p), jnp.stack(wv_p),
            jnp.stack(wk_s), jnp.stack(wv_s), jnp.stack(fk_p), jnp.stack(fv_p), jnp.stack(fl_p),
            jnp.stack(fk_s), jnp.stack(fv_s), jnp.stack(fl_s), jnp.stack(ffn_p), jnp.stack(ffn_s))
```

```python
import functools
import math

import jax
import jax.numpy as jnp
from jax import lax
from jax.experimental import pallas as pl
from jax.experimental.pallas import tpu as pltpu

F32 = jnp.float32
BF16 = jnp.bfloat16

CONV_TAPS = 3
CONV_PREV = CONV_TAPS - 1
SW_HEAD_DIM = 64
SW_KV_HEADS = 8
WINDOW = 128
FOX_HEAD_DIM = 128
FOX_KV_HEADS = 8
PAGE_SIZE = 128
LN_EPS = 1e-5
NEG_INF = -1e30
LOG2_E = math.log2(math.e)

V7X_VMEM_BYTES = 64 * 1024 * 1024
VMEM_LIMIT = V7X_VMEM_BYTES - 8 * 1024 * 1024
LANE = 128


def _params(semantics):
    return pltpu.CompilerParams(dimension_semantics=semantics, vmem_limit_bytes=VMEM_LIMIT)


def _dot(a, b):
    return jnp.dot(a, b, preferred_element_type=F32)


def _dot_nt(a, b):
    return lax.dot_general(a, b, (((1,), (1,)), ((), ())), preferred_element_type=F32)


def _pick(n, *cands):
    for c in cands:
        if n % c == 0:
            return c
    raise ValueError(f"no tile for {n} among {cands}")


def _conv_rows(u, prev, cw, shift):
    rows = u.shape[0]
    if prev is None:
        assert shift == 1
        row = lax.broadcasted_iota(jnp.int32, u.shape, 0)
        u1 = jnp.where(row >= 1, pltpu.roll(u, 1, 0), 0.0)
        u2 = jnp.where(row >= 2, pltpu.roll(u, 2, 0), 0.0)
        tail = u[rows - CONV_PREV:rows]
    else:
        full = jnp.concatenate([prev, u], axis=0)
        u2 = full[0:rows]
        u1 = full[shift:shift + rows]
        tail = full[rows:rows + CONV_PREV * shift]
    y = cw[0:1, :] * u2 + cw[1:2, :] * u1 + cw[2:3, :] * u
    return y, tail


def _gate_conv_core(x, wb, wc, wh, cw, prev, shift):
    bg = _dot(x, wb)
    c = _dot(x, wc)
    h = _dot(x, wh)
    y, tail = _conv_rows(c * h, prev, cw, shift)
    return (bg * y).astype(BF16), tail


def _gate_conv_p_kernel(x_ref, wb_ref, wc_ref, wh_ref, cw_ref, z_ref, st_ref):
    z, tail = _gate_conv_core(x_ref[...], wb_ref[...], wc_ref[...], wh_ref[...], cw_ref[...], None, 1)
    z_ref[...] = z
    st_ref[0] = tail


def _gate_conv_s_kernel(x_ref, wb_ref, wc_ref, wh_ref, cw_ref, prev_ref,
                        z_ref, st_ref, wb16_ref, wc16_ref, wh16_ref, *, shift):
    wb = wb_ref[...].astype(BF16)
    wc = wc_ref[...].astype(BF16)
    wh = wh_ref[...].astype(BF16)
    wb16_ref[...] = wb
    wc16_ref[...] = wc
    wh16_ref[...] = wh
    z, tail = _gate_conv_core(x_ref[...], wb, wc, wh, cw_ref[...], prev_ref[...], shift)
    z_ref[...] = z
    st_ref[...] = tail


def gate_conv_sample(x16, w_in, layer, cw, prev, shift):
    rows, d = x16.shape
    c = w_in.shape[2] // 3
    tn = _pick(c, 256, 128)
    nc = c // tn
    w_spec = lambda sec: pl.BlockSpec((None, d, tn), lambda n, sec=sec: (layer, 0, sec * nc + n))
    col = lambda r: pl.BlockSpec((r, tn), lambda n: (0, n))
    return pl.pallas_call(
        functools.partial(_gate_conv_s_kernel, shift=shift),
        grid=(nc,),
        in_specs=[pl.BlockSpec((rows, d), lambda n: (0, 0)), w_spec(0), w_spec(1), w_spec(2),
                  col(CONV_TAPS), col(CONV_PREV * shift)],
        out_specs=[col(rows), col(CONV_PREV * shift), col(d), col(d), col(d)],
        out_shape=[jax.ShapeDtypeStruct((rows, c), BF16),
                   jax.ShapeDtypeStruct((CONV_PREV * shift, c), F32),
                   jax.ShapeDtypeStruct((d, c), BF16),
                   jax.ShapeDtypeStruct((d, c), BF16),
                   jax.ShapeDtypeStruct((d, c), BF16)],
        compiler_params=_params(("arbitrary",)),
        name="gate_conv_sample",
    )(x16, w_in, w_in, w_in, cw, prev)


def gate_conv_prompt(x16, wb16, wc16, wh16, cw, batch):
    m, d = x16.shape
    s = m // batch
    c = wb16.shape[1]
    tn = _pick(c, 256, 128)
    w_spec = pl.BlockSpec((d, tn), lambda b, n: (0, n))
    return pl.pallas_call(
        _gate_conv_p_kernel,
        grid=(batch, c // tn),
        in_specs=[pl.BlockSpec((s, d), lambda b, n: (b, 0), pipeline_mode=pl.Buffered(1)),
                  w_spec, w_spec, w_spec,
                  pl.BlockSpec((CONV_TAPS, tn), lambda b, n: (0, n))],
        out_specs=[pl.BlockSpec((s, tn), lambda b, n: (b, n)),
                   pl.BlockSpec((1, CONV_PREV, tn), lambda b, n: (b, 0, n))],
        out_shape=[jax.ShapeDtypeStruct((m, c), BF16),
                   jax.ShapeDtypeStruct((batch, CONV_PREV, c), F32)],
        compiler_params=_params(("arbitrary", "arbitrary")),
        name="gate_conv_prompt",
    )(x16, wb16, wc16, wh16, cw)


def _ffn_up_core(x, wg, wu, cwg, cwu, prev_g, prev_u, shift):
    cg, tail_g = _conv_rows(_dot(x, wg), prev_g, cwg, shift)
    cu, tail_u = _conv_rows(_dot(x, wu), prev_u, cwu, shift)
    act = (cg * (1.0 / (1.0 + jnp.exp(-cg)))) * cu
    return act.astype(BF16), tail_g, tail_u


def _ffn_up_p_kernel(x_ref, wg_ref, wu_ref, cwg_ref, cwu_ref, a_ref, sg_ref, su_ref):
    act, tg, tu = _ffn_up_core(x_ref[...], wg_ref[...], wu_ref[...], cwg_ref[...], cwu_ref[...],
                               None, None, 1)
    a_ref[...] = act
    sg_ref[0] = tg
    su_ref[0] = tu


def _ffn_up_s_kernel(x_ref, wg_ref, wu_ref, cwg_ref, cwu_ref, pg_ref, pu_ref,
                     a_ref, sg_ref, su_ref, wg16_ref, wu16_ref, *, shift):
    wg = wg_ref[...].astype(BF16)
    wu = wu_ref[...].astype(BF16)
    wg16_ref[...] = wg
    wu16_ref[...] = wu
    act, tg, tu = _ffn_up_core(x_ref[...], wg, wu, cwg_ref[...], cwu_ref[...],
                               pg_ref[...], pu_ref[...], shift)
    a_ref[...] = act
    sg_ref[...] = tg
    su_ref[...] = tu


def ffn_up_sample(x16, w_up, layer, cw, prev, shift):
    rows, d = x16.shape
    f = w_up.shape[2] // 2
    tn = _pick(f, 256, 128)
    nf = f // tn
    sec = lambda r, k: pl.BlockSpec((r, tn), lambda n, k=k: (0, k * nf + n))
    wsec = lambda k: pl.BlockSpec((None, d, tn), lambda n, k=k: (layer, 0, k * nf + n))
    col = lambda r: pl.BlockSpec((r, tn), lambda n: (0, n))
    pr = CONV_PREV * shift
    return pl.pallas_call(
        functools.partial(_ffn_up_s_kernel, shift=shift),
        grid=(nf,),
        in_specs=[pl.BlockSpec((rows, d), lambda n: (0, 0)), wsec(0), wsec(1),
                  sec(CONV_TAPS, 0), sec(CONV_TAPS, 1), sec(pr, 0), sec(pr, 1)],
        out_specs=[col(rows), col(pr), col(pr), col(d), col(d)],
        out_shape=[jax.ShapeDtypeStruct((rows, f), BF16),
                   jax.ShapeDtypeStruct((pr, f), F32),
                   jax.ShapeDtypeStruct((pr, f), F32),
                   jax.ShapeDtypeStruct((d, f), BF16),
                   jax.ShapeDtypeStruct((d, f), BF16)],
        compiler_params=_params(("arbitrary",)),
        name="ffn_up_sample",
    )(x16, w_up, w_up, cw, cw, prev, prev)


def ffn_up_prompt(x16, wg16, wu16, cw, batch):
    m, d = x16.shape
    s = m // batch
    f = wg16.shape[1]
    tn = _pick(f, 256, 128)
    nf = f // tn
    w_spec = pl.BlockSpec((d, tn), lambda b, n: (0, n))
    st_spec = pl.BlockSpec((1, CONV_PREV, tn), lambda b, n: (b, 0, n))
    return pl.pallas_call(
        _ffn_up_p_kernel,
        grid=(batch, nf),
        in_specs=[pl.BlockSpec((s, d), lambda b, n: (b, 0), pipeline_mode=pl.Buffered(1)),
                  w_spec, w_spec,
                  pl.BlockSpec((CONV_TAPS, tn), lambda b, n: (0, n)),
                  pl.BlockSpec((CONV_TAPS, tn), lambda b, n: (0, nf + n))],
        out_specs=[pl.BlockSpec((s, tn), lambda b, n: (b, n)), st_spec, st_spec],
        out_shape=[jax.ShapeDtypeStruct((m, f), BF16),
                   jax.ShapeDtypeStruct((batch, CONV_PREV, f), F32),
                   jax.ShapeDtypeStruct((batch, CONV_PREV, f), F32)],
        compiler_params=_params(("arbitrary", "arbitrary")),
        name="ffn_up_prompt",
    )(x16, wg16, wu16, cw, cw)


def _mm_p_kernel(*refs, alpha, has_res):
    if has_res:
        x_ref, w_ref, r_ref, o_ref = refs
    else:
        x_ref, w_ref, o_ref = refs
    acc = _dot(x_ref[...], w_ref[...])
    if has_res:
        acc = alpha * r_ref[...] + acc
    o_ref[...] = acc.astype(o_ref.dtype)


def _mm_s_kernel(*refs, alpha, has_res):
    if has_res:
        x_ref, w_ref, r_ref, o_ref, w16_ref = refs
    else:
        x_ref, w_ref, o_ref, w16_ref = refs
    w = w_ref[...].astype(BF16)
    w16_ref[...] = w
    acc = _dot(x_ref[...], w)
    if has_res:
        acc = alpha * r_ref[...] + acc
    o_ref[...] = acc.astype(o_ref.dtype)


def mm_sample(x16, w, layer, n_out, out_dtype, res=None, alpha=None):
    rows, k = x16.shape
    tn = _pick(n_out, 512, 256, 128) if k <= 4096 else _pick(n_out, 256, 128)
    col = lambda r: pl.BlockSpec((r, tn), lambda n: (0, n))
    in_specs = [pl.BlockSpec((rows, k), lambda n: (0, 0)),
                pl.BlockSpec((None, k, tn), lambda n: (layer, 0, n))]
    args = [x16, w]
    if res is not None:
        in_specs.append(col(rows))
        args.append(res)
    return pl.pallas_call(
        functools.partial(_mm_s_kernel, alpha=alpha, has_res=res is not None),
        grid=(n_out // tn,),
        in_specs=in_specs,
        out_specs=[col(rows), col(k)],
        out_shape=[jax.ShapeDtypeStruct((rows, n_out), out_dtype),
                   jax.ShapeDtypeStruct((k, n_out), BF16)],
        compiler_params=_params(("arbitrary",)),
        name="mm_sample",
    )(*args)


def mm_prompt(x16, w16, n_out, col_off, out_dtype, res=None, alpha=None):
    m, k = x16.shape
    if k <= 4096:
        tm, tn = _pick(m, 1024, 512, 256), _pick(n_out, 512, 256, 128)
    else:
        tm, tn = _pick(m, 512, 256), _pick(n_out, 512, 256, 128)
    assert col_off % tn == 0
    off = col_off // tn
    in_specs = [pl.BlockSpec((tm, k), lambda i, n: (i, 0)),
                pl.BlockSpec((k, tn), lambda i, n: (0, off + n))]
    args = [x16, w16]
    if res is not None:
        in_specs.append(pl.BlockSpec((tm, tn), lambda i, n: (i, n)))
        args.append(res)
    return pl.pallas_call(
        functools.partial(_mm_p_kernel, alpha=alpha, has_res=res is not None),
        grid=(m // tm, n_out // tn),
        in_specs=in_specs,
        out_specs=pl.BlockSpec((tm, tn), lambda i, n: (i, n)),
        out_shape=jax.ShapeDtypeStruct((m, n_out), out_dtype),
        compiler_params=_params(("arbitrary", "arbitrary")),
        name="mm_prompt",
    )(*args)


def _ln_kernel(y_ref, g_ref, b_ref, xf_ref, xb_ref):
    y = y_ref[...]
    mu = jnp.mean(y, axis=-1, keepdims=True)
    dev = y - mu
    var = jnp.mean(dev * dev, axis=-1, keepdims=True)
    out = dev * lax.rsqrt(var + LN_EPS) * g_ref[...] + b_ref[...]
    xf_ref[...] = out
    xb_ref[...] = out.astype(BF16)


def layer_norm_rows(y, g, b):
    m, d = y.shape
    tr = _pick(m, 256, 128)
    row = pl.BlockSpec((tr, d), lambda i: (i, 0))
    vec = pl.BlockSpec((1, d), lambda i: (0, 0))
    return pl.pallas_call(
        _ln_kernel,
        grid=(m // tr,),
        in_specs=[row, vec, vec],
        out_specs=[row, row],
        out_shape=[jax.ShapeDtypeStruct((m, d), F32), jax.ShapeDtypeStruct((m, d), BF16)],
        compiler_params=_params(("arbitrary",)),
        name="layer_norm",
    )(y, g.reshape(1, d), b.reshape(1, d))


def _sink_softmax_pv(s, sink, v16):
    m = jnp.maximum(jnp.max(s, axis=-1, keepdims=True), sink)
    p = jnp.exp(s - m)
    den = jnp.sum(p, axis=-1, keepdims=True) + jnp.exp(sink - m)
    return _dot((p / den).astype(BF16), v16)


def _swa_p_kernel(q_ref, kp_ref, ko_ref, vp_ref, vo_ref, sink_ref, o_ref, *, grp, scale):
    blk = q_ref.shape[0]
    first = pl.program_id(1) == 0
    kcat = jnp.concatenate([kp_ref[...], ko_ref[...]], axis=0).astype(BF16)
    vcat = jnp.concatenate([vp_ref[...], vo_ref[...]], axis=0).astype(BF16)
    tl = lax.broadcasted_iota(jnp.int32, (blk, 2 * blk), 0)
    j = lax.broadcasted_iota(jnp.int32, (blk, 2 * blk), 1)
    mask = (j >= tl) & (j <= tl + WINDOW) & ((j >= blk) | jnp.logical_not(first))
    hd = SW_HEAD_DIM
    for h in range(SW_KV_HEADS):
        kh = kcat[:, h * hd:(h + 1) * hd]
        vh = vcat[:, h * hd:(h + 1) * hd]
        for g in range(grp):
            head = h * grp + g
            s = _dot_nt(q_ref[:, head * hd:(head + 1) * hd], kh) * scale
            s = jnp.where(mask, s, NEG_INF)
            o = _sink_softmax_pv(s, sink_ref[head], vh)
            o_ref[:, head * hd:(head + 1) * hd] = o.astype(BF16)


def swa_prompt(q16, kv, sinks, batch):
    m, qw = q16.shape
    s = m // batch
    blk = WINDOW
    nb = s // blk
    kw = SW_KV_HEADS * SW_HEAD_DIM
    grp = qw // kw
    own = lambda c: pl.BlockSpec((blk, kw), lambda b, i, c=c: (b * nb + i, c))
    prev = lambda c: pl.BlockSpec((blk, kw), lambda b, i, c=c: (b * nb + jnp.maximum(i - 1, 0), c))
    return pl.pallas_call(
        functools.partial(_swa_p_kernel, grp=grp, scale=SW_HEAD_DIM ** -0.5),
        grid=(batch, nb),
        in_specs=[pl.BlockSpec((blk, qw), lambda b, i: (b * nb + i, 0)),
                  prev(0), own(0), prev(1), own(1),
                  pl.BlockSpec(memory_space=pltpu.SMEM)],
        out_specs=pl.BlockSpec((blk, qw), lambda b, i: (b * nb + i, 0)),
        out_shape=jax.ShapeDtypeStruct((m, qw), BF16),
        compiler_params=_params(("arbitrary", "arbitrary")),
        name="swa_prompt",
    )(q16, kv, kv, kv, kv, sinks)


def _swa_s_kernel(q_ref, ck_ref, cv_ref, kn_ref, vn_ref, sink_ref, o_ref, *, steps, scale):
    wb = ck_ref.shape[1]
    pad = jnp.zeros((wb - kn_ref.shape[1], ck_ref.shape[2]), F32)
    kcat = jnp.concatenate([ck_ref[0], kn_ref[0], pad], axis=0).astype(BF16)
    vcat = jnp.concatenate([cv_ref[0], vn_ref[0], pad], axis=0).astype(BF16)
    rows = q_ref.shape[2]
    t = lax.broadcasted_iota(jnp.int32, (rows, 2 * wb), 0) % steps
    j = lax.broadcasted_iota(jnp.int32, (rows, 2 * wb), 1)
    mask = (j >= t + wb - WINDOW) & (j <= t + wb)
    hd = SW_HEAD_DIM
    for h in range(SW_KV_HEADS):
        s = _dot_nt(q_ref[0, h], kcat[:, h * hd:(h + 1) * hd]) * scale
        s = jnp.where(mask, s, NEG_INF)
        o_ref[0, h] = _sink_softmax_pv(s, sink_ref[h], vcat[:, h * hd:(h + 1) * hd])


def swa_sample(q16, cache_k, cache_v, k_new, v_new, sink_col, steps):
    bd, kvh, rows, hd = q16.shape
    wb, kw = cache_k.shape[1], cache_k.shape[2]
    per_b = lambda a: pl.BlockSpec((1,) + a.shape[1:], lambda b: (b,) + (0,) * (a.ndim - 1))
    return pl.pallas_call(
        functools.partial(_swa_s_kernel, steps=steps, scale=SW_HEAD_DIM ** -0.5),
        grid=(bd,),
        in_specs=[per_b(q16), per_b(cache_k), per_b(cache_v), per_b(k_new), per_b(v_new),
                  pl.BlockSpec(sink_col.shape, lambda b: (0, 0, 0))],
        out_specs=pl.BlockSpec((1, kvh, rows, hd), lambda b: (b, 0, 0, 0)),
        out_shape=jax.ShapeDtypeStruct((bd, kvh, rows, hd), F32),
        compiler_params=_params(("arbitrary",)),
        name="swa_sample",
    )(q16, cache_k, cache_v, k_new, v_new, sink_col)


def _log_sigmoid(z):
    return jnp.minimum(z, 0.0) - jnp.log1p(jnp.exp(-jnp.abs(z)))


def _fox_logf_p_kernel(x_ref, wf_ref, bf_ref, lf_ref, c_ref, *, blk):
    lf = _log_sigmoid(_dot(x_ref[...], wf_ref[...].astype(BF16)) + bf_ref[...])
    lf_ref[...] = lf
    r = lax.broadcasted_iota(jnp.int32, (blk, blk), 0)
    c = lax.broadcasted_iota(jnp.int32, (blk, blk), 1)
    tri = (c <= r).astype(F32)
    carry = jnp.zeros((1, lf.shape[1]), F32)
    for i in range(lf.shape[0] // blk):
        cb = jnp.dot(tri, lf[i * blk:(i + 1) * blk], precision=lax.Precision.HIGHEST,
                     preferred_element_type=F32) + carry
        c_ref[i * blk:(i + 1) * blk, :] = cb
        carry = cb[blk - 1:blk, :]


def fox_logf_prompt(x16, wf, bf, batch):
    m, d = x16.shape
    s = m // batch
    h = wf.shape[1]
    row = pl.BlockSpec((s, h), lambda b: (b, 0))
    return pl.pallas_call(
        functools.partial(_fox_logf_p_kernel, blk=_pick(s, 256, 128)),
        grid=(batch,),
        in_specs=[pl.BlockSpec((s, d), lambda b: (b, 0)),
                  pl.BlockSpec((d, h), lambda b: (0, 0)),
                  pl.BlockSpec((1, h), lambda b: (0, 0))],
        out_specs=[row, row],
        out_shape=[jax.ShapeDtypeStruct((m, h), F32), jax.ShapeDtypeStruct((m, h), F32)],
        compiler_params=_params(("arbitrary",)),
        name="fox_logf_prompt",
    )(x16, wf, bf.reshape(1, h))


def _fox_logf_s_kernel(x_ref, wf_ref, bf_ref, lf_ref, c_ref, *, shift):
    lf = _log_sigmoid(_dot(x_ref[...], wf_ref[...].astype(BF16)) + bf_ref[...])
    lf_ref[...] = lf
    run = lf[0:shift]
    c_ref[0:shift, :] = run
    for t in range(1, lf.shape[0] // shift):
        run = run + lf[t * shift:(t + 1) * shift]
        c_ref[t * shift:(t + 1) * shift, :] = run


def fox_logf_sample(x16, wf, bf, shift):
    rows, d = x16.shape
    h = wf.shape[1]
    full = lambda a, b: pl.BlockSpec((a, b), lambda i: (0, 0))
    return pl.pallas_call(
        functools.partial(_fox_logf_s_kernel, shift=shift),
        grid=(1,),
        in_specs=[full(rows, d), full(d, h), full(1, h)],
        out_specs=[full(rows, h), full(rows, h)],
        out_shape=[jax.ShapeDtypeStruct((rows, h), F32), jax.ShapeDtypeStruct((rows, h), F32)],
        compiler_params=_params(("arbitrary",)),
        name="fox_logf_sample",
    )(x16, wf, bf.reshape(1, h))


def _fox_p_kernel(q_ref, k_ref, v_ref, cq_ref, ck_ref, o_ref, k16_ref, v16_ref, m_ref, l_ref, acc_ref,
                  *, grp, pre, rest):
    qi = pl.program_id(2)
    tq = q_ref.shape[0]
    tk = tq
    hd = FOX_HEAD_DIM
    rows = grp * tq
    c_exp = rest * LOG2_E

    @pl.when(qi == 0)
    def _():
        k16_ref[...] = k_ref[...].astype(BF16)
        v16_ref[...] = v_ref[...].astype(BF16)

    q = jnp.concatenate([q_ref[:, g * hd:(g + 1) * hd] for g in range(grp)], axis=0) * pre
    cq = jnp.concatenate([cq_ref[0, 0, :, g:g + 1] for g in range(grp)], axis=0) * (1.0 / rest)
    m_ref[...] = jnp.full(m_ref.shape, NEG_INF, F32)
    l_ref[...] = jnp.zeros(l_ref.shape, F32)
    acc_ref[...] = jnp.zeros(acc_ref.shape, F32)

    def update(j, diagonal):
        off = pl.multiple_of(j * tk, tk)
        t = _dot_nt(q, k16_ref[pl.ds(off, tk), :]) + cq
        ck = ck_ref[0, 0, j] * (1.0 / rest)
        t = (t.reshape(grp, tq, tk) - ck[:, None, :]).reshape(rows, tk)
        if diagonal:
            r = lax.broadcasted_iota(jnp.int32, (rows, tk), 0) % tq
            c = lax.broadcasted_iota(jnp.int32, (rows, tk), 1)
            t = jnp.where(c <= r, t, NEG_INF)
        m_old = m_ref[...]
        m_new = jnp.maximum(m_old, jnp.max(t, axis=-1, keepdims=True))
        a = jnp.exp2((m_old - m_new) * c_exp)
        p = jnp.exp2((t - m_new) * c_exp)
        l_ref[...] = a * l_ref[...] + jnp.sum(p, axis=-1, keepdims=True)
        acc_ref[...] = a * acc_ref[...] + _dot(p.astype(BF16), v16_ref[pl.ds(off, tk), :])
        m_ref[...] = m_new

    def below_diagonal(j, carry):
        update(j, False)
        return carry

    lax.fori_loop(0, qi, below_diagonal, 0)
    update(qi, True)
    out = acc_ref[...] / l_ref[...]
    for g in range(grp):
        o_ref[:, g * hd:(g + 1) * hd] = out[g * tq:(g + 1) * tq].astype(BF16)


def fox_prompt(q16, kv, c_q, c_k, batch):
    m, qw = q16.shape
    s = m // batch
    hd, kvh = FOX_HEAD_DIM, FOX_KV_HEADS
    grp = qw // (kvh * hd)
    nk, tq = c_k.shape[2], c_k.shape[4]
    nq = s // tq
    scale = hd ** -0.5
    pre = 2.0 ** math.floor(math.log2(scale))
    kv_spec = lambda part: pl.BlockSpec((s, hd), lambda b, h, qi, part=part: (b, part * kvh + h))
    return pl.pallas_call(
        functools.partial(_fox_p_kernel, grp=grp, pre=pre, rest=scale / pre),
        grid=(batch, kvh, nq),
        in_specs=[pl.BlockSpec((tq, grp * hd), lambda b, h, qi: (b * nq + qi, h)),
                  kv_spec(0), kv_spec(1),
                  pl.BlockSpec((1, 1, tq, grp), lambda b, h, qi: (b, h, qi, 0)),
                  pl.BlockSpec((1, 1, nk, grp, tq), lambda b, h, qi: (b, h, 0, 0, 0))],
        out_specs=pl.BlockSpec((tq, grp * hd), lambda b, h, qi: (b * nq + qi, h)),
        out_shape=jax.ShapeDtypeStruct((m, qw), BF16),
        scratch_shapes=[pltpu.VMEM((s, hd), BF16), pltpu.VMEM((s, hd), BF16),
                        pltpu.VMEM((grp * tq, 1), F32), pltpu.VMEM((grp * tq, 1), F32),
                        pltpu.VMEM((grp * tq, hd), F32)],
        compiler_params=_params(("arbitrary", "arbitrary", "arbitrary")),
        name="fox_prompt",
    )(q16, kv, kv, c_q, c_k)


def _fox_s_kernel(pt_ref, q_ref, ccol_ref, cnt_ref, kn_ref, vn_ref, *rest, pages, steps, grp, scale):
    k_refs = rest[0:pages]
    v_refs = rest[pages:2 * pages]
    lf_refs = rest[2 * pages:3 * pages]
    o_ref, qbd_ref, m_ref, l_ref, acc_ref, carry_ref = rest[3 * pages:]
    p = pl.program_id(1)
    rows, hd = q_ref.shape[1], q_ref.shape[2]
    heads = rows // steps
    kvh = heads // grp
    psz = cnt_ref.shape[2]
    row = lax.broadcasted_iota(jnp.int32, (rows, hd), 0)
    row_kvh = (row % heads) // grp

    def page_update(k16, v16, neg_ck, mask):
        s = _dot_nt(qbd_ref[...], k16) * scale
        s = s + (ccol_ref[0] + jnp.concatenate([neg_ck] * steps, axis=0))
        if mask is not None:
            s = jnp.where(mask, s, NEG_INF)
        m_old = m_ref[...]
        m_new = jnp.maximum(m_old, jnp.max(s, axis=-1, keepdims=True))
        a = jnp.exp(m_old - m_new)
        pr = jnp.exp(s - m_new)
        l_ref[...] = a * l_ref[...] + jnp.sum(pr, axis=-1, keepdims=True)
        acc_ref[...] = a * acc_ref[...] + _dot(pr.astype(BF16), v16)
        m_ref[...] = m_new

    @pl.when(p == 0)
    def _():
        q = q_ref[0]
        for h in range(kvh):
            qbd_ref[:, h * hd:(h + 1) * hd] = jnp.where(row_kvh == h, q, jnp.zeros_like(q))
        m_ref[...] = jnp.full(m_ref.shape, NEG_INF, F32)
        l_ref[...] = jnp.zeros(l_ref.shape, F32)
        acc_ref[...] = jnp.zeros(acc_ref.shape, F32)
        carry_ref[...] = jnp.zeros(carry_ref.shape, F32)
        pad = jnp.zeros((psz - kn_ref.shape[1], kn_ref.shape[2]), F32)
        k16 = jnp.concatenate([kn_ref[0], pad], axis=0).astype(BF16)
        v16 = jnp.concatenate([vn_ref[0], pad], axis=0).astype(BF16)
        t = lax.broadcasted_iota(jnp.int32, (rows, psz), 0) // heads
        j = lax.broadcasted_iota(jnp.int32, (rows, psz), 1)
        page_update(k16, v16, -cnt_ref[0], j <= t)

    def flat16(ref):
        return jnp.concatenate([ref[:, h, :] for h in range(kvh)], axis=1).astype(BF16)

    @pl.when(p > 0)
    def _():
        r = lax.broadcasted_iota(jnp.int32, (psz, 2 * psz), 0)
        c = lax.broadcasted_iota(jnp.int32, (psz, 2 * psz), 1)
        later_and_total = ((r > c) | (c >= psz)).astype(F32)
        carry = carry_ref[...]
        parts = []
        for i in range(pages):
            w = lax.dot_general(lf_refs[i][...], later_and_total, (((0,), (0,)), ((), ())),
                                precision=lax.Precision.HIGHEST, preferred_element_type=F32)
            parts.append(w[:, :psz] + carry)
            carry = carry + w[:, psz:psz + 1]
        carry_ref[...] = carry
        k16 = jnp.concatenate([flat16(k_refs[i]) for i in range(pages)], axis=0)
        v16 = jnp.concatenate([flat16(v_refs[i]) for i in range(pages)], axis=0)
        page_update(k16, v16, jnp.concatenate(parts, axis=1), None)

    @pl.when(p == pl.num_programs(1) - 1)
    def _():
        out = jnp.zeros((rows, hd), F32)
        for h in range(kvh):
            out = out + jnp.where(row_kvh == h, acc_ref[:, h * hd:(h + 1) * hd], 0.0)
        o_ref[0] = (out / l_ref[...]).astype(BF16)


def fox_sample(page_table, q16, c_col, c_new_t, k_new, v_new, pool_k, pool_v, pool_lf, layer, steps):
    bd, rows, hd = q16.shape
    heads = rows // steps
    psz, kvh = pool_k.shape[2], pool_k.shape[3]
    kw = kvh * hd
    n_pages = page_table.shape[1]
    pages = _pick(n_pages, 8, 4, 2, 1)
    n_steps = n_pages // pages + 1

    def page_idx(i, tail):
        return lambda b, p, pt: (layer, pt[b, n_pages - 1 - (jnp.maximum(p - 1, 0) * pages + i)]) + tail

    per_b = lambda a: pl.BlockSpec((1,) + a.shape[1:], lambda b, p, pt: (b,) + (0,) * (a.ndim - 1))
    kv_specs = lambda: [pl.BlockSpec((None, None, psz, kvh, hd), page_idx(i, (0, 0, 0))) for i in range(pages)]
    lf_specs = [pl.BlockSpec((None, None, psz, heads), page_idx(i, (0, 0))) for i in range(pages)]
    grid_spec = pltpu.PrefetchScalarGridSpec(
        num_scalar_prefetch=1,
        grid=(bd, n_steps),
        in_specs=[per_b(q16), per_b(c_col), per_b(c_new_t), per_b(k_new), per_b(v_new)]
        + kv_specs() + kv_specs() + lf_specs,
        out_specs=pl.BlockSpec((1, rows, hd), lambda b, p, pt: (b, 0, 0)),
        scratch_shapes=[pltpu.VMEM((rows, kw), BF16), pltpu.VMEM((rows, 1), F32),
                        pltpu.VMEM((rows, 1), F32), pltpu.VMEM((rows, kw), F32),
                        pltpu.VMEM((heads, 1), F32)],
    )
    return pl.pallas_call(
        functools.partial(_fox_s_kernel, pages=pages, steps=steps, grp=heads // FOX_KV_HEADS,
                          scale=FOX_HEAD_DIM ** -0.5),
        grid_spec=grid_spec,
        out_shape=jax.ShapeDtypeStruct((bd, rows, hd), BF16),
        compiler_params=_params(("arbitrary", "arbitrary")),
        name="fox_sample",
    )(page_table, q16, c_col, c_new_t, k_new, v_new,
      *([pool_k] * pages), *([pool_v] * pages), *([pool_lf] * pages))


def _time_major(a):
    return jnp.swapaxes(a, 0, 1).reshape((a.shape[0] * a.shape[1],) + a.shape[2:])


def _batch_major(a, bd):
    return jnp.swapaxes(a.reshape((a.shape[0] // bd, bd) + a.shape[1:]), 0, 1)


def _pad_rows(a, rows):
    return jnp.pad(a, ((0, 0), (0, rows - a.shape[1]), (0, 0)))


def kernel(x_prompt, x_sample, state_conv, cache_win_k, cache_win_v, cache_k, cache_v, cache_logf,
           state_ffn, page_table, w_in_a, conv_a, w_out_a, w_qkv_b, sinks_b, w_o_b, w_qkvf_c, b_f_c,
           w_o_c, ln1_g, ln1_b, w_up, conv_f, w_down, ln2_g, ln2_b):
    batch, seq, d = x_prompt.shape
    bd, steps, _ = x_sample.shape
    depth = ln1_g.shape[0]
    alpha = (2 * depth) ** 0.25
    n_mixers = 3
    f = w_down.shape[1]

    xp = x_prompt.reshape(batch * seq, d)
    xp16 = xp.astype(BF16)
    xs = _time_major(x_sample)
    xs16 = xs.astype(BF16)

    conv_p, conv_s, wk_p, wv_p, wk_s, wv_s = [], [], [], [], [], []
    fk_p, fv_p, fl_p, fk_s, fv_s, fl_s = [], [], [], [], [], []
    ffn_p, ffn_s = [], []

    for i in range(depth):
        mix, j = i % n_mixers, i // n_mixers
        if mix == 0:
            zs, st_s, wb16, wc16, wh16 = gate_conv_sample(xs16, w_in_a, j, conv_a[j],
                                                          _time_major(state_conv[j]), bd)
            ys, wo16 = mm_sample(zs, w_out_a, j, d, F32, res=xs, alpha=alpha)
            zp, st_p = gate_conv_prompt(xp16, wb16, wc16, wh16, conv_a[j], batch)
            yp = mm_prompt(zp, wo16, d, 0, F32, res=xp, alpha=alpha)
            conv_p.append(st_p)
            conv_s.append(_batch_major(st_s, bd))
        elif mix == 1:
            hd, kvh = SW_HEAD_DIM, SW_KV_HEADS
            kw = kvh * hd
            grp = d // kw
            qkv_s, w16 = mm_sample(xs16, w_qkv_b, j, d + 2 * kw, F32)
            qp = mm_prompt(xp16, w16, d, 0, BF16)
            kvp = mm_prompt(xp16, w16, 2 * kw, d, F32)
            op = swa_prompt(qp, kvp, sinks_b[j], batch)
            qs = qkv_s[:, :d].astype(BF16).reshape(steps, bd, kvh, grp, hd)
            qs = qs.transpose(1, 2, 3, 0, 4).reshape(bd, kvh, grp * steps, hd)
            kn = _batch_major(qkv_s[:, d:d + kw], bd)
            vn = _batch_major(qkv_s[:, d + kw:], bd)
            ck = cache_win_k[j].reshape(bd, -1, kw)
            cv = cache_win_v[j].reshape(bd, -1, kw)
            sink_col = jnp.repeat(sinks_b[j].reshape(kvh, grp), steps, axis=1)[..., None]
            os_ = swa_sample(qs, ck, cv, _pad_rows(kn, 8), _pad_rows(vn, 8), sink_col, steps)
            os_ = os_.reshape(bd, kvh, grp, steps, hd).transpose(3, 0, 1, 2, 4).reshape(steps * bd, d)
            ys, wo16 = mm_sample(os_.astype(BF16), w_o_b, j, d, F32, res=xs, alpha=alpha)
            yp = mm_prompt(op, wo16, d, 0, F32, res=xp, alpha=alpha)
            keep = min(WINDOW, seq)
            wk_p.append(kvp[:, :kw].reshape(batch, seq, kvh, hd)[:, seq - keep:])
            wv_p.append(kvp[:, kw:].reshape(batch, seq, kvh, hd)[:, seq - keep:])
            wb = ck.shape[1]
            wk_s.append(jnp.concatenate([ck, kn], axis=1)[:, -wb:].reshape(bd, wb, kvh, hd))
            wv_s.append(jnp.concatenate([cv, vn], axis=1)[:, -wb:].reshape(bd, wb, kvh, hd))
        else:
            hd, kvh = FOX_HEAD_DIM, FOX_KV_HEADS
            kw = kvh * hd
            heads = d // hd
            grp = heads // kvh
            wf = w_qkvf_c[j, :, d + 2 * kw:]
            qkv_s, w16 = mm_sample(xs16, w_qkvf_c, j, d + 2 * kw, F32)
            lf_s, c_s = fox_logf_sample(xs16, wf, b_f_c[j], bd)
            qp = mm_prompt(xp16, w16, d, 0, BF16)
            kvp = mm_prompt(xp16, w16, 2 * kw, d, F32)
            lf_p, c_p = fox_logf_prompt(xp16, wf, b_f_c[j], batch)
            tk = _pick(seq, 512, 256, 128)
            c4 = c_p.reshape(batch, seq, kvh, grp)
            c_k = c4.reshape(batch, seq // tk, tk, kvh, grp).transpose(0, 3, 1, 4, 2)
            op = fox_prompt(qp, kvp, c4.transpose(0, 2, 1, 3), c_k, batch)
            qs = _batch_major(qkv_s[:, :d].astype(BF16), bd).reshape(bd, steps * heads, hd)
            kn = _batch_major(qkv_s[:, d:d + kw], bd)
            vn = _batch_major(qkv_s[:, d + kw:], bd)
            c_b = _batch_major(c_s, bd)
            c_col = c_b.reshape(bd, steps * heads, 1)
            c_new_t = jnp.pad(c_b.transpose(0, 2, 1), ((0, 0), (0, 0), (0, PAGE_SIZE - steps)))
            os_ = fox_sample(page_table, qs, c_col, c_new_t, _pad_rows(kn, 8), _pad_rows(vn, 8),
                             cache_k, cache_v, cache_logf, j, steps)
            os_ = _time_major(os_.reshape(bd, steps, d))
            ys, wo16 = mm_sample(os_, w_o_c, j, d, F32, res=xs, alpha=alpha)
            yp = mm_prompt(op, wo16, d, 0, F32, res=xp, alpha=alpha)
            fk_p.append(kvp[:, :kw].reshape(batch, seq, kvh, hd))
            fv_p.append(kvp[:, kw:].reshape(batch, seq, kvh, hd))
            fl_p.append(lf_p.reshape(batch, seq, heads))
            fk_s.append(kn.reshape(bd, steps, kvh, hd))
            fv_s.append(vn.reshape(bd, steps, kvh, hd))
            fl_s.append(_batch_major(lf_s, bd))
        xs, xs16 = layer_norm_rows(ys, ln1_g[i], ln1_b[i])
        xp, xp16 = layer_norm_rows(yp, ln1_g[i], ln1_b[i])

        a_s, sg_s, su_s, wg16, wu16 = ffn_up_sample(xs16, w_up, i, conv_f[i],
                                                    _time_major(state_ffn[i]), bd)
        ys, wd16 = mm_sample(a_s, w_down, i, d, F32, res=xs, alpha=alpha)
        a_p, sg_p, su_p = ffn_up_prompt(xp16, wg16, wu16, conv_f[i], batch)
        yp = mm_prompt(a_p, wd16, d, 0, F32, res=xp, alpha=alpha)
        ffn_p.append(jnp.concatenate([sg_p, su_p], axis=-1))
        ffn_s.append(_batch_major(jnp.concatenate([sg_s, su_s], axis=-1), bd))
        xs, xs16 = layer_norm_rows(ys, ln2_g[i], ln2_b[i])
        xp, xp16 = layer_norm_rows(yp, ln2_g[i], ln2_b[i])

    return (xp.reshape(batch, seq, d), _batch_major(xs, bd),
            jnp.stack(conv_p), jnp.stack(conv_s), jnp.stack(wk_p), jnp.stack(wv_p),
            jnp.stack(wk_s), jnp.stack(wv_s), jnp.stack(fk_p), jnp.stack(fv_p), jnp.stack(fl_p),
            jnp.stack(fk_s), jnp.stack(fv_s), jnp.stack(fl_s), jnp.stack(ffn_p), jnp.stack(ffn_s))
```

```python
import functools
import math

import jax
import jax.numpy as jnp
from jax import lax
from jax.experimental import pallas as pl
from jax.experimental.pallas import tpu as pltpu

F32 = jnp.float32
BF16 = jnp.bfloat16

CONV_TAPS = 3
CONV_PREV = CONV_TAPS - 1
SW_HEAD_DIM = 64
SW_KV_HEADS = 8
WINDOW = 128
FOX_HEAD_DIM = 128
FOX_KV_HEADS = 8
PAGE_SIZE = 128
LN_EPS = 1e-5
NEG_INF = -1e30
LOG2_E = math.log2(math.e)

V7X_VMEM_BYTES = 64 * 1024 * 1024
VMEM_LIMIT = V7X_VMEM_BYTES - 8 * 1024 * 1024
LANE = 128


def _params(semantics):
    return pltpu.CompilerParams(dimension_semantics=semantics, vmem_limit_bytes=VMEM_LIMIT)


def _dot(a, b):
    return jnp.dot(a, b, preferred_element_type=F32)


def _dot_nt(a, b):
    return lax.dot_general(a, b, (((1,), (1,)), ((), ())), preferred_element_type=F32)


def _split_scale(scale):
    pre = 2.0 ** math.floor(math.log2(scale))
    return pre, scale / pre


def _pick(n, *cands):
    for c in cands:
        if n % c == 0:
            return c
    raise ValueError(f"no tile for {n} among {cands}")


def _conv_rows(u, prev, cw, shift):
    rows = u.shape[0]
    if prev is None:
        assert shift == 1
        row = lax.broadcasted_iota(jnp.int32, u.shape, 0)
        u1 = jnp.where(row >= 1, pltpu.roll(u, 1, 0), 0.0)
        u2 = jnp.where(row >= 2, pltpu.roll(u, 2, 0), 0.0)
        tail = u[rows - CONV_PREV:rows]
    else:
        full = jnp.concatenate([prev, u], axis=0)
        u2 = full[0:rows]
        u1 = full[shift:shift + rows]
        tail = full[rows:rows + CONV_PREV * shift]
    y = cw[0:1, :] * u2 + cw[1:2, :] * u1 + cw[2:3, :] * u
    return y, tail


def _gate_conv_core(x, wb, wc, wh, cw, prev, shift):
    bg = _dot(x, wb)
    c = _dot(x, wc)
    h = _dot(x, wh)
    y, tail = _conv_rows(c * h, prev, cw, shift)
    return (bg * y).astype(BF16), tail


def _gate_conv_p_kernel(x_ref, wb_ref, wc_ref, wh_ref, cw_ref, z_ref, st_ref):
    z, tail = _gate_conv_core(x_ref[...], wb_ref[...], wc_ref[...], wh_ref[...], cw_ref[...], None, 1)
    z_ref[...] = z
    st_ref[0] = tail


def _gate_conv_s_kernel(x_ref, wb_ref, wc_ref, wh_ref, cw_ref, prev_ref,
                        z_ref, st_ref, wb16_ref, wc16_ref, wh16_ref, *, shift):
    wb = wb_ref[...].astype(BF16)
    wc = wc_ref[...].astype(BF16)
    wh = wh_ref[...].astype(BF16)
    wb16_ref[...] = wb
    wc16_ref[...] = wc
    wh16_ref[...] = wh
    z, tail = _gate_conv_core(x_ref[...], wb, wc, wh, cw_ref[...], prev_ref[...], shift)
    z_ref[...] = z
    st_ref[...] = tail


def gate_conv_sample(x16, w_in, layer, cw, prev, shift):
    rows, d = x16.shape
    c = w_in.shape[2] // 3
    tn = _pick(c, 256, 128)
    nc = c // tn
    w_spec = lambda sec: pl.BlockSpec((None, d, tn), lambda n, sec=sec: (layer, 0, sec * nc + n))
    col = lambda r: pl.BlockSpec((r, tn), lambda n: (0, n))
    return pl.pallas_call(
        functools.partial(_gate_conv_s_kernel, shift=shift),
        grid=(nc,),
        in_specs=[pl.BlockSpec((rows, d), lambda n: (0, 0)), w_spec(0), w_spec(1), w_spec(2),
                  col(CONV_TAPS), col(CONV_PREV * shift)],
        out_specs=[col(rows), col(CONV_PREV * shift), col(d), col(d), col(d)],
        out_shape=[jax.ShapeDtypeStruct((rows, c), BF16),
                   jax.ShapeDtypeStruct((CONV_PREV * shift, c), F32),
                   jax.ShapeDtypeStruct((d, c), BF16),
                   jax.ShapeDtypeStruct((d, c), BF16),
                   jax.ShapeDtypeStruct((d, c), BF16)],
        compiler_params=_params(("arbitrary",)),
        name="gate_conv_sample",
    )(x16, w_in, w_in, w_in, cw, prev)


def gate_conv_prompt(x16, wb16, wc16, wh16, cw, batch):
    m, d = x16.shape
    s = m // batch
    c = wb16.shape[1]
    tn = _pick(c, 256, 128)
    w_spec = pl.BlockSpec((d, tn), lambda b, n: (0, n))
    return pl.pallas_call(
        _gate_conv_p_kernel,
        grid=(batch, c // tn),
        in_specs=[pl.BlockSpec((s, d), lambda b, n: (b, 0), pipeline_mode=pl.Buffered(1)),
                  w_spec, w_spec, w_spec,
                  pl.BlockSpec((CONV_TAPS, tn), lambda b, n: (0, n))],
        out_specs=[pl.BlockSpec((s, tn), lambda b, n: (b, n)),
                   pl.BlockSpec((1, CONV_PREV, tn), lambda b, n: (b, 0, n))],
        out_shape=[jax.ShapeDtypeStruct((m, c), BF16),
                   jax.ShapeDtypeStruct((batch, CONV_PREV, c), F32)],
        compiler_params=_params(("arbitrary", "arbitrary")),
        name="gate_conv_prompt",
    )(x16, wb16, wc16, wh16, cw)


def _ffn_up_core(x, wg, wu, cwg, cwu, prev_g, prev_u, shift):
    cg, tail_g = _conv_rows(_dot(x, wg), prev_g, cwg, shift)
    cu, tail_u = _conv_rows(_dot(x, wu), prev_u, cwu, shift)
    act = (cg * (1.0 / (1.0 + jnp.exp(-cg)))) * cu
    return act.astype(BF16), tail_g, tail_u


def _ffn_up_p_kernel(x_ref, wg_ref, wu_ref, cwg_ref, cwu_ref, a_ref, sg_ref, su_ref):
    act, tg, tu = _ffn_up_core(x_ref[...], wg_ref[...], wu_ref[...], cwg_ref[...], cwu_ref[...],
                               None, None, 1)
    a_ref[...] = act
    sg_ref[0] = tg
    su_ref[0] = tu


def _ffn_up_s_kernel(x_ref, wg_ref, wu_ref, cwg_ref, cwu_ref, pg_ref, pu_ref,
                     a_ref, sg_ref, su_ref, wg16_ref, wu16_ref, *, shift):
    wg = wg_ref[...].astype(BF16)
    wu = wu_ref[...].astype(BF16)
    wg16_ref[...] = wg
    wu16_ref[...] = wu
    act, tg, tu = _ffn_up_core(x_ref[...], wg, wu, cwg_ref[...], cwu_ref[...],
                               pg_ref[...], pu_ref[...], shift)
    a_ref[...] = act
    sg_ref[...] = tg
    su_ref[...] = tu


def ffn_up_sample(x16, w_up, layer, cw, prev, shift):
    rows, d = x16.shape
    f = w_up.shape[2] // 2
    tn = _pick(f, 256, 128)
    nf = f // tn
    sec = lambda r, k: pl.BlockSpec((r, tn), lambda n, k=k: (0, k * nf + n))
    wsec = lambda k: pl.BlockSpec((None, d, tn), lambda n, k=k: (layer, 0, k * nf + n))
    col = lambda r: pl.BlockSpec((r, tn), lambda n: (0, n))
    pr = CONV_PREV * shift
    return pl.pallas_call(
        functools.partial(_ffn_up_s_kernel, shift=shift),
        grid=(nf,),
        in_specs=[pl.BlockSpec((rows, d), lambda n: (0, 0)), wsec(0), wsec(1),
                  sec(CONV_TAPS, 0), sec(CONV_TAPS, 1), sec(pr, 0), sec(pr, 1)],
        out_specs=[col(rows), col(pr), col(pr), col(d), col(d)],
        out_shape=[jax.ShapeDtypeStruct((rows, f), BF16),
                   jax.ShapeDtypeStruct((pr, f), F32),
                   jax.ShapeDtypeStruct((pr, f), F32),
                   jax.ShapeDtypeStruct((d, f), BF16),
                   jax.ShapeDtypeStruct((d, f), BF16)],
        compiler_params=_params(("arbitrary",)),
        name="ffn_up_sample",
    )(x16, w_up, w_up, cw, cw, prev, prev)


def ffn_up_prompt(x16, wg16, wu16, cw, batch):
    m, d = x16.shape
    s = m // batch
    f = wg16.shape[1]
    tn = _pick(f, 256, 128)
    nf = f // tn
    w_spec = pl.BlockSpec((d, tn), lambda b, n: (0, n))
    st_spec = pl.BlockSpec((1, CONV_PREV, tn), lambda b, n: (b, 0, n))
    return pl.pallas_call(
        _ffn_up_p_kernel,
        grid=(batch, nf),
        in_specs=[pl.BlockSpec((s, d), lambda b, n: (b, 0), pipeline_mode=pl.Buffered(1)),
                  w_spec, w_spec,
                  pl.BlockSpec((CONV_TAPS, tn), lambda b, n: (0, n)),
                  pl.BlockSpec((CONV_TAPS, tn), lambda b, n: (0, nf + n))],
        out_specs=[pl.BlockSpec((s, tn), lambda b, n: (b, n)), st_spec, st_spec],
        out_shape=[jax.ShapeDtypeStruct((m, f), BF16),
                   jax.ShapeDtypeStruct((batch, CONV_PREV, f), F32),
                   jax.ShapeDtypeStruct((batch, CONV_PREV, f), F32)],
        compiler_params=_params(("arbitrary", "arbitrary")),
        name="ffn_up_prompt",
    )(x16, wg16, wu16, cw, cw)


def _mm_p_kernel(*refs, alpha, has_res):
    if has_res:
        x_ref, w_ref, r_ref, o_ref = refs
    else:
        x_ref, w_ref, o_ref = refs
    acc = _dot(x_ref[...], w_ref[...])
    if has_res:
        acc = alpha * r_ref[...] + acc
    o_ref[...] = acc.astype(o_ref.dtype)


def _mm_s_kernel(*refs, alpha, has_res):
    if has_res:
        x_ref, w_ref, r_ref, o_ref, w16_ref = refs
    else:
        x_ref, w_ref, o_ref, w16_ref = refs
    w = w_ref[...].astype(BF16)
    w16_ref[...] = w
    acc = _dot(x_ref[...], w)
    if has_res:
        acc = alpha * r_ref[...] + acc
    o_ref[...] = acc.astype(o_ref.dtype)


def mm_sample(x16, w, layer, n_out, out_dtype, res=None, alpha=None):
    rows, k = x16.shape
    tn = _pick(n_out, 512, 256, 128) if k <= 4096 else _pick(n_out, 256, 128)
    col = lambda r: pl.BlockSpec((r, tn), lambda n: (0, n))
    in_specs = [pl.BlockSpec((rows, k), lambda n: (0, 0)),
                pl.BlockSpec((None, k, tn), lambda n: (layer, 0, n))]
    args = [x16, w]
    if res is not None:
        in_specs.append(col(rows))
        args.append(res)
    return pl.pallas_call(
        functools.partial(_mm_s_kernel, alpha=alpha, has_res=res is not None),
        grid=(n_out // tn,),
        in_specs=in_specs,
        out_specs=[col(rows), col(k)],
        out_shape=[jax.ShapeDtypeStruct((rows, n_out), out_dtype),
                   jax.ShapeDtypeStruct((k, n_out), BF16)],
        compiler_params=_params(("arbitrary",)),
        name="mm_sample",
    )(*args)


def mm_prompt(x16, w16, n_out, col_off, out_dtype, res=None, alpha=None):
    m, k = x16.shape
    if k <= 4096:
        tm, tn = _pick(m, 1024, 512, 256), _pick(n_out, 512, 256, 128)
    else:
        tm, tn = _pick(m, 512, 256), _pick(n_out, 512, 256, 128)
    assert col_off % tn == 0
    off = col_off // tn
    in_specs = [pl.BlockSpec((tm, k), lambda i, n: (i, 0)),
                pl.BlockSpec((k, tn), lambda i, n: (0, off + n))]
    args = [x16, w16]
    if res is not None:
        in_specs.append(pl.BlockSpec((tm, tn), lambda i, n: (i, n)))
        args.append(res)
    return pl.pallas_call(
        functools.partial(_mm_p_kernel, alpha=alpha, has_res=res is not None),
        grid=(m // tm, n_out // tn),
        in_specs=in_specs,
        out_specs=pl.BlockSpec((tm, tn), lambda i, n: (i, n)),
        out_shape=jax.ShapeDtypeStruct((m, n_out), out_dtype),
        compiler_params=_params(("arbitrary", "arbitrary")),
        name="mm_prompt",
    )(*args)


def _ln_kernel(y_ref, g_ref, b_ref, xf_ref, xb_ref):
    y = y_ref[...]
    mu = jnp.mean(y, axis=-1, keepdims=True)
    dev = y - mu
    var = jnp.mean(dev * dev, axis=-1, keepdims=True)
    out = dev * lax.rsqrt(var + LN_EPS) * g_ref[...] + b_ref[...]
    xf_ref[...] = out
    xb_ref[...] = out.astype(BF16)


def layer_norm_rows(y, g, b):
    m, d = y.shape
    tr = _pick(m, 256, 128)
    row = pl.BlockSpec((tr, d), lambda i: (i, 0))
    vec = pl.BlockSpec((1, d), lambda i: (0, 0))
    return pl.pallas_call(
        _ln_kernel,
        grid=(m // tr,),
        in_specs=[row, vec, vec],
        out_specs=[row, row],
        out_shape=[jax.ShapeDtypeStruct((m, d), F32), jax.ShapeDtypeStruct((m, d), BF16)],
        compiler_params=_params(("arbitrary",)),
        name="layer_norm",
    )(y, g.reshape(1, d), b.reshape(1, d))


def _sink_softmax_pv(s, sink, v16):
    m = jnp.maximum(jnp.max(s, axis=-1, keepdims=True), sink)
    p = jnp.exp(s - m)
    den = jnp.sum(p, axis=-1, keepdims=True) + jnp.exp(sink - m)
    return _dot((p / den).astype(BF16), v16)


def _swa_p_kernel(q_ref, kp_ref, ko_ref, vp_ref, vo_ref, sink_ref, o_ref, *, grp, scale):
    blk = q_ref.shape[0]
    first = pl.program_id(1) == 0
    kcat = jnp.concatenate([kp_ref[...], ko_ref[...]], axis=0).astype(BF16)
    vcat = jnp.concatenate([vp_ref[...], vo_ref[...]], axis=0).astype(BF16)
    tl = lax.broadcasted_iota(jnp.int32, (blk, 2 * blk), 0)
    j = lax.broadcasted_iota(jnp.int32, (blk, 2 * blk), 1)
    mask = (j >= tl) & (j <= tl + WINDOW) & ((j >= blk) | jnp.logical_not(first))
    hd = SW_HEAD_DIM
    unit = (lax.broadcasted_iota(jnp.int32, (2 * blk, hd), 1) == 0).astype(BF16)
    for h in range(SW_KV_HEADS):
        kh = kcat[:, h * hd:(h + 1) * hd]
        vh = jnp.concatenate([vcat[:, h * hd:(h + 1) * hd], unit], axis=1)
        for g in range(grp):
            head = h * grp + g
            sink = sink_ref[head]
            s = _dot_nt(q_ref[:, head * hd:(head + 1) * hd], kh) * scale
            s = jnp.where(mask, s, NEG_INF)
            m = jnp.maximum(jnp.max(s, axis=-1, keepdims=True), sink)
            ov = _dot(jnp.exp(s - m).astype(BF16), vh)
            den = ov[:, hd:hd + 1] + jnp.exp(sink - m)
            o_ref[:, head * hd:(head + 1) * hd] = (ov[:, :hd] / den).astype(BF16)


def swa_prompt(q16, k, v, sinks, batch):
    m, qw = q16.shape
    s = m // batch
    blk = WINDOW
    nb = s // blk
    kw = SW_KV_HEADS * SW_HEAD_DIM
    grp = qw // kw
    own = pl.BlockSpec((blk, kw), lambda b, i: (b * nb + i, 0))
    prev = pl.BlockSpec((blk, kw), lambda b, i: (b * nb + jnp.maximum(i - 1, 0), 0))
    return pl.pallas_call(
        functools.partial(_swa_p_kernel, grp=grp, scale=SW_HEAD_DIM ** -0.5),
        grid=(batch, nb),
        in_specs=[pl.BlockSpec((blk, qw), lambda b, i: (b * nb + i, 0)),
                  prev, own, prev, own,
                  pl.BlockSpec(memory_space=pltpu.SMEM)],
        out_specs=pl.BlockSpec((blk, qw), lambda b, i: (b * nb + i, 0)),
        out_shape=jax.ShapeDtypeStruct((m, qw), BF16),
        compiler_params=_params(("arbitrary", "arbitrary")),
        name="swa_prompt",
    )(q16, k, k, v, v, sinks)


def _swa_s_kernel(q_ref, ck_ref, cv_ref, kn_ref, vn_ref, sink_ref, o_ref, *, steps, scale):
    wb = ck_ref.shape[1]
    pad = jnp.zeros((wb - kn_ref.shape[1], ck_ref.shape[2]), F32)
    kcat = jnp.concatenate([ck_ref[0], kn_ref[0], pad], axis=0).astype(BF16)
    vcat = jnp.concatenate([cv_ref[0], vn_ref[0], pad], axis=0).astype(BF16)
    rows = q_ref.shape[2]
    t = lax.broadcasted_iota(jnp.int32, (rows, 2 * wb), 0) % steps
    j = lax.broadcasted_iota(jnp.int32, (rows, 2 * wb), 1)
    mask = (j >= t + wb - WINDOW) & (j <= t + wb)
    hd = SW_HEAD_DIM
    for h in range(SW_KV_HEADS):
        s = _dot_nt(q_ref[0, h], kcat[:, h * hd:(h + 1) * hd]) * scale
        s = jnp.where(mask, s, NEG_INF)
        o_ref[0, h] = _sink_softmax_pv(s, sink_ref[h], vcat[:, h * hd:(h + 1) * hd])


def swa_sample(q16, cache_k, cache_v, k_new, v_new, sink_col, steps):
    bd, kvh, rows, hd = q16.shape
    wb, kw = cache_k.shape[1], cache_k.shape[2]
    per_b = lambda a: pl.BlockSpec((1,) + a.shape[1:], lambda b: (b,) + (0,) * (a.ndim - 1))
    return pl.pallas_call(
        functools.partial(_swa_s_kernel, steps=steps, scale=SW_HEAD_DIM ** -0.5),
        grid=(bd,),
        in_specs=[per_b(q16), per_b(cache_k), per_b(cache_v), per_b(k_new), per_b(v_new),
                  pl.BlockSpec(sink_col.shape, lambda b: (0, 0, 0))],
        out_specs=pl.BlockSpec((1, kvh, rows, hd), lambda b: (b, 0, 0, 0)),
        out_shape=jax.ShapeDtypeStruct((bd, kvh, rows, hd), F32),
        compiler_params=_params(("arbitrary",)),
        name="swa_sample",
    )(q16, cache_k, cache_v, k_new, v_new, sink_col)


def _log_sigmoid(z):
    return jnp.minimum(z, 0.0) - jnp.log1p(jnp.exp(-jnp.abs(z)))


def _fox_logf_p_kernel(x_ref, wf_ref, bf_ref, lf_ref, c_ref, *, blk):
    lf = _log_sigmoid(_dot(x_ref[...], wf_ref[...].astype(BF16)) + bf_ref[...])
    lf_ref[...] = lf
    r = lax.broadcasted_iota(jnp.int32, (blk, blk), 0)
    c = lax.broadcasted_iota(jnp.int32, (blk, blk), 1)
    tri = (c <= r).astype(F32)
    carry = jnp.zeros((1, lf.shape[1]), F32)
    for i in range(lf.shape[0] // blk):
        cb = jnp.dot(tri, lf[i * blk:(i + 1) * blk], precision=lax.Precision.HIGHEST,
                     preferred_element_type=F32) + carry
        c_ref[i * blk:(i + 1) * blk, :] = cb
        carry = cb[blk - 1:blk, :]


def fox_logf_prompt(x16, wf, bf, batch):
    m, d = x16.shape
    s = m // batch
    h = wf.shape[1]
    row = pl.BlockSpec((s, h), lambda b: (b, 0))
    return pl.pallas_call(
        functools.partial(_fox_logf_p_kernel, blk=_pick(s, 256, 128)),
        grid=(batch,),
        in_specs=[pl.BlockSpec((s, d), lambda b: (b, 0)),
                  pl.BlockSpec((d, h), lambda b: (0, 0)),
                  pl.BlockSpec((1, h), lambda b: (0, 0))],
        out_specs=[row, row],
        out_shape=[jax.ShapeDtypeStruct((m, h), F32), jax.ShapeDtypeStruct((m, h), F32)],
        compiler_params=_params(("arbitrary",)),
        name="fox_logf_prompt",
    )(x16, wf, bf.reshape(1, h))


def _fox_logf_s_kernel(x_ref, wf_ref, bf_ref, lf_ref, c_ref, *, shift):
    lf = _log_sigmoid(_dot(x_ref[...], wf_ref[...].astype(BF16)) + bf_ref[...])
    lf_ref[...] = lf
    run = lf[0:shift]
    c_ref[0:shift, :] = run
    for t in range(1, lf.shape[0] // shift):
        run = run + lf[t * shift:(t + 1) * shift]
        c_ref[t * shift:(t + 1) * shift, :] = run


def fox_logf_sample(x16, wf, bf, shift):
    rows, d = x16.shape
    h = wf.shape[1]
    full = lambda a, b: pl.BlockSpec((a, b), lambda i: (0, 0))
    return pl.pallas_call(
        functools.partial(_fox_logf_s_kernel, shift=shift),
        grid=(1,),
        in_specs=[full(rows, d), full(d, h), full(1, h)],
        out_specs=[full(rows, h), full(rows, h)],
        out_shape=[jax.ShapeDtypeStruct((rows, h), F32), jax.ShapeDtypeStruct((rows, h), F32)],
        compiler_params=_params(("arbitrary",)),
        name="fox_logf_sample",
    )(x16, wf, bf.reshape(1, h))


def _fox_p_kernel(q_ref, k_ref, v_ref, cq_ref, ck_ref, o_ref, k16_ref, v16_ref, m_ref, l_ref, acc_ref,
                  *, grp, pre, rest):
    qi = pl.program_id(2)
    tq = q_ref.shape[0]
    tk = tq
    hd = FOX_HEAD_DIM
    rows = grp * tq
    c_exp = rest * LOG2_E

    @pl.when(qi == 0)
    def _():
        k16_ref[...] = k_ref[...].astype(BF16)
        v16_ref[...] = v_ref[...].astype(BF16)

    q = jnp.concatenate([q_ref[:, g * hd:(g + 1) * hd] for g in range(grp)], axis=0) * pre
    cq = jnp.concatenate([cq_ref[0, 0, :, g:g + 1] for g in range(grp)], axis=0) * (1.0 / rest)
    m_ref[...] = jnp.full(m_ref.shape, NEG_INF, F32)
    l_ref[...] = jnp.zeros(l_ref.shape, F32)
    acc_ref[...] = jnp.zeros(acc_ref.shape, F32)

    def update(j, diagonal):
        off = pl.multiple_of(j * tk, tk)
        t = _dot_nt(q, k16_ref[pl.ds(off, tk), :]) + cq
        ck = ck_ref[0, 0, j] * (1.0 / rest)
        t = (t.reshape(grp, tq, tk) - ck[:, None, :]).reshape(rows, tk)
        if diagonal:
            r = lax.broadcasted_iota(jnp.int32, (rows, tk), 0) % tq
            c = lax.broadcasted_iota(jnp.int32, (rows, tk), 1)
            t = jnp.where(c <= r, t, NEG_INF)
        m_old = m_ref[...]
        m_new = jnp.maximum(m_old, jnp.max(t, axis=-1, keepdims=True))
        a = jnp.exp2((m_old - m_new) * c_exp)
        p = jnp.exp2((t - m_new) * c_exp)
        l_ref[...] = a * l_ref[...] + jnp.sum(p, axis=-1, keepdims=True)
        acc_ref[...] = a * acc_ref[...] + _dot(p.astype(BF16), v16_ref[pl.ds(off, tk), :])
        m_ref[...] = m_new

    def below_diagonal(j, carry):
        update(j, False)
        return carry

    lax.fori_loop(0, qi, below_diagonal, 0)
    update(qi, True)
    out = acc_ref[...] / l_ref[...]
    for g in range(grp):
        o_ref[:, g * hd:(g + 1) * hd] = out[g * tq:(g + 1) * tq].astype(BF16)


def fox_prompt(q16, k, v, c_q, c_k, batch):
    m, qw = q16.shape
    s = m // batch
    hd, kvh = FOX_HEAD_DIM, FOX_KV_HEADS
    grp = qw // (kvh * hd)
    nk, tq = c_k.shape[2], c_k.shape[4]
    nq = s // tq
    pre, rest = _split_scale(hd ** -0.5)
    kv_spec = pl.BlockSpec((s, hd), lambda b, h, qi: (b, h))
    return pl.pallas_call(
        functools.partial(_fox_p_kernel, grp=grp, pre=pre, rest=rest),
        grid=(batch, kvh, nq),
        in_specs=[pl.BlockSpec((tq, grp * hd), lambda b, h, qi: (b * nq + qi, h)),
                  kv_spec, kv_spec,
                  pl.BlockSpec((1, 1, tq, grp), lambda b, h, qi: (b, h, qi, 0)),
                  pl.BlockSpec((1, 1, nk, grp, tq), lambda b, h, qi: (b, h, 0, 0, 0))],
        out_specs=pl.BlockSpec((tq, grp * hd), lambda b, h, qi: (b * nq + qi, h)),
        out_shape=jax.ShapeDtypeStruct((m, qw), BF16),
        scratch_shapes=[pltpu.VMEM((s, hd), BF16), pltpu.VMEM((s, hd), BF16),
                        pltpu.VMEM((grp * tq, 1), F32), pltpu.VMEM((grp * tq, 1), F32),
                        pltpu.VMEM((grp * tq, hd), F32)],
        compiler_params=_params(("arbitrary", "arbitrary", "arbitrary")),
        name="fox_prompt",
    )(q16, k, v, c_q, c_k)


def _fox_s_kernel(pt_ref, q_ref, ccol_ref, cnt_ref, kn_ref, vn_ref, *rest, pages, steps, grp, scale):
    k_refs = rest[0:pages]
    v_refs = rest[pages:2 * pages]
    lf_refs = rest[2 * pages:3 * pages]
    o_ref, qbd_ref, m_ref, l_ref, acc_ref, carry_ref = rest[3 * pages:]
    p = pl.program_id(1)
    rows, hd = q_ref.shape[1], q_ref.shape[2]
    heads = rows // steps
    kvh = heads // grp
    psz = cnt_ref.shape[2]
    row = lax.broadcasted_iota(jnp.int32, (rows, hd), 0)
    row_kvh = (row % heads) // grp

    def page_update(k16, v16, neg_ck, mask):
        s = _dot_nt(qbd_ref[...], k16) * scale
        s = s + (ccol_ref[0] + jnp.concatenate([neg_ck] * steps, axis=0))
        if mask is not None:
            s = jnp.where(mask, s, NEG_INF)
        m_old = m_ref[...]
        m_new = jnp.maximum(m_old, jnp.max(s, axis=-1, keepdims=True))
        a = jnp.exp(m_old - m_new)
        pr = jnp.exp(s - m_new)
        l_ref[...] = a * l_ref[...] + jnp.sum(pr, axis=-1, keepdims=True)
        acc_ref[...] = a * acc_ref[...] + _dot(pr.astype(BF16), v16)
        m_ref[...] = m_new

    @pl.when(p == 0)
    def _():
        q = q_ref[0]
        for h in range(kvh):
            qbd_ref[:, h * hd:(h + 1) * hd] = jnp.where(row_kvh == h, q, jnp.zeros_like(q))
        m_ref[...] = jnp.full(m_ref.shape, NEG_INF, F32)
        l_ref[...] = jnp.zeros(l_ref.shape, F32)
        acc_ref[...] = jnp.zeros(acc_ref.shape, F32)
        carry_ref[...] = jnp.zeros(carry_ref.shape, F32)
        pad = jnp.zeros((psz - kn_ref.shape[1], kn_ref.shape[2]), F32)
        k16 = jnp.concatenate([kn_ref[0], pad], axis=0).astype(BF16)
        v16 = jnp.concatenate([vn_ref[0], pad], axis=0).astype(BF16)
        t = lax.broadcasted_iota(jnp.int32, (rows, psz), 0) // heads
        j = lax.broadcasted_iota(jnp.int32, (rows, psz), 1)
        page_update(k16, v16, -cnt_ref[0], j <= t)

    def flat16(ref):
        rows2d = ref.reshape(psz * kvh, hd)
        return jnp.concatenate([rows2d[pl.ds(h, psz, stride=kvh), :] for h in range(kvh)],
                               axis=1).astype(BF16)

    @pl.when(p > 0)
    def _():
        r = lax.broadcasted_iota(jnp.int32, (psz, 2 * psz), 0)
        c = lax.broadcasted_iota(jnp.int32, (psz, 2 * psz), 1)
        later_and_total = ((r > c) | (c >= psz)).astype(F32)
        carry = carry_ref[...]
        parts = []
        for i in range(pages):
            w = lax.dot_general(lf_refs[i][...], later_and_total, (((0,), (0,)), ((), ())),
                                precision=lax.Precision.HIGHEST, preferred_element_type=F32)
            parts.append(w[:, :psz] + carry)
            carry = carry + w[:, psz:psz + 1]
        carry_ref[...] = carry
        k16 = jnp.concatenate([flat16(k_refs[i]) for i in range(pages)], axis=0)
        v16 = jnp.concatenate([flat16(v_refs[i]) for i in range(pages)], axis=0)
        page_update(k16, v16, jnp.concatenate(parts, axis=1), None)

    @pl.when(p == pl.num_programs(1) - 1)
    def _():
        out = jnp.zeros((rows, hd), F32)
        for h in range(kvh):
            out = out + jnp.where(row_kvh == h, acc_ref[:, h * hd:(h + 1) * hd], 0.0)
        o_ref[0] = (out / l_ref[...]).astype(BF16)


def fox_sample(page_table, q16, c_col, c_new_t, k_new, v_new, pool_k, pool_v, pool_lf, layer, steps):
    bd, rows, hd = q16.shape
    heads = rows // steps
    psz, kvh = pool_k.shape[2], pool_k.shape[3]
    kw = kvh * hd
    n_pages = page_table.shape[1]
    pages = _pick(n_pages, 8, 4, 2, 1)
    n_steps = n_pages // pages + 1

    def page_idx(i, tail):
        return lambda b, p, pt: (layer, pt[b, n_pages - 1 - (jnp.maximum(p - 1, 0) * pages + i)]) + tail

    per_b = lambda a: pl.BlockSpec((1,) + a.shape[1:], lambda b, p, pt: (b,) + (0,) * (a.ndim - 1))
    kv_specs = lambda: [pl.BlockSpec((None, None, psz, kvh, hd), page_idx(i, (0, 0, 0))) for i in range(pages)]
    lf_specs = [pl.BlockSpec((None, None, psz, heads), page_idx(i, (0, 0))) for i in range(pages)]
    grid_spec = pltpu.PrefetchScalarGridSpec(
        num_scalar_prefetch=1,
        grid=(bd, n_steps),
        in_specs=[per_b(q16), per_b(c_col), per_b(c_new_t), per_b(k_new), per_b(v_new)]
        + kv_specs() + kv_specs() + lf_specs,
        out_specs=pl.BlockSpec((1, rows, hd), lambda b, p, pt: (b, 0, 0)),
        scratch_shapes=[pltpu.VMEM((rows, kw), BF16), pltpu.VMEM((rows, 1), F32),
                        pltpu.VMEM((rows, 1), F32), pltpu.VMEM((rows, kw), F32),
                        pltpu.VMEM((heads, 1), F32)],
    )
    return pl.pallas_call(
        functools.partial(_fox_s_kernel, pages=pages, steps=steps, grp=heads // FOX_KV_HEADS,
                          scale=FOX_HEAD_DIM ** -0.5),
        grid_spec=grid_spec,
        out_shape=jax.ShapeDtypeStruct((bd, rows, hd), BF16),
        compiler_params=_params(("arbitrary", "arbitrary")),
        name="fox_sample",
    )(page_table, q16, c_col, c_new_t, k_new, v_new,
      *([pool_k] * pages), *([pool_v] * pages), *([pool_lf] * pages))


def _time_major(a):
    return jnp.swapaxes(a, 0, 1).reshape((a.shape[0] * a.shape[1],) + a.shape[2:])


def _batch_major(a, bd):
    return jnp.swapaxes(a.reshape((a.shape[0] // bd, bd) + a.shape[1:]), 0, 1)


def _pad_rows(a, rows):
    return jnp.pad(a, ((0, 0), (0, rows - a.shape[1]), (0, 0)))


def kernel(x_prompt, x_sample, state_conv, cache_win_k, cache_win_v, cache_k, cache_v, cache_logf,
           state_ffn, page_table, w_in_a, conv_a, w_out_a, w_qkv_b, sinks_b, w_o_b, w_qkvf_c, b_f_c,
           w_o_c, ln1_g, ln1_b, w_up, conv_f, w_down, ln2_g, ln2_b):
    batch, seq, d = x_prompt.shape
    bd, steps, _ = x_sample.shape
    depth = ln1_g.shape[0]
    alpha = (2 * depth) ** 0.25
    n_mixers = 3
    f = w_down.shape[1]

    xp = x_prompt.reshape(batch * seq, d)
    xp16 = xp.astype(BF16)
    xs = _time_major(x_sample)
    xs16 = xs.astype(BF16)

    conv_p, conv_s, wk_p, wv_p, wk_s, wv_s = [], [], [], [], [], []
    fk_p, fv_p, fl_p, fk_s, fv_s, fl_s = [], [], [], [], [], []
    ffn_p, ffn_s = [], []

    for i in range(depth):
        mix, j = i % n_mixers, i // n_mixers
        if mix == 0:
            zs, st_s, wb16, wc16, wh16 = gate_conv_sample(xs16, w_in_a, j, conv_a[j],
                                                          _time_major(state_conv[j]), bd)
            ys, wo16 = mm_sample(zs, w_out_a, j, d, F32, res=xs, alpha=alpha)
            zp, st_p = gate_conv_prompt(xp16, wb16, wc16, wh16, conv_a[j], batch)
            yp = mm_prompt(zp, wo16, d, 0, F32, res=xp, alpha=alpha)
            conv_p.append(st_p)
            conv_s.append(_batch_major(st_s, bd))
        elif mix == 1:
            hd, kvh = SW_HEAD_DIM, SW_KV_HEADS
            kw = kvh * hd
            grp = d // kw
            qkv_s, w16 = mm_sample(xs16, w_qkv_b, j, d + 2 * kw, F32)
            qp = mm_prompt(xp16, w16, d, 0, BF16)
            kp = mm_prompt(xp16, w16, kw, d, F32)
            vp = mm_prompt(xp16, w16, kw, d + kw, F32)
            op = swa_prompt(qp, kp, vp, sinks_b[j], batch)
            qs = qkv_s[:, :d].astype(BF16).reshape(steps, bd, kvh, grp, hd)
            qs = qs.transpose(1, 2, 3, 0, 4).reshape(bd, kvh, grp * steps, hd)
            kn = _batch_major(qkv_s[:, d:d + kw], bd)
            vn = _batch_major(qkv_s[:, d + kw:], bd)
            ck = cache_win_k[j].reshape(bd, -1, kw)
            cv = cache_win_v[j].reshape(bd, -1, kw)
            sink_col = jnp.repeat(sinks_b[j].reshape(kvh, grp), steps, axis=1)[..., None]
            os_ = swa_sample(qs, ck, cv, _pad_rows(kn, 8), _pad_rows(vn, 8), sink_col, steps)
            os_ = os_.reshape(bd, kvh, grp, steps, hd).transpose(3, 0, 1, 2, 4).reshape(steps * bd, d)
            ys, wo16 = mm_sample(os_.astype(BF16), w_o_b, j, d, F32, res=xs, alpha=alpha)
            yp = mm_prompt(op, wo16, d, 0, F32, res=xp, alpha=alpha)
            keep = min(WINDOW, seq)
            wk_p.append(kp.reshape(batch, seq, kvh, hd)[:, seq - keep:])
            wv_p.append(vp.reshape(batch, seq, kvh, hd)[:, seq - keep:])
            wb = ck.shape[1]
            wk_s.append(jnp.concatenate([ck, kn], axis=1)[:, -wb:].reshape(bd, wb, kvh, hd))
            wv_s.append(jnp.concatenate([cv, vn], axis=1)[:, -wb:].reshape(bd, wb, kvh, hd))
        else:
            hd, kvh = FOX_HEAD_DIM, FOX_KV_HEADS
            kw = kvh * hd
            heads = d // hd
            grp = heads // kvh
            wf = w_qkvf_c[j, :, d + 2 * kw:]
            qkv_s, w16 = mm_sample(xs16, w_qkvf_c, j, d + 2 * kw, F32)
            lf_s, c_s = fox_logf_sample(xs16, wf, b_f_c[j], bd)
            qp = mm_prompt(xp16, w16, d, 0, BF16)
            kp = mm_prompt(xp16, w16, kw, d, F32)
            vp = mm_prompt(xp16, w16, kw, d + kw, F32)
            lf_p, c_p = fox_logf_prompt(xp16, wf, b_f_c[j], batch)
            tk = _pick(seq, 512, 256, 128)
            c4 = c_p.reshape(batch, seq, kvh, grp)
            c_k = c4.reshape(batch, seq // tk, tk, kvh, grp).transpose(0, 3, 1, 4, 2)
            op = fox_prompt(qp, kp, vp, c4.transpose(0, 2, 1, 3), c_k, batch)
            qs = _batch_major(qkv_s[:, :d].astype(BF16), bd).reshape(bd, steps * heads, hd)
            kn = _batch_major(qkv_s[:, d:d + kw], bd)
            vn = _batch_major(qkv_s[:, d + kw:], bd)
            c_b = _batch_major(c_s, bd)
            c_col = c_b.reshape(bd, steps * heads, 1)
            c_new_t = jnp.pad(c_b.transpose(0, 2, 1), ((0, 0), (0, 0), (0, PAGE_SIZE - steps)))
            os_ = fox_sample(page_table, qs, c_col, c_new_t, _pad_rows(kn, 8), _pad_rows(vn, 8),
                             cache_k, cache_v, cache_logf, j, steps)
            os_ = _time_major(os_.reshape(bd, steps, d))
            ys, wo16 = mm_sample(os_, w_o_c, j, d, F32, res=xs, alpha=alpha)
            yp = mm_prompt(op, wo16, d, 0, F32, res=xp, alpha=alpha)
            fk_p.append(kp.reshape(batch, seq, kvh, hd))
            fv_p.append(vp.reshape(batch, seq, kvh, hd))
            fl_p.append(lf_p.reshape(batch, seq, heads))
            fk_s.append(kn.reshape(bd, steps, kvh, hd))
            fv_s.append(vn.reshape(bd, steps, kvh, hd))
            fl_s.append(_batch_major(lf_s, bd))
        xs, xs16 = layer_norm_rows(ys, ln1_g[i], ln1_b[i])
        xp, xp16 = layer_norm_rows(yp, ln1_g[i], ln1_b[i])

        a_s, sg_s, su_s, wg16, wu16 = ffn_up_sample(xs16, w_up, i, conv_f[i],
                                                    _time_major(state_ffn[i]), bd)
        ys, wd16 = mm_sample(a_s, w_down, i, d, F32, res=xs, alpha=alpha)
        a_p, sg_p, su_p = ffn_up_prompt(xp16, wg16, wu16, conv_f[i], batch)
        yp = mm_prompt(a_p, wd16, d, 0, F32, res=xp, alpha=alpha)
        ffn_p.append(jnp.concatenate([sg_p, su_p], axis=-1))
        ffn_s.append(_batch_major(jnp.concatenate([sg_s, su_s], axis=-1), bd))
        xs, xs16 = layer_norm_rows(ys, ln2_g[i], ln2_b[i])
        xp, xp16 = layer_norm_rows(yp, ln2_g[i], ln2_b[i])

    return (xp.reshape(batch, seq, d), _batch_major(xs, bd),
            jnp.stack(conv_p), jnp.stack(conv_s), jnp.stack(wk_p), jnp.stack(wv_p),
            jnp.stack(wk_s), jnp.stack(wv_s), jnp.stack(fk_p), jnp.stack(fv_p), jnp.stack(fl_p),
            jnp.stack(fk_s), jnp.stack(fv_s), jnp.stack(fl_s), jnp.stack(ffn_p), jnp.stack(ffn_s))
```

```python
import functools
import math

import jax
import jax.numpy as jnp
from jax import lax
from jax.experimental import pallas as pl
from jax.experimental.pallas import tpu as pltpu

F32 = jnp.float32
BF16 = jnp.bfloat16

CONV_TAPS = 3
CONV_PREV = CONV_TAPS - 1
CONV_HALO = 8
SW_HEAD_DIM = 64
SW_KV_HEADS = 8
WINDOW = 128
FOX_HEAD_DIM = 128
FOX_KV_HEADS = 8
PAGE_SIZE = 128
LN_EPS = 1e-5
NEG_INF = -1e30
LOG2_E = math.log2(math.e)

V7X_VMEM_BYTES = 64 * 1024 * 1024
VMEM_LIMIT = V7X_VMEM_BYTES - 8 * 1024 * 1024
LANE = 128


def _params(semantics):
    return pltpu.CompilerParams(dimension_semantics=semantics, vmem_limit_bytes=VMEM_LIMIT)


def _dot(a, b):
    return jnp.dot(a, b, preferred_element_type=F32)


def _dot_nt(a, b):
    return lax.dot_general(a, b, (((1,), (1,)), ((), ())), preferred_element_type=F32)


def _split_scale(scale):
    pre = 2.0 ** math.floor(math.log2(scale))
    return pre, scale / pre


def _pick(n, *cands):
    for c in cands:
        if n % c == 0:
            return c
    raise ValueError(f"no tile for {n} among {cands}")


def _conv_rows(u, prev, cw, shift):
    rows = u.shape[0]
    full = jnp.concatenate([prev, u], axis=0)
    y = cw[0:1, :] * full[0:rows] + cw[1:2, :] * full[shift:shift + rows] + cw[2:3, :] * u
    return y, full[rows:rows + CONV_PREV * shift]


def _conv_block(u, halo, cw):
    full = jnp.concatenate([halo, u], axis=0)
    y = cw[0:1, :] * pltpu.roll(full, 2, 0) + cw[1:2, :] * pltpu.roll(full, 1, 0) + cw[2:3, :] * full
    return y[CONV_HALO:]


def _gate_conv_core(x, wb, wc, wh, cw, prev, shift):
    bg = _dot(x, wb)
    c = _dot(x, wc)
    h = _dot(x, wh)
    y, tail = _conv_rows(c * h, prev, cw, shift)
    return (bg * y).astype(BF16), tail


def _gate_conv_p_kernel(x_ref, wb_ref, wc_ref, wh_ref, cw_ref, z_ref, st_ref, *, rb):
    wb, wc, wh, cw = wb_ref[...], wc_ref[...], wh_ref[...], cw_ref[...]
    halo = jnp.zeros((CONV_HALO, wb.shape[1]), F32)
    for r in range(x_ref.shape[0] // rb):
        x = x_ref[r * rb:(r + 1) * rb, :]
        u = _dot(x, wc) * _dot(x, wh)
        z_ref[r * rb:(r + 1) * rb, :] = (_dot(x, wb) * _conv_block(u, halo, cw)).astype(BF16)
        halo = u[rb - CONV_HALO:rb]
    st_ref[0] = halo[CONV_HALO - CONV_PREV:CONV_HALO]


def _gate_conv_s_kernel(x_ref, wb_ref, wc_ref, wh_ref, cw_ref, prev_ref,
                        z_ref, st_ref, wb16_ref, wc16_ref, wh16_ref, *, shift):
    wb = wb_ref[...].astype(BF16)
    wc = wc_ref[...].astype(BF16)
    wh = wh_ref[...].astype(BF16)
    wb16_ref[...] = wb
    wc16_ref[...] = wc
    wh16_ref[...] = wh
    z, tail = _gate_conv_core(x_ref[...], wb, wc, wh, cw_ref[...], prev_ref[...], shift)
    z_ref[...] = z
    st_ref[...] = tail


def gate_conv_sample(x16, w_in, layer, cw, prev, shift):
    rows, d = x16.shape
    c = w_in.shape[2] // 3
    tn = _pick(c, 256, 128)
    nc = c // tn
    w_spec = lambda sec: pl.BlockSpec((None, d, tn), lambda n, sec=sec: (layer, 0, sec * nc + n))
    col = lambda r: pl.BlockSpec((r, tn), lambda n: (0, n))
    return pl.pallas_call(
        functools.partial(_gate_conv_s_kernel, shift=shift),
        grid=(nc,),
        in_specs=[pl.BlockSpec((rows, d), lambda n: (0, 0)), w_spec(0), w_spec(1), w_spec(2),
                  col(CONV_TAPS), col(CONV_PREV * shift)],
        out_specs=[col(rows), col(CONV_PREV * shift), col(d), col(d), col(d)],
        out_shape=[jax.ShapeDtypeStruct((rows, c), BF16),
                   jax.ShapeDtypeStruct((CONV_PREV * shift, c), F32),
                   jax.ShapeDtypeStruct((d, c), BF16),
                   jax.ShapeDtypeStruct((d, c), BF16),
                   jax.ShapeDtypeStruct((d, c), BF16)],
        compiler_params=_params(("arbitrary",)),
        name="gate_conv_sample",
    )(x16, w_in, w_in, w_in, cw, prev)


def gate_conv_prompt(x16, wb16, wc16, wh16, cw, batch):
    m, d = x16.shape
    s = m // batch
    c = wb16.shape[1]
    tn = _pick(c, 256, 128)
    w_spec = pl.BlockSpec((d, tn), lambda b, n: (0, n))
    return pl.pallas_call(
        functools.partial(_gate_conv_p_kernel, rb=_pick(s, 512, 256, 128)),
        grid=(batch, c // tn),
        in_specs=[pl.BlockSpec((s, d), lambda b, n: (b, 0), pipeline_mode=pl.Buffered(1)),
                  w_spec, w_spec, w_spec,
                  pl.BlockSpec((CONV_TAPS, tn), lambda b, n: (0, n))],
        out_specs=[pl.BlockSpec((s, tn), lambda b, n: (b, n)),
                   pl.BlockSpec((1, CONV_PREV, tn), lambda b, n: (b, 0, n))],
        out_shape=[jax.ShapeDtypeStruct((m, c), BF16),
                   jax.ShapeDtypeStruct((batch, CONV_PREV, c), F32)],
        compiler_params=_params(("arbitrary", "arbitrary")),
        name="gate_conv_prompt",
    )(x16, wb16, wc16, wh16, cw)


def _ffn_up_core(x, wg, wu, cwg, cwu, prev_g, prev_u, shift):
    cg, tail_g = _conv_rows(_dot(x, wg), prev_g, cwg, shift)
    cu, tail_u = _conv_rows(_dot(x, wu), prev_u, cwu, shift)
    return _silu_gate(cg, cu), tail_g, tail_u


def _silu_gate(cg, cu):
    return ((cg * (1.0 / (1.0 + jnp.exp(-cg)))) * cu).astype(BF16)


def _ffn_up_p_kernel(x_ref, wg_ref, wu_ref, cwg_ref, cwu_ref, a_ref, sg_ref, su_ref, *, rb):
    wg, wu, cwg, cwu = wg_ref[...], wu_ref[...], cwg_ref[...], cwu_ref[...]
    halo_g = jnp.zeros((CONV_HALO, wg.shape[1]), F32)
    halo_u = halo_g
    for r in range(x_ref.shape[0] // rb):
        x = x_ref[r * rb:(r + 1) * rb, :]
        hg = _dot(x, wg)
        hu = _dot(x, wu)
        a_ref[r * rb:(r + 1) * rb, :] = _silu_gate(_conv_block(hg, halo_g, cwg),
                                                   _conv_block(hu, halo_u, cwu))
        halo_g = hg[rb - CONV_HALO:rb]
        halo_u = hu[rb - CONV_HALO:rb]
    sg_ref[0] = halo_g[CONV_HALO - CONV_PREV:CONV_HALO]
    su_ref[0] = halo_u[CONV_HALO - CONV_PREV:CONV_HALO]


def _ffn_up_s_kernel(x_ref, wg_ref, wu_ref, cwg_ref, cwu_ref, pg_ref, pu_ref,
                     a_ref, sg_ref, su_ref, wg16_ref, wu16_ref, *, shift):
    wg = wg_ref[...].astype(BF16)
    wu = wu_ref[...].astype(BF16)
    wg16_ref[...] = wg
    wu16_ref[...] = wu
    act, tg, tu = _ffn_up_core(x_ref[...], wg, wu, cwg_ref[...], cwu_ref[...],
                               pg_ref[...], pu_ref[...], shift)
    a_ref[...] = act
    sg_ref[...] = tg
    su_ref[...] = tu


def ffn_up_sample(x16, w_up, layer, cw, prev, shift):
    rows, d = x16.shape
    f = w_up.shape[2] // 2
    tn = _pick(f, 256, 128)
    nf = f // tn
    sec = lambda r, k: pl.BlockSpec((r, tn), lambda n, k=k: (0, k * nf + n))
    wsec = lambda k: pl.BlockSpec((None, d, tn), lambda n, k=k: (layer, 0, k * nf + n))
    col = lambda r: pl.BlockSpec((r, tn), lambda n: (0, n))
    pr = CONV_PREV * shift
    return pl.pallas_call(
        functools.partial(_ffn_up_s_kernel, shift=shift),
        grid=(nf,),
        in_specs=[pl.BlockSpec((rows, d), lambda n: (0, 0)), wsec(0), wsec(1),
                  sec(CONV_TAPS, 0), sec(CONV_TAPS, 1), sec(pr, 0), sec(pr, 1)],
        out_specs=[col(rows), col(pr), col(pr), col(d), col(d)],
        out_shape=[jax.ShapeDtypeStruct((rows, f), BF16),
                   jax.ShapeDtypeStruct((pr, f), F32),
                   jax.ShapeDtypeStruct((pr, f), F32),
                   jax.ShapeDtypeStruct((d, f), BF16),
                   jax.ShapeDtypeStruct((d, f), BF16)],
        compiler_params=_params(("arbitrary",)),
        name="ffn_up_sample",
    )(x16, w_up, w_up, cw, cw, prev, prev)


def ffn_up_prompt(x16, wg16, wu16, cw, batch):
    m, d = x16.shape
    s = m // batch
    f = wg16.shape[1]
    tn = _pick(f, 256, 128)
    nf = f // tn
    w_spec = pl.BlockSpec((d, tn), lambda b, n: (0, n))
    st_spec = pl.BlockSpec((1, CONV_PREV, tn), lambda b, n: (b, 0, n))
    return pl.pallas_call(
        functools.partial(_ffn_up_p_kernel, rb=_pick(s, 512, 256, 128)),
        grid=(batch, nf),
        in_specs=[pl.BlockSpec((s, d), lambda b, n: (b, 0), pipeline_mode=pl.Buffered(1)),
                  w_spec, w_spec,
                  pl.BlockSpec((CONV_TAPS, tn), lambda b, n: (0, n)),
                  pl.BlockSpec((CONV_TAPS, tn), lambda b, n: (0, nf + n))],
        out_specs=[pl.BlockSpec((s, tn), lambda b, n: (b, n)), st_spec, st_spec],
        out_shape=[jax.ShapeDtypeStruct((m, f), BF16),
                   jax.ShapeDtypeStruct((batch, CONV_PREV, f), F32),
                   jax.ShapeDtypeStruct((batch, CONV_PREV, f), F32)],
        compiler_params=_params(("arbitrary", "arbitrary")),
        name="ffn_up_prompt",
    )(x16, wg16, wu16, cw, cw)


def _mm_p_kernel(*refs, alpha, has_res):
    if has_res:
        x_ref, w_ref, r_ref, o_ref = refs
    else:
        x_ref, w_ref, o_ref = refs
    acc = _dot(x_ref[...], w_ref[...])
    if has_res:
        acc = alpha * r_ref[...] + acc
    o_ref[...] = acc.astype(o_ref.dtype)


def _mm_s_kernel(*refs, alpha, has_res):
    if has_res:
        x_ref, w_ref, r_ref, o_ref, w16_ref = refs
    else:
        x_ref, w_ref, o_ref, w16_ref = refs
    w = w_ref[...].astype(BF16)
    w16_ref[...] = w
    acc = _dot(x_ref[...], w)
    if has_res:
        acc = alpha * r_ref[...] + acc
    o_ref[...] = acc.astype(o_ref.dtype)


def mm_sample(x16, w, layer, n_out, out_dtype, res=None, alpha=None):
    rows, k = x16.shape
    tn = _pick(n_out, 512, 256, 128) if k <= 4096 else _pick(n_out, 256, 128)
    col = lambda r: pl.BlockSpec((r, tn), lambda n: (0, n))
    in_specs = [pl.BlockSpec((rows, k), lambda n: (0, 0)),
                pl.BlockSpec((None, k, tn), lambda n: (layer, 0, n))]
    args = [x16, w]
    if res is not None:
        in_specs.append(col(rows))
        args.append(res)
    return pl.pallas_call(
        functools.partial(_mm_s_kernel, alpha=alpha, has_res=res is not None),
        grid=(n_out // tn,),
        in_specs=in_specs,
        out_specs=[col(rows), col(k)],
        out_shape=[jax.ShapeDtypeStruct((rows, n_out), out_dtype),
                   jax.ShapeDtypeStruct((k, n_out), BF16)],
        compiler_params=_params(("arbitrary",)),
        name="mm_sample",
    )(*args)


def mm_prompt(x16, w16, n_out, col_off, out_dtype, res=None, alpha=None):
    m, k = x16.shape
    if k <= 4096:
        tm, tn = _pick(m, 1024, 512, 256), _pick(n_out, 512, 256, 128)
    else:
        tm, tn = _pick(m, 512, 256), _pick(n_out, 512, 256, 128)
    assert col_off % tn == 0
    off = col_off // tn
    in_specs = [pl.BlockSpec((tm, k), lambda i, n: (i, 0)),
                pl.BlockSpec((k, tn), lambda i, n: (0, off + n))]
    args = [x16, w16]
    if res is not None:
        in_specs.append(pl.BlockSpec((tm, tn), lambda i, n: (i, n)))
        args.append(res)
    return pl.pallas_call(
        functools.partial(_mm_p_kernel, alpha=alpha, has_res=res is not None),
        grid=(m // tm, n_out // tn),
        in_specs=in_specs,
        out_specs=pl.BlockSpec((tm, tn), lambda i, n: (i, n)),
        out_shape=jax.ShapeDtypeStruct((m, n_out), out_dtype),
        compiler_params=_params(("arbitrary", "arbitrary")),
        name="mm_prompt",
    )(*args)


def _ln_kernel(y_ref, g_ref, b_ref, xf_ref, xb_ref):
    y = y_ref[...]
    mu = jnp.mean(y, axis=-1, keepdims=True)
    dev = y - mu
    var = jnp.mean(dev * dev, axis=-1, keepdims=True)
    out = dev * lax.rsqrt(var + LN_EPS) * g_ref[...] + b_ref[...]
    xf_ref[...] = out
    xb_ref[...] = out.astype(BF16)


def layer_norm_rows(y, g, b):
    m, d = y.shape
    tr = _pick(m, 256, 128)
    row = pl.BlockSpec((tr, d), lambda i: (i, 0))
    vec = pl.BlockSpec((1, d), lambda i: (0, 0))
    return pl.pallas_call(
        _ln_kernel,
        grid=(m // tr,),
        in_specs=[row, vec, vec],
        out_specs=[row, row],
        out_shape=[jax.ShapeDtypeStruct((m, d), F32), jax.ShapeDtypeStruct((m, d), BF16)],
        compiler_params=_params(("arbitrary",)),
        name="layer_norm",
    )(y, g.reshape(1, d), b.reshape(1, d))


def _sink_softmax_pv(s, sink, v16):
    m = jnp.maximum(jnp.max(s, axis=-1, keepdims=True), sink)
    p = jnp.exp(s - m)
    den = jnp.sum(p, axis=-1, keepdims=True) + jnp.exp(sink - m)
    return _dot((p / den).astype(BF16), v16)


def _swa_p_kernel(q_ref, kp_ref, ko_ref, vp_ref, vo_ref, sink_ref, o_ref, *, grp, pre, rest):
    blk = q_ref.shape[0]
    rows = grp * blk
    first = pl.program_id(1) == 0
    kcat = jnp.concatenate([kp_ref[...], ko_ref[...]], axis=0).astype(BF16)
    vcat = jnp.concatenate([vp_ref[...], vo_ref[...]], axis=0).astype(BF16)
    tl = lax.broadcasted_iota(jnp.int32, (rows, 2 * blk), 0) % blk
    j = lax.broadcasted_iota(jnp.int32, (rows, 2 * blk), 1)
    mask = (j >= tl) & (j <= tl + WINDOW) & ((j >= blk) | jnp.logical_not(first))
    hd = SW_HEAD_DIM
    unit = (lax.broadcasted_iota(jnp.int32, (2 * blk, hd), 1) == 0).astype(BF16)
    for h in range(SW_KV_HEADS):
        heads = range(h * grp, (h + 1) * grp)
        q = jnp.concatenate([q_ref[:, a * hd:(a + 1) * hd] for a in heads], axis=0) * pre
        sink = jnp.concatenate([jnp.full((blk, 1), sink_ref[a], F32) for a in heads], axis=0)
        s = _dot_nt(q, kcat[:, h * hd:(h + 1) * hd])
        if rest != 1.0:
            s = s * rest
        s = jnp.where(mask, s, NEG_INF)
        m = jnp.maximum(jnp.max(s, axis=-1, keepdims=True), sink)
        vh = jnp.concatenate([vcat[:, h * hd:(h + 1) * hd], unit], axis=1)
        ov = _dot(jnp.exp(s - m).astype(BF16), vh)
        den = ov[:, hd:hd + 1] + jnp.exp(sink - m)
        o = (ov[:, :hd] / den).astype(BF16)
        for g, a in enumerate(heads):
            o_ref[:, a * hd:(a + 1) * hd] = o[g * blk:(g + 1) * blk]


def swa_prompt(q16, k, v, sinks, batch):
    m, qw = q16.shape
    s = m // batch
    blk = WINDOW
    nb = s // blk
    kw = SW_KV_HEADS * SW_HEAD_DIM
    grp = qw // kw
    own = pl.BlockSpec((blk, kw), lambda b, i: (b * nb + i, 0))
    prev = pl.BlockSpec((blk, kw), lambda b, i: (b * nb + jnp.maximum(i - 1, 0), 0))
    pre, rest = _split_scale(SW_HEAD_DIM ** -0.5)
    return pl.pallas_call(
        functools.partial(_swa_p_kernel, grp=grp, pre=pre, rest=rest),
        grid=(batch, nb),
        in_specs=[pl.BlockSpec((blk, qw), lambda b, i: (b * nb + i, 0)),
                  prev, own, prev, own,
                  pl.BlockSpec(memory_space=pltpu.SMEM)],
        out_specs=pl.BlockSpec((blk, qw), lambda b, i: (b * nb + i, 0)),
        out_shape=jax.ShapeDtypeStruct((m, qw), BF16),
        compiler_params=_params(("arbitrary", "arbitrary")),
        name="swa_prompt",
    )(q16, k, k, v, v, sinks)


def _swa_s_kernel(q_ref, ck_ref, cv_ref, kn_ref, vn_ref, sink_ref, o_ref, *, steps, scale):
    wb = ck_ref.shape[1]
    pad = jnp.zeros((wb - kn_ref.shape[1], ck_ref.shape[2]), F32)
    kcat = jnp.concatenate([ck_ref[0], kn_ref[0], pad], axis=0).astype(BF16)
    vcat = jnp.concatenate([cv_ref[0], vn_ref[0], pad], axis=0).astype(BF16)
    rows = q_ref.shape[2]
    t = lax.broadcasted_iota(jnp.int32, (rows, 2 * wb), 0) % steps
    j = lax.broadcasted_iota(jnp.int32, (rows, 2 * wb), 1)
    mask = (j >= t + wb - WINDOW) & (j <= t + wb)
    hd = SW_HEAD_DIM
    for h in range(SW_KV_HEADS):
        s = _dot_nt(q_ref[0, h], kcat[:, h * hd:(h + 1) * hd]) * scale
        s = jnp.where(mask, s, NEG_INF)
        o_ref[0, h] = _sink_softmax_pv(s, sink_ref[h], vcat[:, h * hd:(h + 1) * hd])


def swa_sample(q16, cache_k, cache_v, k_new, v_new, sink_col, steps):
    bd, kvh, rows, hd = q16.shape
    wb, kw = cache_k.shape[1], cache_k.shape[2]
    per_b = lambda a: pl.BlockSpec((1,) + a.shape[1:], lambda b: (b,) + (0,) * (a.ndim - 1))
    return pl.pallas_call(
        functools.partial(_swa_s_kernel, steps=steps, scale=SW_HEAD_DIM ** -0.5),
        grid=(bd,),
        in_specs=[per_b(q16), per_b(cache_k), per_b(cache_v), per_b(k_new), per_b(v_new),
                  pl.BlockSpec(sink_col.shape, lambda b: (0, 0, 0))],
        out_specs=pl.BlockSpec((1, kvh, rows, hd), lambda b: (b, 0, 0, 0)),
        out_shape=jax.ShapeDtypeStruct((bd, kvh, rows, hd), F32),
        compiler_params=_params(("arbitrary",)),
        name="swa_sample",
    )(q16, cache_k, cache_v, k_new, v_new, sink_col)


def _log_sigmoid(z):
    return jnp.minimum(z, 0.0) - jnp.log1p(jnp.exp(-jnp.abs(z)))


def _fox_logf_p_kernel(x_ref, wf_ref, bf_ref, lf_ref, c_ref, ct_ref, *, blk):
    lf = _log_sigmoid(_dot(x_ref[...], wf_ref[...].astype(BF16)) + bf_ref[...])
    lf_ref[...] = lf
    r = lax.broadcasted_iota(jnp.int32, (blk, blk), 0)
    c = lax.broadcasted_iota(jnp.int32, (blk, blk), 1)
    upto_rows = (c <= r).astype(F32)
    upto_cols = (r <= c).astype(F32)
    carry = jnp.zeros((1, lf.shape[1]), F32)
    carry_t = jnp.zeros((lf.shape[1], 1), F32)
    for i in range(lf.shape[0] // blk):
        part = lf[i * blk:(i + 1) * blk]
        cb = jnp.dot(upto_rows, part, precision=lax.Precision.HIGHEST,
                     preferred_element_type=F32) + carry
        c_ref[i * blk:(i + 1) * blk, :] = cb
        carry = cb[blk - 1:blk, :]
        cbt = lax.dot_general(part, upto_cols, (((0,), (0,)), ((), ())),
                              precision=lax.Precision.HIGHEST, preferred_element_type=F32) + carry_t
        ct_ref[0, :, i * blk:(i + 1) * blk] = cbt
        carry_t = cbt[:, blk - 1:blk]


def fox_logf_prompt(x16, wf, bf, batch):
    m, d = x16.shape
    s = m // batch
    h = wf.shape[1]
    row = pl.BlockSpec((s, h), lambda b: (b, 0))
    return pl.pallas_call(
        functools.partial(_fox_logf_p_kernel, blk=_pick(s, 256, 128)),
        grid=(batch,),
        in_specs=[pl.BlockSpec((s, d), lambda b: (b, 0)),
                  pl.BlockSpec((d, h), lambda b: (0, 0)),
                  pl.BlockSpec((1, h), lambda b: (0, 0))],
        out_specs=[row, row, pl.BlockSpec((1, h, s), lambda b: (b, 0, 0))],
        out_shape=[jax.ShapeDtypeStruct((m, h), F32), jax.ShapeDtypeStruct((m, h), F32),
                   jax.ShapeDtypeStruct((batch, h, s), F32)],
        compiler_params=_params(("arbitrary",)),
        name="fox_logf_prompt",
    )(x16, wf, bf.reshape(1, h))


def _fox_logf_s_kernel(x_ref, wf_ref, bf_ref, lf_ref, c_ref, *, shift):
    lf = _log_sigmoid(_dot(x_ref[...], wf_ref[...].astype(BF16)) + bf_ref[...])
    lf_ref[...] = lf
    run = lf[0:shift]
    c_ref[0:shift, :] = run
    for t in range(1, lf.shape[0] // shift):
        run = run + lf[t * shift:(t + 1) * shift]
        c_ref[t * shift:(t + 1) * shift, :] = run


def fox_logf_sample(x16, wf, bf, shift):
    rows, d = x16.shape
    h = wf.shape[1]
    full = lambda a, b: pl.BlockSpec((a, b), lambda i: (0, 0))
    return pl.pallas_call(
        functools.partial(_fox_logf_s_kernel, shift=shift),
        grid=(1,),
        in_specs=[full(rows, d), full(d, h), full(1, h)],
        out_specs=[full(rows, h), full(rows, h)],
        out_shape=[jax.ShapeDtypeStruct((rows, h), F32), jax.ShapeDtypeStruct((rows, h), F32)],
        compiler_params=_params(("arbitrary",)),
        name="fox_logf_sample",
    )(x16, wf, bf.reshape(1, h))


def _fox_p_kernel(q_ref, k_ref, v_ref, c_ref, ck_ref, o_ref, k16_ref, v16_ref, cq_ref, m_ref, acc_ref,
                  *, grp, pre, rest):
    h = pl.program_id(1)
    qi = pl.program_id(2)
    tq = q_ref.shape[0]
    tk = tq
    hd = FOX_HEAD_DIM
    rows = grp * tq
    c_exp = rest * LOG2_E

    @pl.when(qi == 0)
    def _():
        unit = (lax.broadcasted_iota(jnp.int32, v_ref.shape, 1) == 0).astype(BF16)
        k16_ref[...] = k_ref[...].astype(BF16)
        v16_ref[...] = jnp.concatenate([v_ref[...].astype(BF16), unit], axis=1)

    q = jnp.concatenate([q_ref[:, g * hd:(g + 1) * hd] for g in range(grp)], axis=0) * pre
    c_blk = c_ref[...]
    lane = lax.broadcasted_iota(jnp.int32, c_blk.shape, 1)
    for g in range(grp):
        col = jnp.sum(jnp.where(lane == h * grp + g, c_blk, 0.0), axis=-1, keepdims=True)
        cq_ref[g * tq:(g + 1) * tq, :] = jnp.broadcast_to(col * (1.0 / rest), (tq, hd))
    m_ref[...] = jnp.full(m_ref.shape, NEG_INF, F32)
    acc_ref[...] = jnp.zeros(acc_ref.shape, F32)

    def update(j, diagonal):
        off = pl.multiple_of(j * tk, tk)
        z = _dot_nt(q, k16_ref[pl.ds(off, tk), :])
        cq = cq_ref[...]
        t = jnp.concatenate([z[:, i * hd:(i + 1) * hd] + cq for i in range(tk // hd)], axis=1)
        ck = ck_ref[0, 0, j] * (1.0 / rest)
        t = (t.reshape(grp, tq, tk) - ck[:, None, :]).reshape(rows, tk)
        if diagonal:
            r = lax.broadcasted_iota(jnp.int32, (rows, tk), 0) % tq
            c = lax.broadcasted_iota(jnp.int32, (rows, tk), 1)
            t = jnp.where(c <= r, t, NEG_INF)
        m_old = m_ref[...]
        m_new = jnp.maximum(m_old, jnp.max(t, axis=-1, keepdims=True))
        a = jnp.exp2((m_old - m_new) * c_exp)
        p = jnp.exp2((t - m_new) * c_exp)
        acc_ref[...] = a * acc_ref[...] + _dot(p.astype(BF16), v16_ref[pl.ds(off, tk), :])
        m_ref[...] = m_new

    def below_diagonal(j, carry):
        update(j, False)
        return carry

    lax.fori_loop(0, qi, below_diagonal, 0)
    update(qi, True)
    acc = acc_ref[...]
    out = acc[:, :hd] / acc[:, hd:hd + 1]
    for g in range(grp):
        o_ref[:, g * hd:(g + 1) * hd] = out[g * tq:(g + 1) * tq].astype(BF16)


def fox_prompt(q16, k, v, c, c_k, batch):
    m, qw = q16.shape
    heads = c.shape[1]
    s = m // batch
    hd, kvh = FOX_HEAD_DIM, FOX_KV_HEADS
    grp = qw // (kvh * hd)
    nk, tq = c_k.shape[2], c_k.shape[4]
    nq = s // tq
    pre, rest = _split_scale(hd ** -0.5)
    kv_spec = pl.BlockSpec((s, hd), lambda b, h, qi: (b, h))
    return pl.pallas_call(
        functools.partial(_fox_p_kernel, grp=grp, pre=pre, rest=rest),
        grid=(batch, kvh, nq),
        in_specs=[pl.BlockSpec((tq, grp * hd), lambda b, h, qi: (b * nq + qi, h)),
                  kv_spec, kv_spec,
                  pl.BlockSpec((tq, heads), lambda b, h, qi: (b * nq + qi, 0)),
                  pl.BlockSpec((1, 1, nk, grp, tq), lambda b, h, qi: (b, h, 0, 0, 0))],
        out_specs=pl.BlockSpec((tq, grp * hd), lambda b, h, qi: (b * nq + qi, h)),
        out_shape=jax.ShapeDtypeStruct((m, qw), BF16),
        scratch_shapes=[pltpu.VMEM((s, hd), BF16), pltpu.VMEM((s, 2 * hd), BF16),
                        pltpu.VMEM((grp * tq, hd), F32), pltpu.VMEM((grp * tq, 1), F32),
                        pltpu.VMEM((grp * tq, 2 * hd), F32)],
        compiler_params=_params(("arbitrary", "arbitrary", "arbitrary")),
        name="fox_prompt",
    )(q16, k, v, c, c_k)


def _fox_s_kernel(pt_ref, q_ref, ccol_ref, cnt_ref, kn_ref, vn_ref, *rest, pages, steps, grp, scale):
    k_refs = rest[0:pages]
    v_refs = rest[pages:2 * pages]
    lf_refs = rest[2 * pages:3 * pages]
    o_ref, qbd_ref, m_ref, l_ref, acc_ref, carry_ref = rest[3 * pages:]
    p = pl.program_id(1)
    rows, hd = q_ref.shape[1], q_ref.shape[2]
    heads = rows // steps
    kvh = heads // grp
    psz = cnt_ref.shape[2]
    row = lax.broadcasted_iota(jnp.int32, (rows, hd), 0)
    row_kvh = (row % heads) // grp

    def page_update(k16, v16, neg_ck, mask):
        s = _dot_nt(qbd_ref[...], k16) * scale
        s = s + (ccol_ref[0] + jnp.concatenate([neg_ck] * steps, axis=0))
        if mask is not None:
            s = jnp.where(mask, s, NEG_INF)
        m_old = m_ref[...]
        m_new = jnp.maximum(m_old, jnp.max(s, axis=-1, keepdims=True))
        a = jnp.exp(m_old - m_new)
        pr = jnp.exp(s - m_new)
        l_ref[...] = a * l_ref[...] + jnp.sum(pr, axis=-1, keepdims=True)
        acc_ref[...] = a * acc_ref[...] + _dot(pr.astype(BF16), v16)
        m_ref[...] = m_new

    @pl.when(p == 0)
    def _():
        q = q_ref[0]
        for h in range(kvh):
            qbd_ref[:, h * hd:(h + 1) * hd] = jnp.where(row_kvh == h, q, jnp.zeros_like(q))
        m_ref[...] = jnp.full(m_ref.shape, NEG_INF, F32)
        l_ref[...] = jnp.zeros(l_ref.shape, F32)
        acc_ref[...] = jnp.zeros(acc_ref.shape, F32)
        carry_ref[...] = jnp.zeros(carry_ref.shape, F32)
        pad = jnp.zeros((psz - kn_ref.shape[1], kn_ref.shape[2]), F32)
        k16 = jnp.concatenate([kn_ref[0], pad], axis=0).astype(BF16)
        v16 = jnp.concatenate([vn_ref[0], pad], axis=0).astype(BF16)
        t = lax.broadcasted_iota(jnp.int32, (rows, psz), 0) // heads
        j = lax.broadcasted_iota(jnp.int32, (rows, psz), 1)
        page_update(k16, v16, -cnt_ref[0], j <= t)

    def flat16(ref):
        rows2d = ref.reshape(psz * kvh, hd)
        return jnp.concatenate([rows2d[pl.ds(h, psz, stride=kvh), :] for h in range(kvh)],
                               axis=1).astype(BF16)

    @pl.when(p > 0)
    def _():
        r = lax.broadcasted_iota(jnp.int32, (psz, 2 * psz), 0)
        c = lax.broadcasted_iota(jnp.int32, (psz, 2 * psz), 1)
        later_and_total = ((r > c) | (c >= psz)).astype(F32)
        carry = carry_ref[...]
        parts = []
        for i in range(pages):
            w = lax.dot_general(lf_refs[i][...], later_and_total, (((0,), (0,)), ((), ())),
                                precision=lax.Precision.HIGHEST, preferred_element_type=F32)
            parts.append(w[:, :psz] + carry)
            carry = carry + w[:, psz:psz + 1]
        carry_ref[...] = carry
        k16 = jnp.concatenate([flat16(k_refs[i]) for i in range(pages)], axis=0)
        v16 = jnp.concatenate([flat16(v_refs[i]) for i in range(pages)], axis=0)
        page_update(k16, v16, jnp.concatenate(parts, axis=1), None)

    @pl.when(p == pl.num_programs(1) - 1)
    def _():
        out = jnp.zeros((rows, hd), F32)
        for h in range(kvh):
            out = out + jnp.where(row_kvh == h, acc_ref[:, h * hd:(h + 1) * hd], 0.0)
        o_ref[0] = (out / l_ref[...]).astype(BF16)


def fox_sample(page_table, q16, c_col, c_new_t, k_new, v_new, pool_k, pool_v, pool_lf, layer, steps):
    bd, rows, hd = q16.shape
    heads = rows // steps
    psz, kvh = pool_k.shape[2], pool_k.shape[3]
    kw = kvh * hd
    n_pages = page_table.shape[1]
    pages = _pick(n_pages, 8, 4, 2, 1)
    n_steps = n_pages // pages + 1

    def page_idx(i, tail):
        return lambda b, p, pt: (layer, pt[b, n_pages - 1 - (jnp.maximum(p - 1, 0) * pages + i)]) + tail

    per_b = lambda a: pl.BlockSpec((1,) + a.shape[1:], lambda b, p, pt: (b,) + (0,) * (a.ndim - 1))
    kv_specs = lambda: [pl.BlockSpec((None, None, psz, kvh, hd), page_idx(i, (0, 0, 0))) for i in range(pages)]
    lf_specs = [pl.BlockSpec((None, None, psz, heads), page_idx(i, (0, 0))) for i in range(pages)]
    grid_spec = pltpu.PrefetchScalarGridSpec(
        num_scalar_prefetch=1,
        grid=(bd, n_steps),
        in_specs=[per_b(q16), per_b(c_col), per_b(c_new_t), per_b(k_new), per_b(v_new)]
        + kv_specs() + kv_specs() + lf_specs,
        out_specs=pl.BlockSpec((1, rows, hd), lambda b, p, pt: (b, 0, 0)),
        scratch_shapes=[pltpu.VMEM((rows, kw), BF16), pltpu.VMEM((rows, 1), F32),
                        pltpu.VMEM((rows, 1), F32), pltpu.VMEM((rows, kw), F32),
                        pltpu.VMEM((heads, 1), F32)],
    )
    return pl.pallas_call(
        functools.partial(_fox_s_kernel, pages=pages, steps=steps, grp=heads // FOX_KV_HEADS,
                          scale=FOX_HEAD_DIM ** -0.5),
        grid_spec=grid_spec,
        out_shape=jax.ShapeDtypeStruct((bd, rows, hd), BF16),
        compiler_params=_params(("arbitrary", "arbitrary")),
        name="fox_sample",
    )(page_table, q16, c_col, c_new_t, k_new, v_new,
      *([pool_k] * pages), *([pool_v] * pages), *([pool_lf] * pages))


def _time_major(a):
    return jnp.swapaxes(a, 0, 1).reshape((a.shape[0] * a.shape[1],) + a.shape[2:])


def _batch_major(a, bd):
    return jnp.swapaxes(a.reshape((a.shape[0] // bd, bd) + a.shape[1:]), 0, 1)


def _pad_rows(a, rows):
    return jnp.pad(a, ((0, 0), (0, rows - a.shape[1]), (0, 0)))


def kernel(x_prompt, x_sample, state_conv, cache_win_k, cache_win_v, cache_k, cache_v, cache_logf,
           state_ffn, page_table, w_in_a, conv_a, w_out_a, w_qkv_b, sinks_b, w_o_b, w_qkvf_c, b_f_c,
           w_o_c, ln1_g, ln1_b, w_up, conv_f, w_down, ln2_g, ln2_b):
    batch, seq, d = x_prompt.shape
    bd, steps, _ = x_sample.shape
    depth = ln1_g.shape[0]
    alpha = (2 * depth) ** 0.25
    n_mixers = 3
    f = w_down.shape[1]

    xp = x_prompt.reshape(batch * seq, d)
    xp16 = xp.astype(BF16)
    xs = _time_major(x_sample)
    xs16 = xs.astype(BF16)

    conv_p, conv_s, wk_p, wv_p, wk_s, wv_s = [], [], [], [], [], []
    fk_p, fv_p, fl_p, fk_s, fv_s, fl_s = [], [], [], [], [], []
    ffn_p, ffn_s = [], []

    for i in range(depth):
        mix, j = i % n_mixers, i // n_mixers
        if mix == 0:
            zs, st_s, wb16, wc16, wh16 = gate_conv_sample(xs16, w_in_a, j, conv_a[j],
                                                          _time_major(state_conv[j]), bd)
            ys, wo16 = mm_sample(zs, w_out_a, j, d, F32, res=xs, alpha=alpha)
            zp, st_p = gate_conv_prompt(xp16, wb16, wc16, wh16, conv_a[j], batch)
            yp = mm_prompt(zp, wo16, d, 0, F32, res=xp, alpha=alpha)
            conv_p.append(st_p)
            conv_s.append(_batch_major(st_s, bd))
        elif mix == 1:
            hd, kvh = SW_HEAD_DIM, SW_KV_HEADS
            kw = kvh * hd
            grp = d // kw
            qkv_s, w16 = mm_sample(xs16, w_qkv_b, j, d + 2 * kw, F32)
            qp = mm_prompt(xp16, w16, d, 0, BF16)
            kp = mm_prompt(xp16, w16, kw, d, F32)
            vp = mm_prompt(xp16, w16, kw, d + kw, F32)
            op = swa_prompt(qp, kp, vp, sinks_b[j], batch)
            qs = qkv_s[:, :d].astype(BF16).reshape(steps, bd, kvh, grp, hd)
            qs = qs.transpose(1, 2, 3, 0, 4).reshape(bd, kvh, grp * steps, hd)
            kn = _batch_major(qkv_s[:, d:d + kw], bd)
            vn = _batch_major(qkv_s[:, d + kw:], bd)
            ck = cache_win_k[j].reshape(bd, -1, kw)
            cv = cache_win_v[j].reshape(bd, -1, kw)
            sink_col = jnp.repeat(sinks_b[j].reshape(kvh, grp), steps, axis=1)[..., None]
            os_ = swa_sample(qs, ck, cv, _pad_rows(kn, 8), _pad_rows(vn, 8), sink_col, steps)
            os_ = os_.reshape(bd, kvh, grp, steps, hd).transpose(3, 0, 1, 2, 4).reshape(steps * bd, d)
            ys, wo16 = mm_sample(os_.astype(BF16), w_o_b, j, d, F32, res=xs, alpha=alpha)
            yp = mm_prompt(op, wo16, d, 0, F32, res=xp, alpha=alpha)
            keep = min(WINDOW, seq)
            wk_p.append(kp.reshape(batch, seq, kvh, hd)[:, seq - keep:])
            wv_p.append(vp.reshape(batch, seq, kvh, hd)[:, seq - keep:])
            wb = ck.shape[1]
            wk_s.append(jnp.concatenate([ck, kn], axis=1)[:, -wb:].reshape(bd, wb, kvh, hd))
            wv_s.append(jnp.concatenate([cv, vn], axis=1)[:, -wb:].reshape(bd, wb, kvh, hd))
        else:
            hd, kvh = FOX_HEAD_DIM, FOX_KV_HEADS
            kw = kvh * hd
            heads = d // hd
            grp = heads // kvh
            wf = w_qkvf_c[j, :, d + 2 * kw:]
            qkv_s, w16 = mm_sample(xs16, w_qkvf_c, j, d + 2 * kw, F32)
            lf_s, c_s = fox_logf_sample(xs16, wf, b_f_c[j], bd)
            qp = mm_prompt(xp16, w16, d, 0, BF16)
            kp = mm_prompt(xp16, w16, kw, d, F32)
            vp = mm_prompt(xp16, w16, kw, d + kw, F32)
            lf_p, c_p, ct_p = fox_logf_prompt(xp16, wf, b_f_c[j], batch)
            tk = _pick(seq, 512, 256, 128)
            c_k = ct_p.reshape(batch, kvh, grp, seq // tk, tk).transpose(0, 1, 3, 2, 4)
            op = fox_prompt(qp, kp, vp, c_p, c_k, batch)
            qs = _batch_major(qkv_s[:, :d].astype(BF16), bd).reshape(bd, steps * heads, hd)
            kn = _batch_major(qkv_s[:, d:d + kw], bd)
            vn = _batch_major(qkv_s[:, d + kw:], bd)
            c_b = _batch_major(c_s, bd)
            c_col = c_b.reshape(bd, steps * heads, 1)
            c_new_t = jnp.pad(c_b.transpose(0, 2, 1), ((0, 0), (0, 0), (0, PAGE_SIZE - steps)))
            os_ = fox_sample(page_table, qs, c_col, c_new_t, _pad_rows(kn, 8), _pad_rows(vn, 8),
                             cache_k, cache_v, cache_logf, j, steps)
            os_ = _time_major(os_.reshape(bd, steps, d))
            ys, wo16 = mm_sample(os_, w_o_c, j, d, F32, res=xs, alpha=alpha)
            yp = mm_prompt(op, wo16, d, 0, F32, res=xp, alpha=alpha)
            fk_p.append(kp.reshape(batch, seq, kvh, hd))
            fv_p.append(vp.reshape(batch, seq, kvh, hd))
            fl_p.append(lf_p.reshape(batch, seq, heads))
            fk_s.append(kn.reshape(bd, steps, kvh, hd))
            fv_s.append(vn.reshape(bd, steps, kvh, hd))
            fl_s.append(_batch_major(lf_s, bd))
        xs, xs16 = layer_norm_rows(ys, ln1_g[i], ln1_b[i])
        xp, xp16 = layer_norm_rows(yp, ln1_g[i], ln1_b[i])

        a_s, sg_s, su_s, wg16, wu16 = ffn_up_sample(xs16, w_up, i, conv_f[i],
                                                    _time_major(state_ffn[i]), bd)
        ys, wd16 = mm_sample(a_s, w_down, i, d, F32, res=xs, alpha=alpha)
        a_p, sg_p, su_p = ffn_up_prompt(xp16, wg16, wu16, conv_f[i], batch)
        yp = mm_prompt(a_p, wd16, d, 0, F32, res=xp, alpha=alpha)
        ffn_p.append(jnp.concatenate([sg_p, su_p], axis=-1))
        ffn_s.append(_batch_major(jnp.concatenate([sg_s, su_s], axis=-1), bd))
        xs, xs16 = layer_norm_rows(ys, ln2_g[i], ln2_b[i])
        xp, xp16 = layer_norm_rows(yp, ln2_g[i], ln2_b[i])

    return (xp.reshape(batch, seq, d), _batch_major(xs, bd),
            jnp.stack(conv_p), jnp.stack(conv_s), jnp.stack(wk_p), jnp.stack(wv_p),
            jnp.stack(wk_s), jnp.stack(wv_s), jnp.stack(fk_p), jnp.stack(fv_p), jnp.stack(fl_p),
            jnp.stack(fk_s), jnp.stack(fv_s), jnp.stack(fl_s), jnp.stack(ffn_p), jnp.stack(ffn_s))
```

```python
import functools
import math

import jax
import jax.numpy as jnp
from jax import lax
from jax.experimental import pallas as pl
from jax.experimental.pallas import tpu as pltpu

F32 = jnp.float32
BF16 = jnp.bfloat16

CONV_TAPS = 3
CONV_PREV = CONV_TAPS - 1
CONV_HALO = 8
SW_HEAD_DIM = 64
SW_KV_HEADS = 8
WINDOW = 128
FOX_HEAD_DIM = 128
FOX_KV_HEADS = 8
PAGE_SIZE = 128
LN_EPS = 1e-5
NEG_INF = -1e30
LOG2_E = math.log2(math.e)

V7X_VMEM_BYTES = 64 * 1024 * 1024
VMEM_LIMIT = V7X_VMEM_BYTES - 8 * 1024 * 1024
LANE = 128


def _params(semantics):
    return pltpu.CompilerParams(dimension_semantics=semantics, vmem_limit_bytes=VMEM_LIMIT)


def _dot(a, b):
    return jnp.dot(a, b, preferred_element_type=F32)


def _dot_nt(a, b):
    return lax.dot_general(a, b, (((1,), (1,)), ((), ())), preferred_element_type=F32)


def _split_scale(scale):
    pre = 2.0 ** math.floor(math.log2(scale))
    return pre, scale / pre


def _pick(n, *cands):
    for c in cands:
        if n % c == 0:
            return c
    raise ValueError(f"no tile for {n} among {cands}")


def _conv_rows(u, prev, cw, shift):
    rows = u.shape[0]
    full = jnp.concatenate([prev, u], axis=0)
    y = cw[0:1, :] * full[0:rows] + cw[1:2, :] * full[shift:shift + rows] + cw[2:3, :] * u
    return y, full[rows:rows + CONV_PREV * shift]


def _conv_block(u, halo, cw):
    full = jnp.concatenate([halo, u], axis=0)
    y = cw[0:1, :] * pltpu.roll(full, 2, 0) + cw[1:2, :] * pltpu.roll(full, 1, 0) + cw[2:3, :] * full
    return y[CONV_HALO:]


def _gate_conv_core(x, wb, wc, wh, cw, prev, shift):
    bg = _dot(x, wb)
    c = _dot(x, wc)
    h = _dot(x, wh)
    y, tail = _conv_rows(c * h, prev, cw, shift)
    return (bg * y).astype(BF16), tail


def _gate_conv_p_kernel(x_ref, wb_ref, wc_ref, wh_ref, cw_ref, z_ref, st_ref, *, rb):
    wb, wc, wh, cw = wb_ref[...], wc_ref[...], wh_ref[...], cw_ref[...]
    halo = jnp.zeros((CONV_HALO, wb.shape[1]), F32)
    for r in range(x_ref.shape[0] // rb):
        x = x_ref[r * rb:(r + 1) * rb, :]
        u = _dot(x, wc) * _dot(x, wh)
        z_ref[r * rb:(r + 1) * rb, :] = (_dot(x, wb) * _conv_block(u, halo, cw)).astype(BF16)
        halo = u[rb - CONV_HALO:rb]
    st_ref[0] = halo[CONV_HALO - CONV_PREV:CONV_HALO]


def _gate_conv_s_kernel(x_ref, wb_ref, wc_ref, wh_ref, cw_ref, prev_ref,
                        z_ref, st_ref, wb16_ref, wc16_ref, wh16_ref, *, shift):
    wb = wb_ref[...].astype(BF16)
    wc = wc_ref[...].astype(BF16)
    wh = wh_ref[...].astype(BF16)
    wb16_ref[...] = wb
    wc16_ref[...] = wc
    wh16_ref[...] = wh
    z, tail = _gate_conv_core(x_ref[...], wb, wc, wh, cw_ref[...], prev_ref[...], shift)
    z_ref[...] = z
    st_ref[...] = tail


def gate_conv_sample(x16, w_in, layer, cw, prev, shift):
    rows, d = x16.shape
    c = w_in.shape[2] // 3
    tn = _pick(c, 256, 128)
    nc = c // tn
    w_spec = lambda sec: pl.BlockSpec((None, d, tn), lambda n, sec=sec: (layer, 0, sec * nc + n))
    col = lambda r: pl.BlockSpec((r, tn), lambda n: (0, n))
    return pl.pallas_call(
        functools.partial(_gate_conv_s_kernel, shift=shift),
        grid=(nc,),
        in_specs=[pl.BlockSpec((rows, d), lambda n: (0, 0)), w_spec(0), w_spec(1), w_spec(2),
                  col(CONV_TAPS), col(CONV_PREV * shift)],
        out_specs=[col(rows), col(CONV_PREV * shift), col(d), col(d), col(d)],
        out_shape=[jax.ShapeDtypeStruct((rows, c), BF16),
                   jax.ShapeDtypeStruct((CONV_PREV * shift, c), F32),
                   jax.ShapeDtypeStruct((d, c), BF16),
                   jax.ShapeDtypeStruct((d, c), BF16),
                   jax.ShapeDtypeStruct((d, c), BF16)],
        compiler_params=_params(("arbitrary",)),
        name="gate_conv_sample",
    )(x16, w_in, w_in, w_in, cw, prev)


def gate_conv_prompt(x16, wb16, wc16, wh16, cw, batch):
    m, d = x16.shape
    s = m // batch
    c = wb16.shape[1]
    tn = _pick(c, 256, 128)
    w_spec = pl.BlockSpec((d, tn), lambda b, n: (0, n))
    return pl.pallas_call(
        functools.partial(_gate_conv_p_kernel, rb=_pick(s, 512, 256, 128)),
        grid=(batch, c // tn),
        in_specs=[pl.BlockSpec((s, d), lambda b, n: (b, 0), pipeline_mode=pl.Buffered(1)),
                  w_spec, w_spec, w_spec,
                  pl.BlockSpec((CONV_TAPS, tn), lambda b, n: (0, n))],
        out_specs=[pl.BlockSpec((s, tn), lambda b, n: (b, n)),
                   pl.BlockSpec((1, CONV_PREV, tn), lambda b, n: (b, 0, n))],
        out_shape=[jax.ShapeDtypeStruct((m, c), BF16),
                   jax.ShapeDtypeStruct((batch, CONV_PREV, c), F32)],
        compiler_params=_params(("arbitrary", "arbitrary")),
        name="gate_conv_prompt",
    )(x16, wb16, wc16, wh16, cw)


def _ffn_up_core(x, wg, wu, cwg, cwu, prev_g, prev_u, shift):
    cg, tail_g = _conv_rows(_dot(x, wg), prev_g, cwg, shift)
    cu, tail_u = _conv_rows(_dot(x, wu), prev_u, cwu, shift)
    return _silu_gate(cg, cu), tail_g, tail_u


def _silu_gate(cg, cu):
    return ((cg * (1.0 / (1.0 + jnp.exp(-cg)))) * cu).astype(BF16)


def _ffn_up_p_kernel(*refs, rb, has_next):
    if has_next:
        (x_ref, wg_ref, wu_ref, cwg_ref, cwu_ref, dn_ref, up_ref,
         a_ref, sg_ref, su_ref, dn16_ref, up16_ref) = refs
        up16_ref[...] = up_ref[...].astype(BF16)
    else:
        x_ref, wg_ref, wu_ref, cwg_ref, cwu_ref, dn_ref, a_ref, sg_ref, su_ref, dn16_ref = refs
    dn16_ref[...] = dn_ref[...].astype(BF16)
    wg, wu, cwg, cwu = wg_ref[...], wu_ref[...], cwg_ref[...], cwu_ref[...]
    halo_g = jnp.zeros((CONV_HALO, wg.shape[1]), F32)
    halo_u = halo_g
    for r in range(x_ref.shape[0] // rb):
        x = x_ref[r * rb:(r + 1) * rb, :]
        hg = _dot(x, wg)
        hu = _dot(x, wu)
        a_ref[r * rb:(r + 1) * rb, :] = _silu_gate(_conv_block(hg, halo_g, cwg),
                                                   _conv_block(hu, halo_u, cwu))
        halo_g = hg[rb - CONV_HALO:rb]
        halo_u = hu[rb - CONV_HALO:rb]
    sg_ref[0] = halo_g[CONV_HALO - CONV_PREV:CONV_HALO]
    su_ref[0] = halo_u[CONV_HALO - CONV_PREV:CONV_HALO]


def _ffn_up_s_kernel(x_ref, wg_ref, wu_ref, cwg_ref, cwu_ref, pg_ref, pu_ref,
                     a_ref, sg_ref, su_ref, *w16_refs, shift):
    wg = wg_ref[...].astype(BF16)
    wu = wu_ref[...].astype(BF16)
    if w16_refs:
        w16_refs[0][...] = wg
        w16_refs[1][...] = wu
    act, tg, tu = _ffn_up_core(x_ref[...], wg, wu, cwg_ref[...], cwu_ref[...],
                               pg_ref[...], pu_ref[...], shift)
    a_ref[...] = act
    sg_ref[...] = tg
    su_ref[...] = tu


def ffn_up_sample(x16, w_up, layer, cw, prev, shift):
    rows, d = x16.shape
    emit = layer is not None
    f = w_up.shape[-1] // 2
    tn = _pick(f, 256, 128)
    nf = f // tn
    sec = lambda r, k: pl.BlockSpec((r, tn), lambda n, k=k: (0, k * nf + n))
    if emit:
        wsec = lambda k: pl.BlockSpec((None, d, tn), lambda n, k=k: (layer, 0, k * nf + n))
    else:
        wsec = lambda k: sec(d, k)
    col = lambda r: pl.BlockSpec((r, tn), lambda n: (0, n))
    pr = CONV_PREV * shift
    w16_shape = jax.ShapeDtypeStruct((d, f), BF16)
    return pl.pallas_call(
        functools.partial(_ffn_up_s_kernel, shift=shift),
        grid=(nf,),
        in_specs=[pl.BlockSpec((rows, d), lambda n: (0, 0)), wsec(0), wsec(1),
                  sec(CONV_TAPS, 0), sec(CONV_TAPS, 1), sec(pr, 0), sec(pr, 1)],
        out_specs=[col(rows), col(pr), col(pr)] + ([col(d), col(d)] if emit else []),
        out_shape=[jax.ShapeDtypeStruct((rows, f), BF16),
                   jax.ShapeDtypeStruct((pr, f), F32),
                   jax.ShapeDtypeStruct((pr, f), F32)] + ([w16_shape, w16_shape] if emit else []),
        compiler_params=_params(("arbitrary",)),
        name="ffn_up_sample",
    )(x16, w_up, w_up, cw, cw, prev, prev)


def ffn_up_prompt(x16, wg16, wu16, u_off, cw, batch, w_down, w_up, layer):
    m, d = x16.shape
    s = m // batch
    f = w_down.shape[1]
    tn = _pick(f, 256, 128)
    nf = f // tn
    steps = batch * nf
    has_next = layer + 1 < w_up.shape[0]
    dn_rows, up_cols = f // steps, 2 * f // steps
    assert dn_rows * steps == f and dn_rows % 16 == 0 and up_cols % LANE == 0
    step = lambda b, n: b * nf + n
    st_spec = pl.BlockSpec((1, CONV_PREV, tn), lambda b, n: (b, 0, n))
    in_specs = [pl.BlockSpec((s, d), lambda b, n: (b, 0), pipeline_mode=pl.Buffered(1)),
                pl.BlockSpec((d, tn), lambda b, n: (0, n)),
                pl.BlockSpec((d, tn), lambda b, n: (0, u_off + n)),
                pl.BlockSpec((CONV_TAPS, tn), lambda b, n: (0, n)),
                pl.BlockSpec((CONV_TAPS, tn), lambda b, n: (0, nf + n)),
                pl.BlockSpec((None, dn_rows, d), lambda b, n: (layer, step(b, n), 0))]
    out_specs = [pl.BlockSpec((s, tn), lambda b, n: (b, n)), st_spec, st_spec,
                 pl.BlockSpec((dn_rows, d), lambda b, n: (step(b, n), 0))]
    out_shape = [jax.ShapeDtypeStruct((m, f), BF16),
                 jax.ShapeDtypeStruct((batch, CONV_PREV, f), F32),
                 jax.ShapeDtypeStruct((batch, CONV_PREV, f), F32),
                 jax.ShapeDtypeStruct((f, d), BF16)]
    args = [x16, wg16, wu16, cw, cw, w_down]
    if has_next:
        in_specs.append(pl.BlockSpec((None, d, up_cols), lambda b, n: (layer + 1, 0, step(b, n))))
        out_specs.append(pl.BlockSpec((d, up_cols), lambda b, n: (0, step(b, n))))
        out_shape.append(jax.ShapeDtypeStruct((d, 2 * f), BF16))
        args.append(w_up)
    return pl.pallas_call(
        functools.partial(_ffn_up_p_kernel, rb=_pick(s, 512, 256, 128), has_next=has_next),
        grid=(batch, nf),
        in_specs=in_specs,
        out_specs=out_specs,
        out_shape=out_shape,
        compiler_params=_params(("arbitrary", "arbitrary")),
        name="ffn_up_prompt",
    )(*args)


def _mm_p_kernel(*refs, alpha, has_res):
    if has_res:
        x_ref, w_ref, r_ref, o_ref = refs
    else:
        x_ref, w_ref, o_ref = refs
    acc = _dot(x_ref[...], w_ref[...])
    if has_res:
        acc = alpha * r_ref[...] + acc
    o_ref[...] = acc.astype(o_ref.dtype)


def _mm_s_kernel(*refs, alpha, has_res):
    if has_res:
        x_ref, w_ref, r_ref, o_ref, w16_ref = refs
    else:
        x_ref, w_ref, o_ref, w16_ref = refs
    w = w_ref[...].astype(BF16)
    w16_ref[...] = w
    acc = _dot(x_ref[...], w)
    if has_res:
        acc = alpha * r_ref[...] + acc
    o_ref[...] = acc.astype(o_ref.dtype)


def mm_sample(x16, w, layer, n_out, out_dtype, res=None, alpha=None):
    rows, k = x16.shape
    tn = _pick(n_out, 512, 256, 128) if k <= 4096 else _pick(n_out, 256, 128)
    col = lambda r: pl.BlockSpec((r, tn), lambda n: (0, n))
    in_specs = [pl.BlockSpec((rows, k), lambda n: (0, 0)),
                pl.BlockSpec((None, k, tn), lambda n: (layer, 0, n))]
    args = [x16, w]
    if res is not None:
        in_specs.append(col(rows))
        args.append(res)
    return pl.pallas_call(
        functools.partial(_mm_s_kernel, alpha=alpha, has_res=res is not None),
        grid=(n_out // tn,),
        in_specs=in_specs,
        out_specs=[col(rows), col(k)],
        out_shape=[jax.ShapeDtypeStruct((rows, n_out), out_dtype),
                   jax.ShapeDtypeStruct((k, n_out), BF16)],
        compiler_params=_params(("arbitrary",)),
        name="mm_sample",
    )(*args)


def mm_prompt(x16, w16, n_out, col_off, out_dtype, res=None, alpha=None):
    m, k = x16.shape
    if k <= 4096:
        tm, tn = _pick(m, 1024, 512, 256, 128), _pick(n_out, 1024, 512, 256, 128)
    else:
        tm, tn = _pick(m, 512, 256, 128), _pick(n_out, 512, 256, 128)
    assert col_off % tn == 0
    off = col_off // tn
    in_specs = [pl.BlockSpec((tm, k), lambda i, n: (i, 0)),
                pl.BlockSpec((k, tn), lambda i, n: (0, off + n))]
    args = [x16, w16]
    if res is not None:
        in_specs.append(pl.BlockSpec((tm, tn), lambda i, n: (i, n)))
        args.append(res)
    return pl.pallas_call(
        functools.partial(_mm_p_kernel, alpha=alpha, has_res=res is not None),
        grid=(m // tm, n_out // tn),
        in_specs=in_specs,
        out_specs=pl.BlockSpec((tm, tn), lambda i, n: (i, n)),
        out_shape=jax.ShapeDtypeStruct((m, n_out), out_dtype),
        compiler_params=_params(("arbitrary", "arbitrary")),
        name="mm_prompt",
    )(*args)


def _ln_kernel(y_ref, g_ref, b_ref, xf_ref, xb_ref):
    y = y_ref[...]
    mu = jnp.mean(y, axis=-1, keepdims=True)
    dev = y - mu
    var = jnp.mean(dev * dev, axis=-1, keepdims=True)
    out = dev * lax.rsqrt(var + LN_EPS) * g_ref[...] + b_ref[...]
    xf_ref[...] = out
    xb_ref[...] = out.astype(BF16)


def layer_norm_rows(y, g, b):
    m, d = y.shape
    tr = _pick(m, 256, 128)
    row = pl.BlockSpec((tr, d), lambda i: (i, 0))
    vec = pl.BlockSpec((1, d), lambda i: (0, 0))
    return pl.pallas_call(
        _ln_kernel,
        grid=(m // tr,),
        in_specs=[row, vec, vec],
        out_specs=[row, row],
        out_shape=[jax.ShapeDtypeStruct((m, d), F32), jax.ShapeDtypeStruct((m, d), BF16)],
        compiler_params=_params(("arbitrary",)),
        name="layer_norm",
    )(y, g.reshape(1, d), b.reshape(1, d))


def _sink_softmax_pv(s, sink, v16):
    m = jnp.maximum(jnp.max(s, axis=-1, keepdims=True), sink)
    p = jnp.exp(s - m)
    den = jnp.sum(p, axis=-1, keepdims=True) + jnp.exp(sink - m)
    return _dot((p / den).astype(BF16), v16)


def _swa_p_kernel(q_ref, kp_ref, ko_ref, vp_ref, vo_ref, sink_ref, o_ref, *, grp, pre, rest):
    blk = q_ref.shape[0]
    rows = grp * blk
    first = pl.program_id(1) == 0
    kcat = jnp.concatenate([kp_ref[...], ko_ref[...]], axis=0).astype(BF16)
    vcat = jnp.concatenate([vp_ref[...], vo_ref[...]], axis=0).astype(BF16)
    tl = lax.broadcasted_iota(jnp.int32, (rows, 2 * blk), 0) % blk
    j = lax.broadcasted_iota(jnp.int32, (rows, 2 * blk), 1)
    mask = (j >= tl) & (j <= tl + WINDOW) & ((j >= blk) | jnp.logical_not(first))
    hd = SW_HEAD_DIM
    unit = (lax.broadcasted_iota(jnp.int32, (2 * blk, hd), 1) == 0).astype(BF16)
    for h in range(SW_KV_HEADS):
        heads = range(h * grp, (h + 1) * grp)
        q = jnp.concatenate([q_ref[:, a * hd:(a + 1) * hd] for a in heads], axis=0) * pre
        sink = jnp.concatenate([jnp.full((blk, 1), sink_ref[a], F32) for a in heads], axis=0)
        s = _dot_nt(q, kcat[:, h * hd:(h + 1) * hd])
        if rest != 1.0:
            s = s * rest
        s = jnp.where(mask, s, NEG_INF)
        m = jnp.maximum(jnp.max(s, axis=-1, keepdims=True), sink)
        vh = jnp.concatenate([vcat[:, h * hd:(h + 1) * hd], unit], axis=1)
        ov = _dot(jnp.exp(s - m).astype(BF16), vh)
        den = ov[:, hd:hd + 1] + jnp.exp(sink - m)
        o = (ov[:, :hd] / den).astype(BF16)
        for g, a in enumerate(heads):
            o_ref[:, a * hd:(a + 1) * hd] = o[g * blk:(g + 1) * blk]


def swa_prompt(q16, k, v, sinks, batch):
    m, qw = q16.shape
    s = m // batch
    blk = WINDOW
    nb = s // blk
    kw = SW_KV_HEADS * SW_HEAD_DIM
    grp = qw // kw
    own = pl.BlockSpec((blk, kw), lambda b, i: (b * nb + i, 0))
    prev = pl.BlockSpec((blk, kw), lambda b, i: (b * nb + jnp.maximum(i - 1, 0), 0))
    pre, rest = _split_scale(SW_HEAD_DIM ** -0.5)
    return pl.pallas_call(
        functools.partial(_swa_p_kernel, grp=grp, pre=pre, rest=rest),
        grid=(batch, nb),
        in_specs=[pl.BlockSpec((blk, qw), lambda b, i: (b * nb + i, 0)),
                  prev, own, prev, own,
                  pl.BlockSpec(memory_space=pltpu.SMEM)],
        out_specs=pl.BlockSpec((blk, qw), lambda b, i: (b * nb + i, 0)),
        out_shape=jax.ShapeDtypeStruct((m, qw), BF16),
        compiler_params=_params(("arbitrary", "arbitrary")),
        name="swa_prompt",
    )(q16, k, k, v, v, sinks)


def _swa_s_kernel(q_ref, ck_ref, cv_ref, kn_ref, vn_ref, sink_ref, o_ref, *, steps, scale):
    wb, kvh, hd = ck_ref.shape
    pad = jnp.zeros((wb - kn_ref.shape[1], hd), F32)
    rows = q_ref.shape[2]
    t = lax.broadcasted_iota(jnp.int32, (rows, 2 * wb), 0) % steps
    j = lax.broadcasted_iota(jnp.int32, (rows, 2 * wb), 1)
    mask = (j >= t + wb - WINDOW) & (j <= t + wb)

    def keys(cache_ref, new_ref, h):
        cached = cache_ref.reshape(wb * kvh, hd)[pl.ds(h, wb, stride=kvh), :]
        return jnp.concatenate([cached, new_ref[0, :, h * hd:(h + 1) * hd], pad], axis=0).astype(BF16)

    for h in range(kvh):
        s = _dot_nt(q_ref[0, h], keys(ck_ref, kn_ref, h)) * scale
        s = jnp.where(mask, s, NEG_INF)
        o_ref[0, h] = _sink_softmax_pv(s, sink_ref[h], keys(cv_ref, vn_ref, h))


def swa_sample(q16, cache_k, cache_v, layer, k_new, v_new, sink_col, steps):
    bd, kvh, rows, hd = q16.shape
    wb = cache_k.shape[2]
    per_b = lambda a: pl.BlockSpec((1,) + a.shape[1:], lambda b: (b,) + (0,) * (a.ndim - 1))
    cache_spec = pl.BlockSpec((None, None, wb, kvh, hd), lambda b: (layer, b, 0, 0, 0))
    return pl.pallas_call(
        functools.partial(_swa_s_kernel, steps=steps, scale=SW_HEAD_DIM ** -0.5),
        grid=(bd,),
        in_specs=[per_b(q16), cache_spec, cache_spec, per_b(k_new), per_b(v_new),
                  pl.BlockSpec(sink_col.shape, lambda b: (0, 0, 0))],
        out_specs=pl.BlockSpec((1, kvh, rows, hd), lambda b: (b, 0, 0, 0)),
        out_shape=jax.ShapeDtypeStruct((bd, kvh, rows, hd), F32),
        compiler_params=_params(("arbitrary",)),
        name="swa_sample",
    )(q16, cache_k, cache_v, k_new, v_new, sink_col)


def _log_sigmoid(z):
    return jnp.minimum(z, 0.0) - jnp.log1p(jnp.exp(-jnp.abs(z)))


def _fox_logf_p_kernel(x_ref, wf_ref, bf_ref, lf_ref, c_ref, ct_ref, *, blk):
    lf = _log_sigmoid(_dot(x_ref[...], wf_ref[...].astype(BF16)) + bf_ref[...])
    lf_ref[...] = lf
    r = lax.broadcasted_iota(jnp.int32, (blk, blk), 0)
    c = lax.broadcasted_iota(jnp.int32, (blk, blk), 1)
    upto_rows = (c <= r).astype(F32)
    upto_cols = (r <= c).astype(F32)
    carry = jnp.zeros((1, lf.shape[1]), F32)
    carry_t = jnp.zeros((lf.shape[1], 1), F32)
    for i in range(lf.shape[0] // blk):
        part = lf[i * blk:(i + 1) * blk]
        cb = jnp.dot(upto_rows, part, precision=lax.Precision.HIGHEST,
                     preferred_element_type=F32) + carry
        c_ref[i * blk:(i + 1) * blk, :] = cb
        carry = cb[blk - 1:blk, :]
        cbt = lax.dot_general(part, upto_cols, (((0,), (0,)), ((), ())),
                              precision=lax.Precision.HIGHEST, preferred_element_type=F32) + carry_t
        ct_ref[0, :, i * blk:(i + 1) * blk] = cbt
        carry_t = cbt[:, blk - 1:blk]


def fox_logf_prompt(x16, wf, bf, batch):
    m, d = x16.shape
    s = m // batch
    h = wf.shape[1]
    row = pl.BlockSpec((s, h), lambda b: (b, 0))
    return pl.pallas_call(
        functools.partial(_fox_logf_p_kernel, blk=_pick(s, 256, 128)),
        grid=(batch,),
        in_specs=[pl.BlockSpec((s, d), lambda b: (b, 0)),
                  pl.BlockSpec((d, h), lambda b: (0, 0)),
                  pl.BlockSpec((1, h), lambda b: (0, 0))],
        out_specs=[row, row, pl.BlockSpec((1, h, s), lambda b: (b, 0, 0))],
        out_shape=[jax.ShapeDtypeStruct((m, h), F32), jax.ShapeDtypeStruct((m, h), F32),
                   jax.ShapeDtypeStruct((batch, h, s), F32)],
        compiler_params=_params(("arbitrary",)),
        name="fox_logf_prompt",
    )(x16, wf, bf.reshape(1, h))


def _fox_logf_s_kernel(x_ref, wf_ref, bf_ref, lf_ref, c_ref, *, shift):
    lf = _log_sigmoid(_dot(x_ref[...], wf_ref[...].astype(BF16)) + bf_ref[...])
    lf_ref[...] = lf
    run = lf[0:shift]
    c_ref[0:shift, :] = run
    for t in range(1, lf.shape[0] // shift):
        run = run + lf[t * shift:(t + 1) * shift]
        c_ref[t * shift:(t + 1) * shift, :] = run


def fox_logf_sample(x16, wf, bf, shift):
    rows, d = x16.shape
    h = wf.shape[1]
    full = lambda a, b: pl.BlockSpec((a, b), lambda i: (0, 0))
    return pl.pallas_call(
        functools.partial(_fox_logf_s_kernel, shift=shift),
        grid=(1,),
        in_specs=[full(rows, d), full(d, h), full(1, h)],
        out_specs=[full(rows, h), full(rows, h)],
        out_shape=[jax.ShapeDtypeStruct((rows, h), F32), jax.ShapeDtypeStruct((rows, h), F32)],
        compiler_params=_params(("arbitrary",)),
        name="fox_logf_sample",
    )(x16, wf, bf.reshape(1, h))


def _fox_p_kernel(q_ref, k_ref, v_ref, c_ref, ck_ref, o_ref, k16_ref, v16_ref, cq_ref, m_ref, acc_ref,
                  *, grp, pre, rest):
    h = pl.program_id(1)
    qi = pl.program_id(2)
    tq = q_ref.shape[0]
    tk = tq
    hd = FOX_HEAD_DIM
    rows = grp * tq
    c_exp = rest * LOG2_E

    @pl.when(qi == 0)
    def _():
        unit = (lax.broadcasted_iota(jnp.int32, v_ref.shape, 1) == 0).astype(BF16)
        k16_ref[...] = k_ref[...].astype(BF16)
        v16_ref[...] = jnp.concatenate([v_ref[...].astype(BF16), unit], axis=1)

    q = jnp.concatenate([q_ref[:, g * hd:(g + 1) * hd] for g in range(grp)], axis=0) * pre
    c_blk = c_ref[...]
    lane = lax.broadcasted_iota(jnp.int32, c_blk.shape, 1)
    for g in range(grp):
        col = jnp.sum(jnp.where(lane == h * grp + g, c_blk, 0.0), axis=-1, keepdims=True)
        cq_ref[g * tq:(g + 1) * tq, :] = jnp.broadcast_to(col * (1.0 / rest), (tq, hd))
    m_ref[...] = jnp.full(m_ref.shape, NEG_INF, F32)
    acc_ref[...] = jnp.zeros(acc_ref.shape, F32)

    def update(j, diagonal):
        off = pl.multiple_of(j * tk, tk)
        z = _dot_nt(q, k16_ref[pl.ds(off, tk), :])
        cq = cq_ref[...]
        t = jnp.concatenate([z[:, i * hd:(i + 1) * hd] + cq for i in range(tk // hd)], axis=1)
        ck = ck_ref[0, 0, j] * (1.0 / rest)
        t = (t.reshape(grp, tq, tk) - ck[:, None, :]).reshape(rows, tk)
        if diagonal:
            r = lax.broadcasted_iota(jnp.int32, (rows, tk), 0) % tq
            c = lax.broadcasted_iota(jnp.int32, (rows, tk), 1)
            t = jnp.where(c <= r, t, NEG_INF)
        m_old = m_ref[...]
        m_new = jnp.maximum(m_old, jnp.max(t, axis=-1, keepdims=True))
        a = jnp.exp2((m_old - m_new) * c_exp)
        p = jnp.exp2((t - m_new) * c_exp)
        acc_ref[...] = a * acc_ref[...] + _dot(p.astype(BF16), v16_ref[pl.ds(off, tk), :])
        m_ref[...] = m_new

    def below_diagonal(j, carry):
        update(j, False)
        return carry

    lax.fori_loop(0, qi, below_diagonal, 0)
    update(qi, True)
    acc = acc_ref[...]
    out = acc[:, :hd] / acc[:, hd:hd + 1]
    for g in range(grp):
        o_ref[:, g * hd:(g + 1) * hd] = out[g * tq:(g + 1) * tq].astype(BF16)


def fox_prompt(q16, k, v, c, c_k, batch):
    m, qw = q16.shape
    heads = c.shape[1]
    s = m // batch
    hd, kvh = FOX_HEAD_DIM, FOX_KV_HEADS
    grp = qw // (kvh * hd)
    nk, tq = c_k.shape[2], c_k.shape[4]
    nq = s // tq
    pre, rest = _split_scale(hd ** -0.5)
    kv_spec = pl.BlockSpec((s, hd), lambda b, h, qi: (b, h))
    return pl.pallas_call(
        functools.partial(_fox_p_kernel, grp=grp, pre=pre, rest=rest),
        grid=(batch, kvh, nq),
        in_specs=[pl.BlockSpec((tq, grp * hd), lambda b, h, qi: (b * nq + qi, h)),
                  kv_spec, kv_spec,
                  pl.BlockSpec((tq, heads), lambda b, h, qi: (b * nq + qi, 0)),
                  pl.BlockSpec((1, 1, nk, grp, tq), lambda b, h, qi: (b, h, 0, 0, 0))],
        out_specs=pl.BlockSpec((tq, grp * hd), lambda b, h, qi: (b * nq + qi, h)),
        out_shape=jax.ShapeDtypeStruct((m, qw), BF16),
        scratch_shapes=[pltpu.VMEM((s, hd), BF16), pltpu.VMEM((s, 2 * hd), BF16),
                        pltpu.VMEM((grp * tq, hd), F32), pltpu.VMEM((grp * tq, 1), F32),
                        pltpu.VMEM((grp * tq, 2 * hd), F32)],
        compiler_params=_params(("arbitrary", "arbitrary", "arbitrary")),
        name="fox_prompt",
    )(q16, k, v, c, c_k)


def _fox_s_kernel(pt_ref, q_ref, ccol_ref, cnt_ref, kn_ref, vn_ref, *rest, pages, steps, grp, scale):
    k_refs = rest[0:pages]
    v_refs = rest[pages:2 * pages]
    lf_refs = rest[2 * pages:3 * pages]
    o_ref, qbd_ref, m_ref, l_ref, acc_ref, carry_ref = rest[3 * pages:]
    p = pl.program_id(1)
    rows, hd = q_ref.shape[1], q_ref.shape[2]
    heads = rows // steps
    kvh = heads // grp
    psz = cnt_ref.shape[2]
    row = lax.broadcasted_iota(jnp.int32, (rows, hd), 0)
    row_kvh = (row % heads) // grp

    def page_update(k16, v16, neg_ck, mask):
        s = _dot_nt(qbd_ref[...], k16) * scale
        s = s + (ccol_ref[0] + jnp.concatenate([neg_ck] * steps, axis=0))
        if mask is not None:
            s = jnp.where(mask, s, NEG_INF)
        m_old = m_ref[...]
        m_new = jnp.maximum(m_old, jnp.max(s, axis=-1, keepdims=True))
        a = jnp.exp(m_old - m_new)
        pr = jnp.exp(s - m_new)
        l_ref[...] = a * l_ref[...] + jnp.sum(pr, axis=-1, keepdims=True)
        acc_ref[...] = a * acc_ref[...] + _dot(pr.astype(BF16), v16)
        m_ref[...] = m_new

    @pl.when(p == 0)
    def _():
        q = q_ref[0]
        for h in range(kvh):
            qbd_ref[:, h * hd:(h + 1) * hd] = jnp.where(row_kvh == h, q, jnp.zeros_like(q))
        m_ref[...] = jnp.full(m_ref.shape, NEG_INF, F32)
        l_ref[...] = jnp.zeros(l_ref.shape, F32)
        acc_ref[...] = jnp.zeros(acc_ref.shape, F32)
        carry_ref[...] = jnp.zeros(carry_ref.shape, F32)
        pad = jnp.zeros((psz - kn_ref.shape[1], kn_ref.shape[2]), F32)
        k16 = jnp.concatenate([kn_ref[0], pad], axis=0).astype(BF16)
        v16 = jnp.concatenate([vn_ref[0], pad], axis=0).astype(BF16)
        t = lax.broadcasted_iota(jnp.int32, (rows, psz), 0) // heads
        j = lax.broadcasted_iota(jnp.int32, (rows, psz), 1)
        page_update(k16, v16, -cnt_ref[0], j <= t)

    def flat16(ref):
        rows2d = ref.reshape(psz * kvh, hd)
        return jnp.concatenate([rows2d[pl.ds(h, psz, stride=kvh), :] for h in range(kvh)],
                               axis=1).astype(BF16)

    @pl.when(p > 0)
    def _():
        r = lax.broadcasted_iota(jnp.int32, (psz, 2 * psz), 0)
        c = lax.broadcasted_iota(jnp.int32, (psz, 2 * psz), 1)
        later_and_total = ((r > c) | (c >= psz)).astype(F32)
        carry = carry_ref[...]
        parts = []
        for i in range(pages):
            w = lax.dot_general(lf_refs[i][...], later_and_total, (((0,), (0,)), ((), ())),
                                precision=lax.Precision.HIGHEST, preferred_element_type=F32)
            parts.append(w[:, :psz] + carry)
            carry = carry + w[:, psz:psz + 1]
        carry_ref[...] = carry
        k16 = jnp.concatenate([flat16(k_refs[i]) for i in range(pages)], axis=0)
        v16 = jnp.concatenate([flat16(v_refs[i]) for i in range(pages)], axis=0)
        page_update(k16, v16, jnp.concatenate(parts, axis=1), None)

    @pl.when(p == pl.num_programs(1) - 1)
    def _():
        out = jnp.zeros((rows, hd), F32)
        for h in range(kvh):
            out = out + jnp.where(row_kvh == h, acc_ref[:, h * hd:(h + 1) * hd], 0.0)
        o_ref[0] = (out / l_ref[...]).astype(BF16)


def fox_sample(page_table, q16, c_col, c_new_t, k_new, v_new, pool_k, pool_v, pool_lf, layer, steps):
    bd, rows, hd = q16.shape
    heads = rows // steps
    psz, kvh = pool_k.shape[2], pool_k.shape[3]
    kw = kvh * hd
    n_pages = page_table.shape[1]
    pages = _pick(n_pages, 8, 4, 2, 1)
    n_steps = n_pages // pages + 1

    def page_idx(i, tail):
        return lambda b, p, pt: (layer, pt[b, n_pages - 1 - (jnp.maximum(p - 1, 0) * pages + i)]) + tail

    per_b = lambda a: pl.BlockSpec((1,) + a.shape[1:], lambda b, p, pt: (b,) + (0,) * (a.ndim - 1))
    kv_specs = lambda: [pl.BlockSpec((None, None, psz, kvh, hd), page_idx(i, (0, 0, 0))) for i in range(pages)]
    lf_specs = [pl.BlockSpec((None, None, psz, heads), page_idx(i, (0, 0))) for i in range(pages)]
    grid_spec = pltpu.PrefetchScalarGridSpec(
        num_scalar_prefetch=1,
        grid=(bd, n_steps),
        in_specs=[per_b(q16), per_b(c_col), per_b(c_new_t), per_b(k_new), per_b(v_new)]
        + kv_specs() + kv_specs() + lf_specs,
        out_specs=pl.BlockSpec((1, rows, hd), lambda b, p, pt: (b, 0, 0)),
        scratch_shapes=[pltpu.VMEM((rows, kw), BF16), pltpu.VMEM((rows, 1), F32),
                        pltpu.VMEM((rows, 1), F32), pltpu.VMEM((rows, kw), F32),
                        pltpu.VMEM((heads, 1), F32)],
    )
    return pl.pallas_call(
        functools.partial(_fox_s_kernel, pages=pages, steps=steps, grp=heads // FOX_KV_HEADS,
                          scale=FOX_HEAD_DIM ** -0.5),
        grid_spec=grid_spec,
        out_shape=jax.ShapeDtypeStruct((bd, rows, hd), BF16),
        compiler_params=_params(("arbitrary", "arbitrary")),
        name="fox_sample",
    )(page_table, q16, c_col, c_new_t, k_new, v_new,
      *([pool_k] * pages), *([pool_v] * pages), *([pool_lf] * pages))


def _time_major(a):
    return jnp.swapaxes(a, 0, 1).reshape((a.shape[0] * a.shape[1],) + a.shape[2:])


def _batch_major(a, bd):
    return jnp.swapaxes(a.reshape((a.shape[0] // bd, bd) + a.shape[1:]), 0, 1)


def _pad_rows(a, rows):
    return jnp.pad(a, ((0, 0), (0, rows - a.shape[1]), (0, 0)))


def kernel(x_prompt, x_sample, state_conv, cache_win_k, cache_win_v, cache_k, cache_v, cache_logf,
           state_ffn, page_table, w_in_a, conv_a, w_out_a, w_qkv_b, sinks_b, w_o_b, w_qkvf_c, b_f_c,
           w_o_c, ln1_g, ln1_b, w_up, conv_f, w_down, ln2_g, ln2_b):
    batch, seq, d = x_prompt.shape
    bd, steps, _ = x_sample.shape
    depth = ln1_g.shape[0]
    alpha = (2 * depth) ** 0.25
    n_mixers = 3
    f = w_down.shape[1]

    xp = x_prompt.reshape(batch * seq, d)
    xp16 = xp.astype(BF16)
    xs = _time_major(x_sample)
    xs16 = xs.astype(BF16)

    up16 = None
    conv_p, conv_s, wk_p, wv_p, wk_s, wv_s = [], [], [], [], [], []
    fk_p, fv_p, fl_p, fk_s, fv_s, fl_s = [], [], [], [], [], []
    ffn_p, ffn_s = [], []

    for i in range(depth):
        mix, j = i % n_mixers, i // n_mixers
        if mix == 0:
            zs, st_s, wb16, wc16, wh16 = gate_conv_sample(xs16, w_in_a, j, conv_a[j],
                                                          _time_major(state_conv[j]), bd)
            ys, wo16 = mm_sample(zs, w_out_a, j, d, F32, res=xs, alpha=alpha)
            zp, st_p = gate_conv_prompt(xp16, wb16, wc16, wh16, conv_a[j], batch)
            yp = mm_prompt(zp, wo16, d, 0, F32, res=xp, alpha=alpha)
            conv_p.append(st_p)
            conv_s.append(_batch_major(st_s, bd))
        elif mix == 1:
            hd, kvh = SW_HEAD_DIM, SW_KV_HEADS
            kw = kvh * hd
            grp = d // kw
            qkv_s, w16 = mm_sample(xs16, w_qkv_b, j, d + 2 * kw, F32)
            qp = mm_prompt(xp16, w16, d, 0, BF16)
            kp = mm_prompt(xp16, w16, kw, d, F32)
            vp = mm_prompt(xp16, w16, kw, d + kw, F32)
            op = swa_prompt(qp, kp, vp, sinks_b[j], batch)
            qs = qkv_s[:, :d].astype(BF16).reshape(steps, bd, kvh, grp, hd)
            qs = qs.transpose(1, 2, 3, 0, 4).reshape(bd, kvh, grp * steps, hd)
            kn = _batch_major(qkv_s[:, d:d + kw], bd)
            vn = _batch_major(qkv_s[:, d + kw:], bd)
            sink_col = jnp.repeat(sinks_b[j].reshape(kvh, grp), steps, axis=1)[..., None]
            os_ = swa_sample(qs, cache_win_k, cache_win_v, j, _pad_rows(kn, 8), _pad_rows(vn, 8),
                             sink_col, steps)
            os_ = os_.reshape(bd, kvh, grp, steps, hd).transpose(3, 0, 1, 2, 4).reshape(steps * bd, d)
            ys, wo16 = mm_sample(os_.astype(BF16), w_o_b, j, d, F32, res=xs, alpha=alpha)
            yp = mm_prompt(op, wo16, d, 0, F32, res=xp, alpha=alpha)
            keep = min(WINDOW, seq)
            wk_p.append(kp.reshape(batch, seq, kvh, hd)[:, seq - keep:])
            wv_p.append(vp.reshape(batch, seq, kvh, hd)[:, seq - keep:])
            wb = cache_win_k.shape[2]
            wk_s.append(jnp.concatenate([cache_win_k[j], kn.reshape(bd, steps, kvh, hd)], axis=1)[:, -wb:])
            wv_s.append(jnp.concatenate([cache_win_v[j], vn.reshape(bd, steps, kvh, hd)], axis=1)[:, -wb:])
        else:
            hd, kvh = FOX_HEAD_DIM, FOX_KV_HEADS
            kw = kvh * hd
            heads = d // hd
            grp = heads // kvh
            wf = w_qkvf_c[j, :, d + 2 * kw:]
            qkv_s, w16 = mm_sample(xs16, w_qkvf_c, j, d + 2 * kw, F32)
            lf_s, c_s = fox_logf_sample(xs16, wf, b_f_c[j], bd)
            qp = mm_prompt(xp16, w16, d, 0, BF16)
            kp = mm_prompt(xp16, w16, kw, d, F32)
            vp = mm_prompt(xp16, w16, kw, d + kw, F32)
            lf_p, c_p, ct_p = fox_logf_prompt(xp16, wf, b_f_c[j], batch)
            tk = _pick(seq, 512, 256, 128)
            c_k = ct_p.reshape(batch, kvh, grp, seq // tk, tk).transpose(0, 1, 3, 2, 4)
            op = fox_prompt(qp, kp, vp, c_p, c_k, batch)
            qs = _batch_major(qkv_s[:, :d].astype(BF16), bd).reshape(bd, steps * heads, hd)
            kn = _batch_major(qkv_s[:, d:d + kw], bd)
            vn = _batch_major(qkv_s[:, d + kw:], bd)
            c_b = _batch_major(c_s, bd)
            c_col = c_b.reshape(bd, steps * heads, 1)
            c_new_t = jnp.pad(c_b.transpose(0, 2, 1), ((0, 0), (0, 0), (0, PAGE_SIZE - steps)))
            os_ = fox_sample(page_table, qs, c_col, c_new_t, _pad_rows(kn, 8), _pad_rows(vn, 8),
                             cache_k, cache_v, cache_logf, j, steps)
            os_ = _time_major(os_.reshape(bd, steps, d))
            ys, wo16 = mm_sample(os_, w_o_c, j, d, F32, res=xs, alpha=alpha)
            yp = mm_prompt(op, wo16, d, 0, F32, res=xp, alpha=alpha)
            fk_p.append(kp.reshape(batch, seq, kvh, hd))
            fv_p.append(vp.reshape(batch, seq, kvh, hd))
            fl_p.append(lf_p.reshape(batch, seq, heads))
            fk_s.append(kn.reshape(bd, steps, kvh, hd))
            fv_s.append(vn.reshape(bd, steps, kvh, hd))
            fl_s.append(_batch_major(lf_s, bd))
        xs, xs16 = layer_norm_rows(ys, ln1_g[i], ln1_b[i])
        xp, xp16 = layer_norm_rows(yp, ln1_g[i], ln1_b[i])

        prev_ffn = _time_major(state_ffn[i])
        if up16 is None:
            a_s, sg_s, su_s, wg16, wu16 = ffn_up_sample(xs16, w_up, i, conv_f[i], prev_ffn, bd)
            u_off = 0
        else:
            a_s, sg_s, su_s = ffn_up_sample(xs16, up16, None, conv_f[i], prev_ffn, bd)
            wg16 = wu16 = up16
            u_off = f // _pick(f, 256, 128)
        a_p, sg_p, su_p, wd16, *nxt = ffn_up_prompt(xp16, wg16, wu16, u_off, conv_f[i], batch,
                                                   w_down, w_up, i)
        up16 = nxt[0] if nxt else None
        ys = mm_prompt(a_s, wd16, d, 0, F32, res=xs, alpha=alpha)
        yp = mm_prompt(a_p, wd16, d, 0, F32, res=xp, alpha=alpha)
        ffn_p.append(jnp.concatenate([sg_p, su_p], axis=-1))
        ffn_s.append(_batch_major(jnp.concatenate([sg_s, su_s], axis=-1), bd))
        xs, xs16 = layer_norm_rows(ys, ln2_g[i], ln2_b[i])
        xp, xp16 = layer_norm_rows(yp, ln2_g[i], ln2_b[i])

    return (xp.reshape(batch, seq, d), _batch_major(xs, bd),
            jnp.stack(conv_p), jnp.stack(conv_s), jnp.stack(wk_p), jnp.stack(wv_p),
            jnp.stack(wk_s), jnp.stack(wv_s), jnp.stack(fk_p), jnp.stack(fv_p), jnp.stack(fl_p),
            jnp.stack(fk_s), jnp.stack(fv_s), jnp.stack(fl_s), jnp.stack(ffn_p), jnp.stack(ffn_s))
```

```python
import functools
import math

import jax
import jax.numpy as jnp
from jax import lax
from jax.experimental import pallas as pl
from jax.experimental.pallas import tpu as pltpu

F32 = jnp.float32
BF16 = jnp.bfloat16

CONV_TAPS = 3
CONV_PREV = CONV_TAPS - 1
CONV_HALO = 8
SW_HEAD_DIM = 64
SW_KV_HEADS = 8
WINDOW = 128
FOX_HEAD_DIM = 128
FOX_KV_HEADS = 8
PAGE_SIZE = 128
LN_EPS = 1e-5
NEG_INF = -1e30
LOG2_E = math.log2(math.e)

V7X_VMEM_BYTES = 64 * 1024 * 1024
VMEM_LIMIT = V7X_VMEM_BYTES - 8 * 1024 * 1024
LANE = 128
BF16_SUBLANES = 16


def _params(semantics):
    return pltpu.CompilerParams(dimension_semantics=semantics, vmem_limit_bytes=VMEM_LIMIT)


def _dot(a, b):
    return jnp.dot(a, b, preferred_element_type=F32)


def _dot_nt(a, b):
    return lax.dot_general(a, b, (((1,), (1,)), ((), ())), preferred_element_type=F32)


def _split_scale(scale):
    pre = 2.0 ** math.floor(math.log2(scale))
    return pre, scale / pre


def _pick(n, *cands):
    for c in cands:
        if n % c == 0:
            return c
    raise ValueError(f"no tile for {n} among {cands}")


def _conv_rows(u, prev, cw, shift):
    rows = u.shape[0]
    full = jnp.concatenate([prev, u], axis=0)
    y = cw[0:1, :] * full[0:rows] + cw[1:2, :] * full[shift:shift + rows] + cw[2:3, :] * u
    return y, full[rows:rows + CONV_PREV * shift]


def _conv_block(u, halo, cw):
    full = jnp.concatenate([halo, u], axis=0)
    y = cw[0:1, :] * pltpu.roll(full, 2, 0) + cw[1:2, :] * pltpu.roll(full, 1, 0) + cw[2:3, :] * full
    return y[CONV_HALO:]


def _gate_conv_core(x, wb, wc, wh, cw, prev, shift):
    bg = _dot(x, wb)
    c = _dot(x, wc)
    h = _dot(x, wh)
    y, tail = _conv_rows(c * h, prev, cw, shift)
    return (bg * y).astype(BF16), tail


def _gate_conv_p_kernel(x_ref, wb_ref, wc_ref, wh_ref, cw_ref, z_ref, st_ref, *, rb):
    wb, wc, wh, cw = wb_ref[...], wc_ref[...], wh_ref[...], cw_ref[...]
    halo = jnp.zeros((CONV_HALO, wb.shape[1]), F32)
    for r in range(x_ref.shape[0] // rb):
        x = x_ref[r * rb:(r + 1) * rb, :]
        u = _dot(x, wc) * _dot(x, wh)
        z_ref[r * rb:(r + 1) * rb, :] = (_dot(x, wb) * _conv_block(u, halo, cw)).astype(BF16)
        halo = u[rb - CONV_HALO:rb]
    st_ref[0] = halo[CONV_HALO - CONV_PREV:CONV_HALO]


def _gate_conv_s_kernel(x_ref, wb_ref, wc_ref, wh_ref, cw_ref, prev_ref,
                        z_ref, st_ref, wb16_ref, wc16_ref, wh16_ref, *, shift):
    wb = wb_ref[...].astype(BF16)
    wc = wc_ref[...].astype(BF16)
    wh = wh_ref[...].astype(BF16)
    wb16_ref[...] = wb
    wc16_ref[...] = wc
    wh16_ref[...] = wh
    z, tail = _gate_conv_core(x_ref[...], wb, wc, wh, cw_ref[...], prev_ref[...], shift)
    z_ref[...] = z
    st_ref[...] = tail


def gate_conv_sample(x16, w_in, layer, cw, prev, shift):
    rows, d = x16.shape
    c = w_in.shape[2] // 3
    tn = _pick(c, 256, 128)
    nc = c // tn
    w_spec = lambda sec: pl.BlockSpec((None, d, tn), lambda n, sec=sec: (layer, 0, sec * nc + n))
    col = lambda r: pl.BlockSpec((r, tn), lambda n: (0, n))
    return pl.pallas_call(
        functools.partial(_gate_conv_s_kernel, shift=shift),
        grid=(nc,),
        in_specs=[pl.BlockSpec((rows, d), lambda n: (0, 0)), w_spec(0), w_spec(1), w_spec(2),
                  col(CONV_TAPS), col(CONV_PREV * shift)],
        out_specs=[col(rows), col(CONV_PREV * shift), col(d), col(d), col(d)],
        out_shape=[jax.ShapeDtypeStruct((rows, c), BF16),
                   jax.ShapeDtypeStruct((CONV_PREV * shift, c), F32),
                   jax.ShapeDtypeStruct((d, c), BF16),
                   jax.ShapeDtypeStruct((d, c), BF16),
                   jax.ShapeDtypeStruct((d, c), BF16)],
        compiler_params=_params(("arbitrary",)),
        name="gate_conv_sample",
    )(x16, w_in, w_in, w_in, cw, prev)


def gate_conv_prompt(x16, wb16, wc16, wh16, cw, batch):
    m, d = x16.shape
    s = m // batch
    c = wb16.shape[1]
    tn = _pick(c, 256, 128)
    w_spec = pl.BlockSpec((d, tn), lambda b, n: (0, n))
    return pl.pallas_call(
        functools.partial(_gate_conv_p_kernel, rb=_pick(s, 512, 256, 128)),
        grid=(batch, c // tn),
        in_specs=[pl.BlockSpec((s, d), lambda b, n: (b, 0)),
                  w_spec, w_spec, w_spec,
                  pl.BlockSpec((CONV_TAPS, tn), lambda b, n: (0, n))],
        out_specs=[pl.BlockSpec((s, tn), lambda b, n: (b, n)),
                   pl.BlockSpec((1, CONV_PREV, tn), lambda b, n: (b, 0, n))],
        out_shape=[jax.ShapeDtypeStruct((m, c), BF16),
                   jax.ShapeDtypeStruct((batch, CONV_PREV, c), F32)],
        compiler_params=_params(("arbitrary", "arbitrary")),
        name="gate_conv_prompt",
    )(x16, wb16, wc16, wh16, cw)


def _ffn_up_core(x, wg, wu, cwg, cwu, prev_g, prev_u, shift):
    cg, tail_g = _conv_rows(_dot(x, wg), prev_g, cwg, shift)
    cu, tail_u = _conv_rows(_dot(x, wu), prev_u, cwu, shift)
    return _silu_gate(cg, cu), tail_g, tail_u


def _silu_gate(cg, cu):
    half = 0.5 * cg
    return ((half * jnp.tanh(half) + half) * cu).astype(BF16)


def _ffn_up_p_kernel(*refs, rb, has_next):
    if has_next:
        (x_ref, wg_ref, wu_ref, cwg_ref, cwu_ref, dn_ref, up_ref,
         a_ref, sg_ref, su_ref, dn16_ref, up16_ref) = refs
        up16_ref[...] = up_ref[...].astype(BF16)
    else:
        x_ref, wg_ref, wu_ref, cwg_ref, cwu_ref, dn_ref, a_ref, sg_ref, su_ref, dn16_ref = refs
    dn16_ref[...] = dn_ref[...].astype(BF16)
    wg, wu, cwg, cwu = wg_ref[...], wu_ref[...], cwg_ref[...], cwu_ref[...]
    halo_g = jnp.zeros((CONV_HALO, wg.shape[1]), F32)
    halo_u = halo_g
    for r in range(x_ref.shape[0] // rb):
        x = x_ref[r * rb:(r + 1) * rb, :]
        hg = _dot(x, wg)
        hu = _dot(x, wu)
        a_ref[r * rb:(r + 1) * rb, :] = _silu_gate(_conv_block(hg, halo_g, cwg),
                                                   _conv_block(hu, halo_u, cwu))
        halo_g = hg[rb - CONV_HALO:rb]
        halo_u = hu[rb - CONV_HALO:rb]
    sg_ref[0] = halo_g[CONV_HALO - CONV_PREV:CONV_HALO]
    su_ref[0] = halo_u[CONV_HALO - CONV_PREV:CONV_HALO]


def _ffn_up_s_kernel(x_ref, wg_ref, wu_ref, cwg_ref, cwu_ref, pg_ref, pu_ref,
                     a_ref, sg_ref, su_ref, *w16_refs, shift):
    wg = wg_ref[...].astype(BF16)
    wu = wu_ref[...].astype(BF16)
    if w16_refs:
        w16_refs[0][...] = wg
        w16_refs[1][...] = wu
    act, tg, tu = _ffn_up_core(x_ref[...], wg, wu, cwg_ref[...], cwu_ref[...],
                               pg_ref[...], pu_ref[...], shift)
    a_ref[...] = act
    sg_ref[...] = tg
    su_ref[...] = tu


def ffn_up_sample(x16, w_up, layer, cw, prev, shift):
    rows, d = x16.shape
    emit = layer is not None
    f = w_up.shape[-1] // 2
    tn = _pick(f, 256, 128)
    nf = f // tn
    sec = lambda r, k: pl.BlockSpec((r, tn), lambda n, k=k: (0, k * nf + n))
    if emit:
        wsec = lambda k: pl.BlockSpec((None, d, tn), lambda n, k=k: (layer, 0, k * nf + n))
    else:
        wsec = lambda k: sec(d, k)
    col = lambda r: pl.BlockSpec((r, tn), lambda n: (0, n))
    pr = CONV_PREV * shift
    w16_shape = jax.ShapeDtypeStruct((d, f), BF16)
    return pl.pallas_call(
        functools.partial(_ffn_up_s_kernel, shift=shift),
        grid=(nf,),
        in_specs=[pl.BlockSpec((rows, d), lambda n: (0, 0)), wsec(0), wsec(1),
                  sec(CONV_TAPS, 0), sec(CONV_TAPS, 1), sec(pr, 0), sec(pr, 1)],
        out_specs=[col(rows), col(pr), col(pr)] + ([col(d), col(d)] if emit else []),
        out_shape=[jax.ShapeDtypeStruct((rows, f), BF16),
                   jax.ShapeDtypeStruct((pr, f), F32),
                   jax.ShapeDtypeStruct((pr, f), F32)] + ([w16_shape, w16_shape] if emit else []),
        compiler_params=_params(("arbitrary",)),
        name="ffn_up_sample",
    )(x16, w_up, w_up, cw, cw, prev, prev)


def ffn_up_prompt(x16, wg16, wu16, u_off, cw, batch, w_down, w_up, layer):
    m, d = x16.shape
    s = m // batch
    f = w_down.shape[1]
    tn = _pick(f, 256, 128)
    nf = f // tn
    steps = batch * nf
    has_next = layer + 1 < w_up.shape[0]
    dn_rows, up_cols = f // steps, 2 * f // steps
    assert dn_rows * steps == f and dn_rows % BF16_SUBLANES == 0 and up_cols % LANE == 0
    step = lambda b, n: b * nf + n
    st_spec = pl.BlockSpec((1, CONV_PREV, tn), lambda b, n: (b, 0, n))
    in_specs = [pl.BlockSpec((s, d), lambda b, n: (b, 0)),
                pl.BlockSpec((d, tn), lambda b, n: (0, n)),
                pl.BlockSpec((d, tn), lambda b, n: (0, u_off + n)),
                pl.BlockSpec((CONV_TAPS, tn), lambda b, n: (0, n)),
                pl.BlockSpec((CONV_TAPS, tn), lambda b, n: (0, nf + n)),
                pl.BlockSpec((None, dn_rows, d), lambda b, n: (layer, step(b, n), 0))]
    out_specs = [pl.BlockSpec((s, tn), lambda b, n: (b, n)), st_spec, st_spec,
                 pl.BlockSpec((dn_rows, d), lambda b, n: (step(b, n), 0))]
    out_shape = [jax.ShapeDtypeStruct((m, f), BF16),
                 jax.ShapeDtypeStruct((batch, CONV_PREV, f), F32),
                 jax.ShapeDtypeStruct((batch, CONV_PREV, f), F32),
                 jax.ShapeDtypeStruct((f, d), BF16)]
    args = [x16, wg16, wu16, cw, cw, w_down]
    if has_next:
        in_specs.append(pl.BlockSpec((None, d, up_cols), lambda b, n: (layer + 1, 0, step(b, n))))
        out_specs.append(pl.BlockSpec((d, up_cols), lambda b, n: (0, step(b, n))))
        out_shape.append(jax.ShapeDtypeStruct((d, 2 * f), BF16))
        args.append(w_up)
    return pl.pallas_call(
        functools.partial(_ffn_up_p_kernel, rb=_pick(s, 512, 256, 128), has_next=has_next),
        grid=(batch, nf),
        in_specs=in_specs,
        out_specs=out_specs,
        out_shape=out_shape,
        compiler_params=_params(("arbitrary", "arbitrary")),
        name="ffn_up_prompt",
    )(*args)


def _mm_p_kernel(*refs, alpha, has_res):
    if has_res:
        x_ref, w_ref, r_ref, o_ref = refs
    else:
        x_ref, w_ref, o_ref = refs
    acc = _dot(x_ref[...], w_ref[...])
    if has_res:
        acc = alpha * r_ref[...] + acc
    o_ref[...] = acc.astype(o_ref.dtype)


def _mm_s_kernel(*refs, alpha, has_res):
    if has_res:
        x_ref, w_ref, r_ref, o_ref, w16_ref = refs
    else:
        x_ref, w_ref, o_ref, w16_ref = refs
    w = w_ref[...].astype(BF16)
    w16_ref[...] = w
    acc = _dot(x_ref[...], w)
    if has_res:
        acc = alpha * r_ref[...] + acc
    o_ref[...] = acc.astype(o_ref.dtype)


def mm_sample(x16, w, layer, n_out, out_dtype, res=None, alpha=None):
    rows, k = x16.shape
    tn = _pick(n_out, 512, 256, 128) if k <= 4096 else _pick(n_out, 256, 128)
    col = lambda r: pl.BlockSpec((r, tn), lambda n: (0, n))
    in_specs = [pl.BlockSpec((rows, k), lambda n: (0, 0)),
                pl.BlockSpec((None, k, tn), lambda n: (layer, 0, n))]
    args = [x16, w]
    if res is not None:
        in_specs.append(col(rows))
        args.append(res)
    return pl.pallas_call(
        functools.partial(_mm_s_kernel, alpha=alpha, has_res=res is not None),
        grid=(n_out // tn,),
        in_specs=in_specs,
        out_specs=[col(rows), col(k)],
        out_shape=[jax.ShapeDtypeStruct((rows, n_out), out_dtype),
                   jax.ShapeDtypeStruct((k, n_out), BF16)],
        compiler_params=_params(("arbitrary",)),
        name="mm_sample",
    )(*args)


def mm_prompt(x16, w16, n_out, col_off, out_dtype, res=None, alpha=None):
    m, k = x16.shape
    if k <= 4096:
        tm, tn = _pick(m, 1024, 512, 256, 128), _pick(n_out, 1024, 512, 256, 128)
    else:
        tm, tn = _pick(m, 512, 256, 128), _pick(n_out, 512, 256, 128)
    assert col_off % tn == 0
    off = col_off // tn
    in_specs = [pl.BlockSpec((tm, k), lambda i, n: (i, 0)),
                pl.BlockSpec((k, tn), lambda i, n: (0, off + n))]
    args = [x16, w16]
    if res is not None:
        in_specs.append(pl.BlockSpec((tm, tn), lambda i, n: (i, n)))
        args.append(res)
    return pl.pallas_call(
        functools.partial(_mm_p_kernel, alpha=alpha, has_res=res is not None),
        grid=(m // tm, n_out // tn),
        in_specs=in_specs,
        out_specs=pl.BlockSpec((tm, tn), lambda i, n: (i, n)),
        out_shape=jax.ShapeDtypeStruct((m, n_out), out_dtype),
        compiler_params=_params(("arbitrary", "arbitrary")),
        name="mm_prompt",
    )(*args)


def _ln_kernel(y_ref, g_ref, b_ref, xf_ref, xb_ref):
    y = y_ref[...]
    mu = jnp.mean(y, axis=-1, keepdims=True)
    dev = y - mu
    var = jnp.mean(dev * dev, axis=-1, keepdims=True)
    out = dev * lax.rsqrt(var + LN_EPS) * g_ref[...] + b_ref[...]
    xf_ref[...] = out
    xb_ref[...] = out.astype(BF16)


def layer_norm_rows(y, g, b):
    m, d = y.shape
    tr = _pick(m, 512, 256, 128)
    row = pl.BlockSpec((tr, d), lambda i: (i, 0))
    vec = pl.BlockSpec((1, d), lambda i: (0, 0))
    return pl.pallas_call(
        _ln_kernel,
        grid=(m // tr,),
        in_specs=[row, vec, vec],
        out_specs=[row, row],
        out_shape=[jax.ShapeDtypeStruct((m, d), F32), jax.ShapeDtypeStruct((m, d), BF16)],
        compiler_params=_params(("arbitrary",)),
        name="layer_norm",
    )(y, g.reshape(1, d), b.reshape(1, d))


def _sink_softmax_pv(s, sink, v16):
    m = jnp.maximum(jnp.max(s, axis=-1, keepdims=True), sink)
    p = jnp.exp(s - m)
    den = jnp.sum(p, axis=-1, keepdims=True) + jnp.exp(sink - m)
    return _dot((p / den).astype(BF16), v16)


def _swa_p_kernel(q_ref, kp_ref, ko_ref, vp_ref, vo_ref, sink_ref, o_ref, *, grp, pre, rest):
    blk = q_ref.shape[0]
    rows = grp * blk
    first = pl.program_id(1) == 0
    kcat = jnp.concatenate([kp_ref[...], ko_ref[...]], axis=0).astype(BF16)
    vcat = jnp.concatenate([vp_ref[...], vo_ref[...]], axis=0).astype(BF16)
    tl = lax.broadcasted_iota(jnp.int32, (rows, 2 * blk), 0) % blk
    j = lax.broadcasted_iota(jnp.int32, (rows, 2 * blk), 1)
    mask = (j >= tl) & (j <= tl + WINDOW) & ((j >= blk) | jnp.logical_not(first))
    hd = SW_HEAD_DIM
    unit = (lax.broadcasted_iota(jnp.int32, (2 * blk, hd), 1) == 0).astype(BF16)
    for h in range(SW_KV_HEADS):
        heads = range(h * grp, (h + 1) * grp)
        q = jnp.concatenate([q_ref[:, a * hd:(a + 1) * hd] for a in heads], axis=0) * pre
        sink = jnp.concatenate([jnp.full((blk, 1), sink_ref[a], F32) for a in heads], axis=0)
        s = _dot_nt(q, kcat[:, h * hd:(h + 1) * hd])
        if rest != 1.0:
            s = s * rest
        s = jnp.where(mask, s, NEG_INF)
        m = jnp.maximum(jnp.max(s, axis=-1, keepdims=True), sink)
        vh = jnp.concatenate([vcat[:, h * hd:(h + 1) * hd], unit], axis=1)
        ov = _dot(jnp.exp(s - m).astype(BF16), vh)
        den = ov[:, hd:hd + 1] + jnp.exp(sink - m)
        o = (ov[:, :hd] / den).astype(BF16)
        for g, a in enumerate(heads):
            o_ref[:, a * hd:(a + 1) * hd] = o[g * blk:(g + 1) * blk]


def swa_prompt(q16, k, v, sinks, batch):
    m, qw = q16.shape
    s = m // batch
    blk = WINDOW
    nb = s // blk
    kw = SW_KV_HEADS * SW_HEAD_DIM
    grp = qw // kw
    own = pl.BlockSpec((blk, kw), lambda b, i: (b * nb + i, 0))
    prev = pl.BlockSpec((blk, kw), lambda b, i: (b * nb + jnp.maximum(i - 1, 0), 0))
    pre, rest = _split_scale(SW_HEAD_DIM ** -0.5)
    return pl.pallas_call(
        functools.partial(_swa_p_kernel, grp=grp, pre=pre, rest=rest),
        grid=(batch, nb),
        in_specs=[pl.BlockSpec((blk, qw), lambda b, i: (b * nb + i, 0)),
                  prev, own, prev, own,
                  pl.BlockSpec(memory_space=pltpu.SMEM)],
        out_specs=pl.BlockSpec((blk, qw), lambda b, i: (b * nb + i, 0)),
        out_shape=jax.ShapeDtypeStruct((m, qw), BF16),
        compiler_params=_params(("arbitrary", "arbitrary")),
        name="swa_prompt",
    )(q16, k, k, v, v, sinks)


def _swa_s_kernel(q_ref, ck_ref, cv_ref, kn_ref, vn_ref, sink_ref, o_ref, *, steps, scale):
    wb, kvh, hd = ck_ref.shape
    pad = jnp.zeros((wb - kn_ref.shape[1], hd), F32)
    rows = q_ref.shape[2]
    t = lax.broadcasted_iota(jnp.int32, (rows, 2 * wb), 0) % steps
    j = lax.broadcasted_iota(jnp.int32, (rows, 2 * wb), 1)
    mask = (j >= t + wb - WINDOW) & (j <= t + wb)

    def keys(cache_ref, new_ref, h):
        cached = cache_ref.reshape(wb * kvh, hd)[pl.ds(h, wb, stride=kvh), :]
        return jnp.concatenate([cached, new_ref[0, :, h * hd:(h + 1) * hd], pad], axis=0).astype(BF16)

    for h in range(kvh):
        s = _dot_nt(q_ref[0, h], keys(ck_ref, kn_ref, h)) * scale
        s = jnp.where(mask, s, NEG_INF)
        o_ref[0, h] = _sink_softmax_pv(s, sink_ref[h], keys(cv_ref, vn_ref, h))


def swa_sample(q16, cache_k, cache_v, layer, k_new, v_new, sink_col, steps):
    bd, kvh, rows, hd = q16.shape
    wb = cache_k.shape[2]
    per_b = lambda a: pl.BlockSpec((1,) + a.shape[1:], lambda b: (b,) + (0,) * (a.ndim - 1))
    cache_spec = pl.BlockSpec((None, None, wb, kvh, hd), lambda b: (layer, b, 0, 0, 0))
    return pl.pallas_call(
        functools.partial(_swa_s_kernel, steps=steps, scale=SW_HEAD_DIM ** -0.5),
        grid=(bd,),
        in_specs=[per_b(q16), cache_spec, cache_spec, per_b(k_new), per_b(v_new),
                  pl.BlockSpec(sink_col.shape, lambda b: (0, 0, 0))],
        out_specs=pl.BlockSpec((1, kvh, rows, hd), lambda b: (b, 0, 0, 0)),
        out_shape=jax.ShapeDtypeStruct((bd, kvh, rows, hd), F32),
        compiler_params=_params(("arbitrary",)),
        name="swa_sample",
    )(q16, cache_k, cache_v, k_new, v_new, sink_col)


def _log_sigmoid(z):
    return jnp.minimum(z, 0.0) - jnp.log1p(jnp.exp(-jnp.abs(z)))


def _fox_logf_p_kernel(x_ref, wf_ref, bf_ref, lf_ref, c_ref, ct_ref, *, blk):
    lf = _log_sigmoid(_dot(x_ref[...], wf_ref[...].astype(BF16)) + bf_ref[...])
    lf_ref[...] = lf
    r = lax.broadcasted_iota(jnp.int32, (blk, blk), 0)
    c = lax.broadcasted_iota(jnp.int32, (blk, blk), 1)
    upto_rows = (c <= r).astype(F32)
    upto_cols = (r <= c).astype(F32)
    carry = jnp.zeros((1, lf.shape[1]), F32)
    carry_t = jnp.zeros((lf.shape[1], 1), F32)
    for i in range(lf.shape[0] // blk):
        part = lf[i * blk:(i + 1) * blk]
        cb = jnp.dot(upto_rows, part, precision=lax.Precision.HIGHEST,
                     preferred_element_type=F32) + carry
        c_ref[i * blk:(i + 1) * blk, :] = cb
        carry = cb[blk - 1:blk, :]
        cbt = lax.dot_general(part, upto_cols, (((0,), (0,)), ((), ())),
                              precision=lax.Precision.HIGHEST, preferred_element_type=F32) + carry_t
        ct_ref[0, :, i * blk:(i + 1) * blk] = cbt
        carry_t = cbt[:, blk - 1:blk]


def fox_logf_prompt(x16, wf, bf, batch):
    m, d = x16.shape
    s = m // batch
    h = wf.shape[1]
    row = pl.BlockSpec((s, h), lambda b: (b, 0))
    return pl.pallas_call(
        functools.partial(_fox_logf_p_kernel, blk=_pick(s, 256, 128)),
        grid=(batch,),
        in_specs=[pl.BlockSpec((s, d), lambda b: (b, 0)),
                  pl.BlockSpec((d, h), lambda b: (0, 0)),
                  pl.BlockSpec((1, h), lambda b: (0, 0))],
        out_specs=[row, row, pl.BlockSpec((1, h, s), lambda b: (b, 0, 0))],
        out_shape=[jax.ShapeDtypeStruct((m, h), F32), jax.ShapeDtypeStruct((m, h), F32),
                   jax.ShapeDtypeStruct((batch, h, s), F32)],
        compiler_params=_params(("arbitrary",)),
        name="fox_logf_prompt",
    )(x16, wf, bf.reshape(1, h))


def _fox_logf_s_kernel(x_ref, wf_ref, bf_ref, lf_ref, c_ref, *, shift):
    lf = _log_sigmoid(_dot(x_ref[...], wf_ref[...].astype(BF16)) + bf_ref[...])
    lf_ref[...] = lf
    run = lf[0:shift]
    c_ref[0:shift, :] = run
    for t in range(1, lf.shape[0] // shift):
        run = run + lf[t * shift:(t + 1) * shift]
        c_ref[t * shift:(t + 1) * shift, :] = run


def fox_logf_sample(x16, wf, bf, shift):
    rows, d = x16.shape
    h = wf.shape[1]
    full = lambda a, b: pl.BlockSpec((a, b), lambda i: (0, 0))
    return pl.pallas_call(
        functools.partial(_fox_logf_s_kernel, shift=shift),
        grid=(1,),
        in_specs=[full(rows, d), full(d, h), full(1, h)],
        out_specs=[full(rows, h), full(rows, h)],
        out_shape=[jax.ShapeDtypeStruct((rows, h), F32), jax.ShapeDtypeStruct((rows, h), F32)],
        compiler_params=_params(("arbitrary",)),
        name="fox_logf_sample",
    )(x16, wf, bf.reshape(1, h))


def _fox_p_kernel(q_ref, k_ref, v_ref, c_ref, ck_ref, o_ref, k16_ref, v16_ref, cq_ref, m_ref, acc_ref,
                  *, grp, pre, rest):
    h = pl.program_id(1)
    qi = pl.program_id(2)
    tq = q_ref.shape[0]
    tk = tq
    hd = FOX_HEAD_DIM
    rows = grp * tq
    c_exp = rest * LOG2_E

    @pl.when(qi == 0)
    def _():
        unit = (lax.broadcasted_iota(jnp.int32, v_ref.shape, 1) == 0).astype(BF16)
        k16_ref[...] = k_ref[...].astype(BF16)
        v16_ref[...] = jnp.concatenate([v_ref[...].astype(BF16), unit], axis=1)

    q = jnp.concatenate([q_ref[:, g * hd:(g + 1) * hd] for g in range(grp)], axis=0) * pre
    c_blk = c_ref[...]
    lane = lax.broadcasted_iota(jnp.int32, c_blk.shape, 1)
    for g in range(grp):
        col = jnp.sum(jnp.where(lane == h * grp + g, c_blk, 0.0), axis=-1, keepdims=True)
        cq_ref[g * tq:(g + 1) * tq, :] = jnp.broadcast_to(col * (1.0 / rest), (tq, hd))
    m_ref[...] = jnp.full(m_ref.shape, NEG_INF, F32)
    acc_ref[...] = jnp.zeros(acc_ref.shape, F32)

    def update(j, diagonal):
        off = pl.multiple_of(j * tk, tk)
        z = _dot_nt(q, k16_ref[pl.ds(off, tk), :])
        cq = cq_ref[...]
        t = jnp.concatenate([z[:, i * hd:(i + 1) * hd] + cq for i in range(tk // hd)], axis=1)
        ck = ck_ref[0, 0, j] * (1.0 / rest)
        t = (t.reshape(grp, tq, tk) - ck[:, None, :]).reshape(rows, tk)
        if diagonal:
            r = lax.broadcasted_iota(jnp.int32, (rows, tk), 0) % tq
            c = lax.broadcasted_iota(jnp.int32, (rows, tk), 1)
            t = jnp.where(c <= r, t, NEG_INF)
        m_old = m_ref[...]
        m_new = jnp.maximum(m_old, jnp.max(t, axis=-1, keepdims=True))
        a = jnp.exp2((m_old - m_new) * c_exp)
        p = jnp.exp2((t - m_new) * c_exp)
        acc_ref[...] = a * acc_ref[...] + _dot(p.astype(BF16), v16_ref[pl.ds(off, tk), :])
        m_ref[...] = m_new

    def below_diagonal(j, carry):
        update(j, False)
        return carry

    lax.fori_loop(0, qi, below_diagonal, 0)
    update(qi, True)
    acc = acc_ref[...]
    out = acc[:, :hd] / acc[:, hd:hd + 1]
    for g in range(grp):
        o_ref[:, g * hd:(g + 1) * hd] = out[g * tq:(g + 1) * tq].astype(BF16)


def fox_prompt(q16, k, v, c, c_k, batch):
    m, qw = q16.shape
    heads = c.shape[1]
    s = m // batch
    hd, kvh = FOX_HEAD_DIM, FOX_KV_HEADS
    grp = qw // (kvh * hd)
    nk, tq = c_k.shape[2], c_k.shape[4]
    nq = s // tq
    pre, rest = _split_scale(hd ** -0.5)
    kv_spec = pl.BlockSpec((s, hd), lambda b, h, qi: (b, h))
    return pl.pallas_call(
        functools.partial(_fox_p_kernel, grp=grp, pre=pre, rest=rest),
        grid=(batch, kvh, nq),
        in_specs=[pl.BlockSpec((tq, grp * hd), lambda b, h, qi: (b * nq + qi, h)),
                  kv_spec, kv_spec,
                  pl.BlockSpec((tq, heads), lambda b, h, qi: (b * nq + qi, 0)),
                  pl.BlockSpec((1, 1, nk, grp, tq), lambda b, h, qi: (b, h, 0, 0, 0))],
        out_specs=pl.BlockSpec((tq, grp * hd), lambda b, h, qi: (b * nq + qi, h)),
        out_shape=jax.ShapeDtypeStruct((m, qw), BF16),
        scratch_shapes=[pltpu.VMEM((s, hd), BF16), pltpu.VMEM((s, 2 * hd), BF16),
                        pltpu.VMEM((grp * tq, hd), F32), pltpu.VMEM((grp * tq, 1), F32),
                        pltpu.VMEM((grp * tq, 2 * hd), F32)],
        compiler_params=_params(("arbitrary", "arbitrary", "arbitrary")),
        name="fox_prompt",
    )(q16, k, v, c, c_k)


def _fox_s_kernel(pt_ref, q_ref, ccol_ref, cnt_ref, kn_ref, vn_ref, *rest, pages, steps, grp, scale):
    k_refs = rest[0:pages]
    v_refs = rest[pages:2 * pages]
    lf_refs = rest[2 * pages:3 * pages]
    o_ref, qbd_ref, m_ref, l_ref, acc_ref, carry_ref = rest[3 * pages:]
    p = pl.program_id(1)
    rows, hd = q_ref.shape[1], q_ref.shape[2]
    heads = rows // steps
    kvh = heads // grp
    psz = cnt_ref.shape[2]
    row = lax.broadcasted_iota(jnp.int32, (rows, hd), 0)
    row_kvh = (row % heads) // grp

    def page_update(k16, v16, neg_ck, mask):
        s = _dot_nt(qbd_ref[...], k16) * scale
        s = s + (ccol_ref[0] + jnp.concatenate([neg_ck] * steps, axis=0))
        if mask is not None:
            s = jnp.where(mask, s, NEG_INF)
        m_old = m_ref[...]
        m_new = jnp.maximum(m_old, jnp.max(s, axis=-1, keepdims=True))
        a = jnp.exp(m_old - m_new)
        pr = jnp.exp(s - m_new)
        l_ref[...] = a * l_ref[...] + jnp.sum(pr, axis=-1, keepdims=True)
        acc_ref[...] = a * acc_ref[...] + _dot(pr.astype(BF16), v16)
        m_ref[...] = m_new

    @pl.when(p == 0)
    def _():
        q = q_ref[0]
        for h in range(kvh):
            qbd_ref[:, h * hd:(h + 1) * hd] = jnp.where(row_kvh == h, q, jnp.zeros_like(q))
        m_ref[...] = jnp.full(m_ref.shape, NEG_INF, F32)
        l_ref[...] = jnp.zeros(l_ref.shape, F32)
        acc_ref[...] = jnp.zeros(acc_ref.shape, F32)
        carry_ref[...] = jnp.zeros(carry_ref.shape, F32)
        pad = jnp.zeros((psz - kn_ref.shape[1], kn_ref.shape[2]), F32)
        k16 = jnp.concatenate([kn_ref[0], pad], axis=0).astype(BF16)
        v16 = jnp.concatenate([vn_ref[0], pad], axis=0).astype(BF16)
        t = lax.broadcasted_iota(jnp.int32, (rows, psz), 0) // heads
        j = lax.broadcasted_iota(jnp.int32, (rows, psz), 1)
        page_update(k16, v16, -cnt_ref[0], j <= t)

    def flat16(ref):
        rows2d = ref.reshape(psz * kvh, hd)
        return jnp.concatenate([rows2d[pl.ds(h, psz, stride=kvh), :] for h in range(kvh)],
                               axis=1).astype(BF16)

    @pl.when(p > 0)
    def _():
        r = lax.broadcasted_iota(jnp.int32, (psz, 2 * psz), 0)
        c = lax.broadcasted_iota(jnp.int32, (psz, 2 * psz), 1)
        later_and_total = ((r > c) | (c >= psz)).astype(F32)
        carry = carry_ref[...]
        parts = []
        for i in range(pages):
            w = lax.dot_general(lf_refs[i][...], later_and_total, (((0,), (0,)), ((), ())),
                                precision=lax.Precision.HIGHEST, preferred_element_type=F32)
            parts.append(w[:, :psz] + carry)
            carry = carry + w[:, psz:psz + 1]
        carry_ref[...] = carry
        k16 = jnp.concatenate([flat16(k_refs[i]) for i in range(pages)], axis=0)
        v16 = jnp.concatenate([flat16(v_refs[i]) for i in range(pages)], axis=0)
        page_update(k16, v16, jnp.concatenate(parts, axis=1), None)

    @pl.when(p == pl.num_programs(1) - 1)
    def _():
        out = jnp.zeros((rows, hd), F32)
        for h in range(kvh):
            out = out + jnp.where(row_kvh == h, acc_ref[:, h * hd:(h + 1) * hd], 0.0)
        o_ref[0] = (out / l_ref[...]).astype(BF16)


def fox_sample(page_table, q16, c_col, c_new_t, k_new, v_new, pool_k, pool_v, pool_lf, layer, steps):
    bd, rows, hd = q16.shape
    heads = rows // steps
    psz, kvh = pool_k.shape[2], pool_k.shape[3]
    kw = kvh * hd
    n_pages = page_table.shape[1]
    pages = _pick(n_pages, 8, 4, 2, 1)
    n_steps = n_pages // pages + 1

    def page_idx(i, tail):
        return lambda b, p, pt: (layer, pt[b, n_pages - 1 - (jnp.maximum(p - 1, 0) * pages + i)]) + tail

    per_b = lambda a: pl.BlockSpec((1,) + a.shape[1:], lambda b, p, pt: (b,) + (0,) * (a.ndim - 1))
    kv_specs = lambda: [pl.BlockSpec((None, None, psz, kvh, hd), page_idx(i, (0, 0, 0))) for i in range(pages)]
    lf_specs = [pl.BlockSpec((None, None, psz, heads), page_idx(i, (0, 0))) for i in range(pages)]
    grid_spec = pltpu.PrefetchScalarGridSpec(
        num_scalar_prefetch=1,
        grid=(bd, n_steps),
        in_specs=[per_b(q16), per_b(c_col), per_b(c_new_t), per_b(k_new), per_b(v_new)]
        + kv_specs() + kv_specs() + lf_specs,
        out_specs=pl.BlockSpec((1, rows, hd), lambda b, p, pt: (b, 0, 0)),
        scratch_shapes=[pltpu.VMEM((rows, kw), BF16), pltpu.VMEM((rows, 1), F32),
                        pltpu.VMEM((rows, 1), F32), pltpu.VMEM((rows, kw), F32),
                        pltpu.VMEM((heads, 1), F32)],
    )
    return pl.pallas_call(
        functools.partial(_fox_s_kernel, pages=pages, steps=steps, grp=heads // FOX_KV_HEADS,
                          scale=FOX_HEAD_DIM ** -0.5),
        grid_spec=grid_spec,
        out_shape=jax.ShapeDtypeStruct((bd, rows, hd), BF16),
        compiler_params=_params(("arbitrary", "arbitrary")),
        name="fox_sample",
    )(page_table, q16, c_col, c_new_t, k_new, v_new,
      *([pool_k] * pages), *([pool_v] * pages), *([pool_lf] * pages))


def _time_major(a):
    return jnp.swapaxes(a, 0, 1).reshape((a.shape[0] * a.shape[1],) + a.shape[2:])


def _batch_major(a, bd):
    return jnp.swapaxes(a.reshape((a.shape[0] // bd, bd) + a.shape[1:]), 0, 1)


def _pad_rows(a, rows):
    return jnp.pad(a, ((0, 0), (0, rows - a.shape[1]), (0, 0)))


def kernel(x_prompt, x_sample, state_conv, cache_win_k, cache_win_v, cache_k, cache_v, cache_logf,
           state_ffn, page_table, w_in_a, conv_a, w_out_a, w_qkv_b, sinks_b, w_o_b, w_qkvf_c, b_f_c,
           w_o_c, ln1_g, ln1_b, w_up, conv_f, w_down, ln2_g, ln2_b):
    batch, seq, d = x_prompt.shape
    bd, steps, _ = x_sample.shape
    depth = ln1_g.shape[0]
    alpha = (2 * depth) ** 0.25
    n_mixers = 3
    f = w_down.shape[1]

    xp = x_prompt.reshape(batch * seq, d)
    xp16 = xp.astype(BF16)
    xs = _time_major(x_sample)
    xs16 = xs.astype(BF16)

    up16 = None
    conv_p, conv_s, wk_p, wv_p, wk_s, wv_s = [], [], [], [], [], []
    fk_p, fv_p, fl_p, fk_s, fv_s, fl_s = [], [], [], [], [], []
    ffn_p, ffn_s = [], []

    for i in range(depth):
        mix, j = i % n_mixers, i // n_mixers
        if mix == 0:
            zs, st_s, wb16, wc16, wh16 = gate_conv_sample(xs16, w_in_a, j, conv_a[j],
                                                          _time_major(state_conv[j]), bd)
            ys, wo16 = mm_sample(zs, w_out_a, j, d, F32, res=xs, alpha=alpha)
            zp, st_p = gate_conv_prompt(xp16, wb16, wc16, wh16, conv_a[j], batch)
            yp = mm_prompt(zp, wo16, d, 0, F32, res=xp, alpha=alpha)
            conv_p.append(st_p)
            conv_s.append(_batch_major(st_s, bd))
        elif mix == 1:
            hd, kvh = SW_HEAD_DIM, SW_KV_HEADS
            kw = kvh * hd
            grp = d // kw
            qkv_s, w16 = mm_sample(xs16, w_qkv_b, j, d + 2 * kw, F32)
            qp = mm_prompt(xp16, w16, d, 0, BF16)
            kp = mm_prompt(xp16, w16, kw, d, F32)
            vp = mm_prompt(xp16, w16, kw, d + kw, F32)
            op = swa_prompt(qp, kp, vp, sinks_b[j], batch)
            qs = qkv_s[:, :d].astype(BF16).reshape(steps, bd, kvh, grp, hd)
            qs = qs.transpose(1, 2, 3, 0, 4).reshape(bd, kvh, grp * steps, hd)
            kn = _batch_major(qkv_s[:, d:d + kw], bd)
            vn = _batch_major(qkv_s[:, d + kw:], bd)
            sink_col = jnp.repeat(sinks_b[j].reshape(kvh, grp), steps, axis=1)[..., None]
            os_ = swa_sample(qs, cache_win_k, cache_win_v, j, _pad_rows(kn, 8), _pad_rows(vn, 8),
                             sink_col, steps)
            os_ = os_.reshape(bd, kvh, grp, steps, hd).transpose(3, 0, 1, 2, 4).reshape(steps * bd, d)
            ys, wo16 = mm_sample(os_.astype(BF16), w_o_b, j, d, F32, res=xs, alpha=alpha)
            yp = mm_prompt(op, wo16, d, 0, F32, res=xp, alpha=alpha)
            keep = min(WINDOW, seq)
            wk_p.append(kp.reshape(batch, seq, kvh, hd)[:, seq - keep:])
            wv_p.append(vp.reshape(batch, seq, kvh, hd)[:, seq - keep:])
            wb = cache_win_k.shape[2]
            wk_s.append(jnp.concatenate([cache_win_k[j], kn.reshape(bd, steps, kvh, hd)], axis=1)[:, -wb:])
            wv_s.append(jnp.concatenate([cache_win_v[j], vn.reshape(bd, steps, kvh, hd)], axis=1)[:, -wb:])
        else:
            hd, kvh = FOX_HEAD_DIM, FOX_KV_HEADS
            kw = kvh * hd
            heads = d // hd
            grp = heads // kvh
            wf = w_qkvf_c[j, :, d + 2 * kw:]
            qkv_s, w16 = mm_sample(xs16, w_qkvf_c, j, d + 2 * kw, F32)
            lf_s, c_s = fox_logf_sample(xs16, wf, b_f_c[j], bd)
            qp = mm_prompt(xp16, w16, d, 0, BF16)
            kp = mm_prompt(xp16, w16, kw, d, F32)
            vp = mm_prompt(xp16, w16, kw, d + kw, F32)
            lf_p, c_p, ct_p = fox_logf_prompt(xp16, wf, b_f_c[j], batch)
            tk = _pick(seq, 512, 256, 128)
            c_k = ct_p.reshape(batch, kvh, grp, seq // tk, tk).transpose(0, 1, 3, 2, 4)
            op = fox_prompt(qp, kp, vp, c_p, c_k, batch)
            qs = _batch_major(qkv_s[:, :d].astype(BF16), bd).reshape(bd, steps * heads, hd)
            kn = _batch_major(qkv_s[:, d:d + kw], bd)
            vn = _batch_major(qkv_s[:, d + kw:], bd)
            c_b = _batch_major(c_s, bd)
            c_col = c_b.reshape(bd, steps * heads, 1)
            c_new_t = jnp.pad(c_b.transpose(0, 2, 1), ((0, 0), (0, 0), (0, PAGE_SIZE - steps)))
            os_ = fox_sample(page_table, qs, c_col, c_new_t, _pad_rows(kn, 8), _pad_rows(vn, 8),
                             cache_k, cache_v, cache_logf, j, steps)
            os_ = _time_major(os_.reshape(bd, steps, d))
            ys, wo16 = mm_sample(os_, w_o_c, j, d, F32, res=xs, alpha=alpha)
            yp = mm_prompt(op, wo16, d, 0, F32, res=xp, alpha=alpha)
            fk_p.append(kp.reshape(batch, seq, kvh, hd))
            fv_p.append(vp.reshape(batch, seq, kvh, hd))
            fl_p.append(lf_p.reshape(batch, seq, heads))
            fk_s.append(kn.reshape(bd, steps, kvh, hd))
            fv_s.append(vn.reshape(bd, steps, kvh, hd))
            fl_s.append(_batch_major(lf_s, bd))
        xs, xs16 = layer_norm_rows(ys, ln1_g[i], ln1_b[i])
        xp, xp16 = layer_norm_rows(yp, ln1_g[i], ln1_b[i])

        prev_ffn = _time_major(state_ffn[i])
        if up16 is None:
            a_s, sg_s, su_s, wg16, wu16 = ffn_up_sample(xs16, w_up, i, conv_f[i], prev_ffn, bd)
            u_off = 0
        else:
            a_s, sg_s, su_s = ffn_up_sample(xs16, up16, None, conv_f[i], prev_ffn, bd)
            wg16 = wu16 = up16
            u_off = f // _pick(f, 256, 128)
        a_p, sg_p, su_p, wd16, *nxt = ffn_up_prompt(xp16, wg16, wu16, u_off, conv_f[i], batch,
                                                   w_down, w_up, i)
        up16 = nxt[0] if nxt else None
        ys = mm_prompt(a_s, wd16, d, 0, F32, res=xs, alpha=alpha)
        yp = mm_prompt(a_p, wd16, d, 0, F32, res=xp, alpha=alpha)
        ffn_p.append(jnp.concatenate([sg_p, su_p], axis=-1))
        ffn_s.append(_batch_major(jnp.concatenate([sg_s, su_s], axis=-1), bd))
        xs, xs16 = layer_norm_rows(ys, ln2_g[i], ln2_b[i])
        xp, xp16 = layer_norm_rows(yp, ln2_g[i], ln2_b[i])

    return (xp.reshape(batch, seq, d), _batch_major(xs, bd),
            jnp.stack(conv_p), jnp.stack(conv_s), jnp.stack(wk_p), jnp.stack(wv_p),
            jnp.stack(wk_s), jnp.stack(wv_s), jnp.stack(fk_p), jnp.stack(fv_p), jnp.stack(fl_p),
            jnp.stack(fk_s), jnp.stack(fv_s), jnp.stack(fl_s), jnp.stack(ffn_p), jnp.stack(ffn_s))
```

```python
import functools
import math

import jax
import jax.numpy as jnp
from jax import lax
from jax.experimental import pallas as pl
from jax.experimental.pallas import tpu as pltpu

F32 = jnp.float32
BF16 = jnp.bfloat16

CONV_TAPS = 3
CONV_PREV = CONV_TAPS - 1
CONV_HALO = 8
SW_HEAD_DIM = 64
SW_KV_HEADS = 8
WINDOW = 128
FOX_HEAD_DIM = 128
FOX_KV_HEADS = 8
PAGE_SIZE = 128
LN_EPS = 1e-5
NEG_INF = -1e30
LOG2_E = math.log2(math.e)

V7X_VMEM_BYTES = 64 * 1024 * 1024
VMEM_LIMIT = V7X_VMEM_BYTES - 8 * 1024 * 1024
LANE = 128
BF16_SUBLANES = 16


def _params(semantics):
    return pltpu.CompilerParams(dimension_semantics=semantics, vmem_limit_bytes=VMEM_LIMIT)


def _dot(a, b):
    return jnp.dot(a, b, preferred_element_type=F32)


def _dot_nt(a, b):
    return lax.dot_general(a, b, (((1,), (1,)), ((), ())), preferred_element_type=F32)


def _split_scale(scale):
    pre = 2.0 ** math.floor(math.log2(scale))
    return pre, scale / pre


def _pick(n, *cands):
    for c in cands:
        if n % c == 0:
            return c
    raise ValueError(f"no tile for {n} among {cands}")


def _conv_rows(u, prev, cw, shift):
    rows = u.shape[0]
    full = jnp.concatenate([prev, u], axis=0)
    y = cw[0:1, :] * full[0:rows] + cw[1:2, :] * full[shift:shift + rows] + cw[2:3, :] * u
    return y, full[rows:rows + CONV_PREV * shift]


def _conv_block(u, halo, cw):
    full = jnp.concatenate([halo, u], axis=0)
    y = cw[0:1, :] * pltpu.roll(full, 2, 0) + cw[1:2, :] * pltpu.roll(full, 1, 0) + cw[2:3, :] * full
    return y[CONV_HALO:]


def _gate_conv_core(x, wb, wc, wh, cw, prev, shift):
    bg = _dot(x, wb)
    c = _dot(x, wc)
    h = _dot(x, wh)
    y, tail = _conv_rows(c * h, prev, cw, shift)
    return (bg * y).astype(BF16), tail


def _gate_conv_p_kernel(x_ref, wb_ref, wc_ref, wh_ref, cw_ref, z_ref, st_ref, *, rb):
    wb, wc, wh, cw = wb_ref[...], wc_ref[...], wh_ref[...], cw_ref[...]
    halo = jnp.zeros((CONV_HALO, wb.shape[1]), F32)
    for r in range(x_ref.shape[0] // rb):
        x = x_ref[r * rb:(r + 1) * rb, :]
        u = _dot(x, wc) * _dot(x, wh)
        z_ref[r * rb:(r + 1) * rb, :] = (_dot(x, wb) * _conv_block(u, halo, cw)).astype(BF16)
        halo = u[rb - CONV_HALO:rb]
    st_ref[0] = halo[CONV_HALO - CONV_PREV:CONV_HALO]


def _gate_conv_s_kernel(x_ref, wb_ref, wc_ref, wh_ref, cw_ref, prev_ref,
                        z_ref, st_ref, wb16_ref, wc16_ref, wh16_ref, *, shift):
    wb = wb_ref[...].astype(BF16)
    wc = wc_ref[...].astype(BF16)
    wh = wh_ref[...].astype(BF16)
    wb16_ref[...] = wb
    wc16_ref[...] = wc
    wh16_ref[...] = wh
    z, tail = _gate_conv_core(x_ref[...], wb, wc, wh, cw_ref[...], prev_ref[...], shift)
    z_ref[...] = z
    st_ref[...] = tail


def gate_conv_sample(x16, w_in, layer, cw, prev, shift):
    rows, d = x16.shape
    c = w_in.shape[2] // 3
    tn = _pick(c, 256, 128)
    nc = c // tn
    w_spec = lambda sec: pl.BlockSpec((None, d, tn), lambda n, sec=sec: (layer, 0, sec * nc + n))
    col = lambda r: pl.BlockSpec((r, tn), lambda n: (0, n))
    return pl.pallas_call(
        functools.partial(_gate_conv_s_kernel, shift=shift),
        grid=(nc,),
        in_specs=[pl.BlockSpec((rows, d), lambda n: (0, 0)), w_spec(0), w_spec(1), w_spec(2),
                  col(CONV_TAPS), col(CONV_PREV * shift)],
        out_specs=[col(rows), col(CONV_PREV * shift), col(d), col(d), col(d)],
        out_shape=[jax.ShapeDtypeStruct((rows, c), BF16),
                   jax.ShapeDtypeStruct((CONV_PREV * shift, c), F32),
                   jax.ShapeDtypeStruct((d, c), BF16),
                   jax.ShapeDtypeStruct((d, c), BF16),
                   jax.ShapeDtypeStruct((d, c), BF16)],
        compiler_params=_params(("arbitrary",)),
        name="gate_conv_sample",
    )(x16, w_in, w_in, w_in, cw, prev)


def gate_conv_prompt(x16, wb16, wc16, wh16, cw, batch):
    m, d = x16.shape
    s = m // batch
    c = wb16.shape[1]
    tn = _pick(c, 256, 128)
    w_spec = pl.BlockSpec((d, tn), lambda b, n: (0, n))
    return pl.pallas_call(
        functools.partial(_gate_conv_p_kernel, rb=_pick(s, 512, 256, 128)),
        grid=(batch, c // tn),
        in_specs=[pl.BlockSpec((s, d), lambda b, n: (b, 0)),
                  w_spec, w_spec, w_spec,
                  pl.BlockSpec((CONV_TAPS, tn), lambda b, n: (0, n))],
        out_specs=[pl.BlockSpec((s, tn), lambda b, n: (b, n)),
                   pl.BlockSpec((1, CONV_PREV, tn), lambda b, n: (b, 0, n))],
        out_shape=[jax.ShapeDtypeStruct((m, c), BF16),
                   jax.ShapeDtypeStruct((batch, CONV_PREV, c), F32)],
        compiler_params=_params(("arbitrary", "arbitrary")),
        name="gate_conv_prompt",
    )(x16, wb16, wc16, wh16, cw)


def _ffn_up_core(x, wg, wu, cwg, cwu, prev_g, prev_u, shift):
    cg, tail_g = _conv_rows(_dot(x, wg), prev_g, cwg, shift)
    cu, tail_u = _conv_rows(_dot(x, wu), prev_u, cwu, shift)
    return _silu_gate(cg, cu), tail_g, tail_u


def _silu_gate(cg, cu):
    half = 0.5 * cg
    return ((half * jnp.tanh(half) + half) * cu).astype(BF16)


def _ffn_up_p_kernel(*refs, rb, has_next):
    if has_next:
        (x_ref, wg_ref, wu_ref, cwg_ref, cwu_ref, dn_ref, up_ref,
         a_ref, sg_ref, su_ref, dn16_ref, up16_ref) = refs
        up16_ref[...] = up_ref[...].astype(BF16)
    else:
        x_ref, wg_ref, wu_ref, cwg_ref, cwu_ref, dn_ref, a_ref, sg_ref, su_ref, dn16_ref = refs
    dn16_ref[...] = dn_ref[...].astype(BF16)
    wg, wu, cwg, cwu = wg_ref[...], wu_ref[...], cwg_ref[...], cwu_ref[...]
    halo_g = jnp.zeros((CONV_HALO, wg.shape[1]), F32)
    halo_u = halo_g
    for r in range(x_ref.shape[0] // rb):
        x = x_ref[r * rb:(r + 1) * rb, :]
        hg = _dot(x, wg)
        hu = _dot(x, wu)
        a_ref[r * rb:(r + 1) * rb, :] = _silu_gate(_conv_block(hg, halo_g, cwg),
                                                   _conv_block(hu, halo_u, cwu))
        halo_g = hg[rb - CONV_HALO:rb]
        halo_u = hu[rb - CONV_HALO:rb]
    sg_ref[0] = halo_g[CONV_HALO - CONV_PREV:CONV_HALO]
    su_ref[0] = halo_u[CONV_HALO - CONV_PREV:CONV_HALO]


def _ffn_up_s_kernel(x_ref, wg_ref, wu_ref, cwg_ref, cwu_ref, pg_ref, pu_ref,
                     a_ref, sg_ref, su_ref, *w16_refs, shift):
    wg = wg_ref[...].astype(BF16)
    wu = wu_ref[...].astype(BF16)
    if w16_refs:
        w16_refs[0][...] = wg
        w16_refs[1][...] = wu
    act, tg, tu = _ffn_up_core(x_ref[...], wg, wu, cwg_ref[...], cwu_ref[...],
                               pg_ref[...], pu_ref[...], shift)
    a_ref[...] = act
    sg_ref[...] = tg
    su_ref[...] = tu


def ffn_up_sample(x16, w_up, layer, cw, prev, shift):
    rows, d = x16.shape
    emit = layer is not None
    f = w_up.shape[-1] // 2
    tn = _pick(f, 256, 128)
    nf = f // tn
    sec = lambda r, k: pl.BlockSpec((r, tn), lambda n, k=k: (0, k * nf + n))
    if emit:
        wsec = lambda k: pl.BlockSpec((None, d, tn), lambda n, k=k: (layer, 0, k * nf + n))
    else:
        wsec = lambda k: sec(d, k)
    col = lambda r: pl.BlockSpec((r, tn), lambda n: (0, n))
    pr = CONV_PREV * shift
    w16_shape = jax.ShapeDtypeStruct((d, f), BF16)
    return pl.pallas_call(
        functools.partial(_ffn_up_s_kernel, shift=shift),
        grid=(nf,),
        in_specs=[pl.BlockSpec((rows, d), lambda n: (0, 0)), wsec(0), wsec(1),
                  sec(CONV_TAPS, 0), sec(CONV_TAPS, 1), sec(pr, 0), sec(pr, 1)],
        out_specs=[col(rows), col(pr), col(pr)] + ([col(d), col(d)] if emit else []),
        out_shape=[jax.ShapeDtypeStruct((rows, f), BF16),
                   jax.ShapeDtypeStruct((pr, f), F32),
                   jax.ShapeDtypeStruct((pr, f), F32)] + ([w16_shape, w16_shape] if emit else []),
        compiler_params=_params(("arbitrary",)),
        name="ffn_up_sample",
    )(x16, w_up, w_up, cw, cw, prev, prev)


def ffn_up_prompt(x16, wg16, wu16, u_off, cw, batch, w_down, w_up, layer):
    m, d = x16.shape
    s = m // batch
    f = w_down.shape[1]
    tn = _pick(f, 256, 128)
    nf = f // tn
    steps = batch * nf
    has_next = layer + 1 < w_up.shape[0]
    dn_rows, up_cols = f // steps, 2 * f // steps
    assert dn_rows * steps == f and dn_rows % BF16_SUBLANES == 0 and up_cols % LANE == 0
    step = lambda b, n: b * nf + n
    st_spec = pl.BlockSpec((1, CONV_PREV, tn), lambda b, n: (b, 0, n))
    in_specs = [pl.BlockSpec((s, d), lambda b, n: (b, 0)),
                pl.BlockSpec((d, tn), lambda b, n: (0, n)),
                pl.BlockSpec((d, tn), lambda b, n: (0, u_off + n)),
                pl.BlockSpec((CONV_TAPS, tn), lambda b, n: (0, n)),
                pl.BlockSpec((CONV_TAPS, tn), lambda b, n: (0, nf + n)),
                pl.BlockSpec((None, dn_rows, d), lambda b, n: (layer, step(b, n), 0))]
    out_specs = [pl.BlockSpec((s, tn), lambda b, n: (b, n)), st_spec, st_spec,
                 pl.BlockSpec((dn_rows, d), lambda b, n: (step(b, n), 0))]
    out_shape = [jax.ShapeDtypeStruct((m, f), BF16),
                 jax.ShapeDtypeStruct((batch, CONV_PREV, f), F32),
                 jax.ShapeDtypeStruct((batch, CONV_PREV, f), F32),
                 jax.ShapeDtypeStruct((f, d), BF16)]
    args = [x16, wg16, wu16, cw, cw, w_down]
    if has_next:
        in_specs.append(pl.BlockSpec((None, d, up_cols), lambda b, n: (layer + 1, 0, step(b, n))))
        out_specs.append(pl.BlockSpec((d, up_cols), lambda b, n: (0, step(b, n))))
        out_shape.append(jax.ShapeDtypeStruct((d, 2 * f), BF16))
        args.append(w_up)
    return pl.pallas_call(
        functools.partial(_ffn_up_p_kernel, rb=_pick(s, 512, 256, 128), has_next=has_next),
        grid=(batch, nf),
        in_specs=in_specs,
        out_specs=out_specs,
        out_shape=out_shape,
        compiler_params=_params(("arbitrary", "arbitrary")),
        name="ffn_up_prompt",
    )(*args)


def _mm_p_kernel(*refs, alpha, has_res):
    if has_res:
        x_ref, w_ref, r_ref, o_ref = refs
    else:
        x_ref, w_ref, o_ref = refs
    acc = _dot(x_ref[...], w_ref[...])
    if has_res:
        acc = alpha * r_ref[...] + acc
    if len(o_ref.shape) == 2:
        o_ref[...] = acc.astype(o_ref.dtype)
    else:
        _, heads, hd = o_ref.shape
        for h in range(heads):
            o_ref[:, h, :] = acc[:, h * hd:(h + 1) * hd].astype(o_ref.dtype)


def _mm_s_kernel(*refs, alpha, has_res):
    if has_res:
        x_ref, w_ref, r_ref, o_ref, w16_ref = refs
    else:
        x_ref, w_ref, o_ref, w16_ref = refs
    w = w_ref[...].astype(BF16)
    w16_ref[...] = w
    acc = _dot(x_ref[...], w)
    if has_res:
        acc = alpha * r_ref[...] + acc
    o_ref[...] = acc.astype(o_ref.dtype)


def mm_sample(x16, w, layer, n_out, out_dtype, res=None, alpha=None):
    rows, k = x16.shape
    tn = _pick(n_out, 512, 256, 128) if k <= 4096 else _pick(n_out, 256, 128)
    col = lambda r: pl.BlockSpec((r, tn), lambda n: (0, n))
    in_specs = [pl.BlockSpec((rows, k), lambda n: (0, 0)),
                pl.BlockSpec((None, k, tn), lambda n: (layer, 0, n))]
    args = [x16, w]
    if res is not None:
        in_specs.append(col(rows))
        args.append(res)
    return pl.pallas_call(
        functools.partial(_mm_s_kernel, alpha=alpha, has_res=res is not None),
        grid=(n_out // tn,),
        in_specs=in_specs,
        out_specs=[col(rows), col(k)],
        out_shape=[jax.ShapeDtypeStruct((rows, n_out), out_dtype),
                   jax.ShapeDtypeStruct((k, n_out), BF16)],
        compiler_params=_params(("arbitrary",)),
        name="mm_sample",
    )(*args)


def mm_prompt(x16, w16, n_out, col_off, out_dtype, res=None, alpha=None, head_dim=None):
    m, k = x16.shape
    if k <= 4096:
        tm, tn = _pick(m, 1024, 512, 256, 128), _pick(n_out, 1024, 512, 256, 128)
    else:
        tm, tn = _pick(m, 512, 256, 128), _pick(n_out, 512, 256, 128)
    assert col_off % tn == 0
    off = col_off // tn
    in_specs = [pl.BlockSpec((tm, k), lambda i, n: (i, 0)),
                pl.BlockSpec((k, tn), lambda i, n: (0, off + n))]
    args = [x16, w16]
    if res is not None:
        in_specs.append(pl.BlockSpec((tm, tn), lambda i, n: (i, n)))
        args.append(res)
    if head_dim is None:
        out_spec = pl.BlockSpec((tm, tn), lambda i, n: (i, n))
        out_shape = jax.ShapeDtypeStruct((m, n_out), out_dtype)
    else:
        assert tn == n_out
        out_spec = pl.BlockSpec((tm, n_out // head_dim, head_dim), lambda i, n: (i, 0, 0))
        out_shape = jax.ShapeDtypeStruct((m, n_out // head_dim, head_dim), out_dtype)
    return pl.pallas_call(
        functools.partial(_mm_p_kernel, alpha=alpha, has_res=res is not None),
        grid=(m // tm, n_out // tn),
        in_specs=in_specs,
        out_specs=out_spec,
        out_shape=out_shape,
        compiler_params=_params(("arbitrary", "arbitrary")),
        name="mm_prompt",
    )(*args)


def _ln_kernel(y_ref, g_ref, b_ref, xf_ref, xb_ref):
    y = y_ref[...]
    mu = jnp.mean(y, axis=-1, keepdims=True)
    dev = y - mu
    var = jnp.mean(dev * dev, axis=-1, keepdims=True)
    out = dev * lax.rsqrt(var + LN_EPS) * g_ref[...] + b_ref[...]
    xf_ref[...] = out
    xb_ref[...] = out.astype(BF16)


def layer_norm_rows(y, g, b):
    m, d = y.shape
    tr = _pick(m, 512, 256, 128)
    row = pl.BlockSpec((tr, d), lambda i: (i, 0))
    vec = pl.BlockSpec((1, d), lambda i: (0, 0))
    return pl.pallas_call(
        _ln_kernel,
        grid=(m // tr,),
        in_specs=[row, vec, vec],
        out_specs=[row, row],
        out_shape=[jax.ShapeDtypeStruct((m, d), F32), jax.ShapeDtypeStruct((m, d), BF16)],
        compiler_params=_params(("arbitrary",)),
        name="layer_norm",
    )(y, g.reshape(1, d), b.reshape(1, d))


def _sink_softmax_pv(s, sink, v16):
    m = jnp.maximum(jnp.max(s, axis=-1, keepdims=True), sink)
    p = jnp.exp(s - m)
    den = jnp.sum(p, axis=-1, keepdims=True) + jnp.exp(sink - m)
    return _dot((p / den).astype(BF16), v16)


def _swa_p_kernel(q_ref, kp_ref, ko_ref, vp_ref, vo_ref, sink_ref, o_ref, *, grp, pre, rest):
    blk = q_ref.shape[0]
    rows = grp * blk
    first = pl.program_id(1) == 0
    kcat = jnp.concatenate([kp_ref[...], ko_ref[...]], axis=0).astype(BF16)
    vcat = jnp.concatenate([vp_ref[...], vo_ref[...]], axis=0).astype(BF16)
    tl = lax.broadcasted_iota(jnp.int32, (rows, 2 * blk), 0) % blk
    j = lax.broadcasted_iota(jnp.int32, (rows, 2 * blk), 1)
    mask = (j >= tl) & (j <= tl + WINDOW) & ((j >= blk) | jnp.logical_not(first))
    hd = SW_HEAD_DIM
    unit = (lax.broadcasted_iota(jnp.int32, (2 * blk, hd), 1) == 0).astype(BF16)
    for h in range(SW_KV_HEADS):
        heads = range(h * grp, (h + 1) * grp)
        q = jnp.concatenate([q_ref[:, a * hd:(a + 1) * hd] for a in heads], axis=0) * pre
        sink = jnp.concatenate([jnp.full((blk, 1), sink_ref[a], F32) for a in heads], axis=0)
        s = _dot_nt(q, kcat[:, h * hd:(h + 1) * hd])
        if rest != 1.0:
            s = s * rest
        s = jnp.where(mask, s, NEG_INF)
        m = jnp.maximum(jnp.max(s, axis=-1, keepdims=True), sink)
        vh = jnp.concatenate([vcat[:, h * hd:(h + 1) * hd], unit], axis=1)
        ov = _dot(jnp.exp(s - m).astype(BF16), vh)
        den = ov[:, hd:hd + 1] + jnp.exp(sink - m)
        o = (ov[:, :hd] / den).astype(BF16)
        for g, a in enumerate(heads):
            o_ref[:, a * hd:(a + 1) * hd] = o[g * blk:(g + 1) * blk]


def swa_prompt(q16, k, v, sinks, batch):
    m, qw = q16.shape
    s = m // batch
    blk = WINDOW
    nb = s // blk
    kw = SW_KV_HEADS * SW_HEAD_DIM
    grp = qw // kw
    own = pl.BlockSpec((blk, kw), lambda b, i: (b * nb + i, 0))
    prev = pl.BlockSpec((blk, kw), lambda b, i: (b * nb + jnp.maximum(i - 1, 0), 0))
    pre, rest = _split_scale(SW_HEAD_DIM ** -0.5)
    return pl.pallas_call(
        functools.partial(_swa_p_kernel, grp=grp, pre=pre, rest=rest),
        grid=(batch, nb),
        in_specs=[pl.BlockSpec((blk, qw), lambda b, i: (b * nb + i, 0)),
                  prev, own, prev, own,
                  pl.BlockSpec(memory_space=pltpu.SMEM)],
        out_specs=pl.BlockSpec((blk, qw), lambda b, i: (b * nb + i, 0)),
        out_shape=jax.ShapeDtypeStruct((m, qw), BF16),
        compiler_params=_params(("arbitrary", "arbitrary")),
        name="swa_prompt",
    )(q16, k, k, v, v, sinks)


def _swa_s_kernel(q_ref, ck_ref, cv_ref, kn_ref, vn_ref, sink_ref, o_ref, *, steps, scale):
    wb, kvh, hd = ck_ref.shape
    pad = jnp.zeros((wb - kn_ref.shape[1], hd), F32)
    rows = q_ref.shape[2]
    t = lax.broadcasted_iota(jnp.int32, (rows, 2 * wb), 0) % steps
    j = lax.broadcasted_iota(jnp.int32, (rows, 2 * wb), 1)
    mask = (j >= t + wb - WINDOW) & (j <= t + wb)

    def keys(cache_ref, new_ref, h):
        cached = cache_ref.reshape(wb * kvh, hd)[pl.ds(h, wb, stride=kvh), :]
        return jnp.concatenate([cached, new_ref[0, :, h * hd:(h + 1) * hd], pad], axis=0).astype(BF16)

    for h in range(kvh):
        s = _dot_nt(q_ref[0, h], keys(ck_ref, kn_ref, h)) * scale
        s = jnp.where(mask, s, NEG_INF)
        o_ref[0, h] = _sink_softmax_pv(s, sink_ref[h], keys(cv_ref, vn_ref, h))


def swa_sample(q16, cache_k, cache_v, layer, k_new, v_new, sink_col, steps):
    bd, kvh, rows, hd = q16.shape
    wb = cache_k.shape[2]
    per_b = lambda a: pl.BlockSpec((1,) + a.shape[1:], lambda b: (b,) + (0,) * (a.ndim - 1))
    cache_spec = pl.BlockSpec((None, None, wb, kvh, hd), lambda b: (layer, b, 0, 0, 0))
    return pl.pallas_call(
        functools.partial(_swa_s_kernel, steps=steps, scale=SW_HEAD_DIM ** -0.5),
        grid=(bd,),
        in_specs=[per_b(q16), cache_spec, cache_spec, per_b(k_new), per_b(v_new),
                  pl.BlockSpec(sink_col.shape, lambda b: (0, 0, 0))],
        out_specs=pl.BlockSpec((1, kvh, rows, hd), lambda b: (b, 0, 0, 0)),
        out_shape=jax.ShapeDtypeStruct((bd, kvh, rows, hd), F32),
        compiler_params=_params(("arbitrary",)),
        name="swa_sample",
    )(q16, cache_k, cache_v, k_new, v_new, sink_col)


def _log_sigmoid(z):
    return jnp.minimum(z, 0.0) - jnp.log1p(jnp.exp(-jnp.abs(z)))


def _fox_logf_p_kernel(x_ref, wf_ref, bf_ref, lf_ref, c_ref, ct_ref, *, blk):
    lf = _log_sigmoid(_dot(x_ref[...], wf_ref[...].astype(BF16)) + bf_ref[...])
    lf_ref[...] = lf
    r = lax.broadcasted_iota(jnp.int32, (blk, blk), 0)
    c = lax.broadcasted_iota(jnp.int32, (blk, blk), 1)
    upto_rows = (c <= r).astype(F32)
    upto_cols = (r <= c).astype(F32)
    carry = jnp.zeros((1, lf.shape[1]), F32)
    carry_t = jnp.zeros((lf.shape[1], 1), F32)
    for i in range(lf.shape[0] // blk):
        part = lf[i * blk:(i + 1) * blk]
        cb = jnp.dot(upto_rows, part, precision=lax.Precision.HIGHEST,
                     preferred_element_type=F32) + carry
        c_ref[i * blk:(i + 1) * blk, :] = cb
        carry = cb[blk - 1:blk, :]
        cbt = lax.dot_general(part, upto_cols, (((0,), (0,)), ((), ())),
                              precision=lax.Precision.HIGHEST, preferred_element_type=F32) + carry_t
        ct_ref[0, :, i * blk:(i + 1) * blk] = cbt
        carry_t = cbt[:, blk - 1:blk]


def fox_logf_prompt(x16, wf, bf, batch):
    m, d = x16.shape
    s = m // batch
    h = wf.shape[1]
    row = pl.BlockSpec((s, h), lambda b: (b, 0))
    return pl.pallas_call(
        functools.partial(_fox_logf_p_kernel, blk=_pick(s, 256, 128)),
        grid=(batch,),
        in_specs=[pl.BlockSpec((s, d), lambda b: (b, 0)),
                  pl.BlockSpec((d, h), lambda b: (0, 0)),
                  pl.BlockSpec((1, h), lambda b: (0, 0))],
        out_specs=[row, row, pl.BlockSpec((1, h, s), lambda b: (b, 0, 0))],
        out_shape=[jax.ShapeDtypeStruct((m, h), F32), jax.ShapeDtypeStruct((m, h), F32),
                   jax.ShapeDtypeStruct((batch, h, s), F32)],
        compiler_params=_params(("arbitrary",)),
        name="fox_logf_prompt",
    )(x16, wf, bf.reshape(1, h))


def _fox_logf_s_kernel(x_ref, wf_ref, bf_ref, lf_ref, c_ref, *, shift):
    lf = _log_sigmoid(_dot(x_ref[...], wf_ref[...].astype(BF16)) + bf_ref[...])
    lf_ref[...] = lf
    run = lf[0:shift]
    c_ref[0:shift, :] = run
    for t in range(1, lf.shape[0] // shift):
        run = run + lf[t * shift:(t + 1) * shift]
        c_ref[t * shift:(t + 1) * shift, :] = run


def fox_logf_sample(x16, wf, bf, shift):
    rows, d = x16.shape
    h = wf.shape[1]
    full = lambda a, b: pl.BlockSpec((a, b), lambda i: (0, 0))
    return pl.pallas_call(
        functools.partial(_fox_logf_s_kernel, shift=shift),
        grid=(1,),
        in_specs=[full(rows, d), full(d, h), full(1, h)],
        out_specs=[full(rows, h), full(rows, h)],
        out_shape=[jax.ShapeDtypeStruct((rows, h), F32), jax.ShapeDtypeStruct((rows, h), F32)],
        compiler_params=_params(("arbitrary",)),
        name="fox_logf_sample",
    )(x16, wf, bf.reshape(1, h))


def _fox_p_kernel(q_ref, k_ref, v_ref, c_ref, ck_ref, o_ref, k16_ref, v16_ref, cq_ref, m_ref, acc_ref,
                  *, grp, pre, rest):
    h = pl.program_id(1)
    qi = pl.program_id(2)
    tq = q_ref.shape[0]
    tk = tq
    hd = FOX_HEAD_DIM
    rows = grp * tq
    c_exp = rest * LOG2_E

    @pl.when(qi == 0)
    def _():
        seq, kvh = k_ref.shape[0], k_ref.shape[1]
        rows_h = pl.ds(h, seq, stride=kvh)
        k16_ref[...] = k_ref.reshape(seq * kvh, hd)[rows_h, :].astype(BF16)
        unit = (lax.broadcasted_iota(jnp.int32, (seq, hd), 1) == 0).astype(BF16)
        v16_ref[...] = jnp.concatenate([v_ref.reshape(seq * kvh, hd)[rows_h, :].astype(BF16), unit], axis=1)

    q = jnp.concatenate([q_ref[:, g * hd:(g + 1) * hd] for g in range(grp)], axis=0) * pre
    c_blk = c_ref[...]
    lane = lax.broadcasted_iota(jnp.int32, c_blk.shape, 1)
    for g in range(grp):
        col = jnp.sum(jnp.where(lane == h * grp + g, c_blk, 0.0), axis=-1, keepdims=True)
        cq_ref[g * tq:(g + 1) * tq, :] = jnp.broadcast_to(col * (1.0 / rest), (tq, hd))
    m_ref[...] = jnp.full(m_ref.shape, NEG_INF, F32)
    acc_ref[...] = jnp.zeros(acc_ref.shape, F32)

    def update(j, diagonal):
        off = pl.multiple_of(j * tk, tk)
        z = _dot_nt(q, k16_ref[pl.ds(off, tk), :])
        cq = cq_ref[...]
        t = jnp.concatenate([z[:, i * hd:(i + 1) * hd] + cq for i in range(tk // hd)], axis=1)
        ck = ck_ref[0, 0, j] * (1.0 / rest)
        t = (t.reshape(grp, tq, tk) - ck[:, None, :]).reshape(rows, tk)
        if diagonal:
            r = lax.broadcasted_iota(jnp.int32, (rows, tk), 0) % tq
            c = lax.broadcasted_iota(jnp.int32, (rows, tk), 1)
            t = jnp.where(c <= r, t, NEG_INF)
        m_old = m_ref[...]
        m_new = jnp.maximum(m_old, jnp.max(t, axis=-1, keepdims=True))
        a = jnp.exp2((m_old - m_new) * c_exp)
        p = jnp.exp2((t - m_new) * c_exp)
        acc_ref[...] = a * acc_ref[...] + _dot(p.astype(BF16), v16_ref[pl.ds(off, tk), :])
        m_ref[...] = m_new

    def below_diagonal(j, carry):
        update(j, False)
        return carry

    lax.fori_loop(0, qi, below_diagonal, 0)
    update(qi, True)
    acc = acc_ref[...]
    out = acc[:, :hd] / acc[:, hd:hd + 1]
    for g in range(grp):
        o_ref[:, g * hd:(g + 1) * hd] = out[g * tq:(g + 1) * tq].astype(BF16)


def fox_prompt(q16, k, v, c, c_k, batch):
    m, qw = q16.shape
    heads = c.shape[1]
    s = m // batch
    hd, kvh = FOX_HEAD_DIM, FOX_KV_HEADS
    grp = qw // (kvh * hd)
    nk, tq = c_k.shape[2], c_k.shape[4]
    nq = s // tq
    pre, rest = _split_scale(hd ** -0.5)
    kv_spec = pl.BlockSpec((s, kvh, hd), lambda b, h, qi: (b, 0, 0), pipeline_mode=pl.Buffered(1))
    return pl.pallas_call(
        functools.partial(_fox_p_kernel, grp=grp, pre=pre, rest=rest),
        grid=(batch, kvh, nq),
        in_specs=[pl.BlockSpec((tq, grp * hd), lambda b, h, qi: (b * nq + qi, h)),
                  kv_spec, kv_spec,
                  pl.BlockSpec((tq, heads), lambda b, h, qi: (b * nq + qi, 0)),
                  pl.BlockSpec((1, 1, nk, grp, tq), lambda b, h, qi: (b, h, 0, 0, 0))],
        out_specs=pl.BlockSpec((tq, grp * hd), lambda b, h, qi: (b * nq + qi, h)),
        out_shape=jax.ShapeDtypeStruct((m, qw), BF16),
        scratch_shapes=[pltpu.VMEM((s, hd), BF16), pltpu.VMEM((s, 2 * hd), BF16),
                        pltpu.VMEM((grp * tq, hd), F32), pltpu.VMEM((grp * tq, 1), F32),
                        pltpu.VMEM((grp * tq, 2 * hd), F32)],
        compiler_params=_params(("arbitrary", "arbitrary", "arbitrary")),
        name="fox_prompt",
    )(q16, k, v, c, c_k)


def _fox_s_kernel(pt_ref, q_ref, ccol_ref, cnt_ref, kn_ref, vn_ref, *rest, pages, steps, grp, scale):
    k_refs = rest[0:pages]
    v_refs = rest[pages:2 * pages]
    lf_refs = rest[2 * pages:3 * pages]
    o_ref, qbd_ref, m_ref, l_ref, acc_ref, carry_ref = rest[3 * pages:]
    p = pl.program_id(1)
    rows, hd = q_ref.shape[1], q_ref.shape[2]
    heads = rows // steps
    kvh = heads // grp
    psz = cnt_ref.shape[2]
    row = lax.broadcasted_iota(jnp.int32, (rows, hd), 0)
    row_kvh = (row % heads) // grp

    def page_update(k16, v16, neg_ck, mask):
        s = _dot_nt(qbd_ref[...], k16) * scale
        s = s + (ccol_ref[0] + jnp.concatenate([neg_ck] * steps, axis=0))
        if mask is not None:
            s = jnp.where(mask, s, NEG_INF)
        m_old = m_ref[...]
        m_new = jnp.maximum(m_old, jnp.max(s, axis=-1, keepdims=True))
        a = jnp.exp(m_old - m_new)
        pr = jnp.exp(s - m_new)
        l_ref[...] = a * l_ref[...] + jnp.sum(pr, axis=-1, keepdims=True)
        acc_ref[...] = a * acc_ref[...] + _dot(pr.astype(BF16), v16)
        m_ref[...] = m_new

    @pl.when(p == 0)
    def _():
        q = q_ref[0]
        for h in range(kvh):
            qbd_ref[:, h * hd:(h + 1) * hd] = jnp.where(row_kvh == h, q, jnp.zeros_like(q))
        m_ref[...] = jnp.full(m_ref.shape, NEG_INF, F32)
        l_ref[...] = jnp.zeros(l_ref.shape, F32)
        acc_ref[...] = jnp.zeros(acc_ref.shape, F32)
        carry_ref[...] = jnp.zeros(carry_ref.shape, F32)
        pad = jnp.zeros((psz - kn_ref.shape[1], kn_ref.shape[2]), F32)
        k16 = jnp.concatenate([kn_ref[0], pad], axis=0).astype(BF16)
        v16 = jnp.concatenate([vn_ref[0], pad], axis=0).astype(BF16)
        t = lax.broadcasted_iota(jnp.int32, (rows, psz), 0) // heads
        j = lax.broadcasted_iota(jnp.int32, (rows, psz), 1)
        page_update(k16, v16, -cnt_ref[0], j <= t)

    def flat16(ref):
        rows2d = ref.reshape(psz * kvh, hd)
        return jnp.concatenate([rows2d[pl.ds(h, psz, stride=kvh), :] for h in range(kvh)],
                               axis=1).astype(BF16)

    @pl.when(p > 0)
    def _():
        r = lax.broadcasted_iota(jnp.int32, (psz, 2 * psz), 0)
        c = lax.broadcasted_iota(jnp.int32, (psz, 2 * psz), 1)
        later_and_total = ((r > c) | (c >= psz)).astype(F32)
        carry = carry_ref[...]
        parts = []
        for i in range(pages):
            w = lax.dot_general(lf_refs[i][...], later_and_total, (((0,), (0,)), ((), ())),
                                precision=lax.Precision.HIGHEST, preferred_element_type=F32)
            parts.append(w[:, :psz] + carry)
            carry = carry + w[:, psz:psz + 1]
        carry_ref[...] = carry
        k16 = jnp.concatenate([flat16(k_refs[i]) for i in range(pages)], axis=0)
        v16 = jnp.concatenate([flat16(v_refs[i]) for i in range(pages)], axis=0)
        page_update(k16, v16, jnp.concatenate(parts, axis=1), None)

    @pl.when(p == pl.num_programs(1) - 1)
    def _():
        out = jnp.zeros((rows, hd), F32)
        for h in range(kvh):
            out = out + jnp.where(row_kvh == h, acc_ref[:, h * hd:(h + 1) * hd], 0.0)
        o_ref[0] = (out / l_ref[...]).astype(BF16)


def fox_sample(page_table, q16, c_col, c_new_t, k_new, v_new, pool_k, pool_v, pool_lf, layer, steps):
    bd, rows, hd = q16.shape
    heads = rows // steps
    psz, kvh = pool_k.shape[2], pool_k.shape[3]
    kw = kvh * hd
    n_pages = page_table.shape[1]
    pages = _pick(n_pages, 8, 4, 2, 1)
    n_steps = n_pages // pages + 1

    def page_idx(i, tail):
        return lambda b, p, pt: (layer, pt[b, n_pages - 1 - (jnp.maximum(p - 1, 0) * pages + i)]) + tail

    per_b = lambda a: pl.BlockSpec((1,) + a.shape[1:], lambda b, p, pt: (b,) + (0,) * (a.ndim - 1))
    kv_specs = lambda: [pl.BlockSpec((None, None, psz, kvh, hd), page_idx(i, (0, 0, 0))) for i in range(pages)]
    lf_specs = [pl.BlockSpec((None, None, psz, heads), page_idx(i, (0, 0))) for i in range(pages)]
    grid_spec = pltpu.PrefetchScalarGridSpec(
        num_scalar_prefetch=1,
        grid=(bd, n_steps),
        in_specs=[per_b(q16), per_b(c_col), per_b(c_new_t), per_b(k_new), per_b(v_new)]
        + kv_specs() + kv_specs() + lf_specs,
        out_specs=pl.BlockSpec((1, rows, hd), lambda b, p, pt: (b, 0, 0)),
        scratch_shapes=[pltpu.VMEM((rows, kw), BF16), pltpu.VMEM((rows, 1), F32),
                        pltpu.VMEM((rows, 1), F32), pltpu.VMEM((rows, kw), F32),
                        pltpu.VMEM((heads, 1), F32)],
    )
    return pl.pallas_call(
        functools.partial(_fox_s_kernel, pages=pages, steps=steps, grp=heads // FOX_KV_HEADS,
                          scale=FOX_HEAD_DIM ** -0.5),
        grid_spec=grid_spec,
        out_shape=jax.ShapeDtypeStruct((bd, rows, hd), BF16),
        compiler_params=_params(("arbitrary", "arbitrary")),
        name="fox_sample",
    )(page_table, q16, c_col, c_new_t, k_new, v_new,
      *([pool_k] * pages), *([pool_v] * pages), *([pool_lf] * pages))


def _time_major(a):
    return jnp.swapaxes(a, 0, 1).reshape((a.shape[0] * a.shape[1],) + a.shape[2:])


def _batch_major(a, bd):
    return jnp.swapaxes(a.reshape((a.shape[0] // bd, bd) + a.shape[1:]), 0, 1)


def _pad_rows(a, rows):
    return jnp.pad(a, ((0, 0), (0, rows - a.shape[1]), (0, 0)))


def kernel(x_prompt, x_sample, state_conv, cache_win_k, cache_win_v, cache_k, cache_v, cache_logf,
           state_ffn, page_table, w_in_a, conv_a, w_out_a, w_qkv_b, sinks_b, w_o_b, w_qkvf_c, b_f_c,
           w_o_c, ln1_g, ln1_b, w_up, conv_f, w_down, ln2_g, ln2_b):
    batch, seq, d = x_prompt.shape
    bd, steps, _ = x_sample.shape
    depth = ln1_g.shape[0]
    alpha = (2 * depth) ** 0.25
    n_mixers = 3
    f = w_down.shape[1]

    xp = x_prompt.reshape(batch * seq, d)
    xp16 = xp.astype(BF16)
    xs = _time_major(x_sample)
    xs16 = xs.astype(BF16)

    up16 = None
    conv_p, conv_s, wk_p, wv_p, wk_s, wv_s = [], [], [], [], [], []
    fk_p, fv_p, fl_p, fk_s, fv_s, fl_s = [], [], [], [], [], []
    ffn_p, ffn_s = [], []

    for i in range(depth):
        mix, j = i % n_mixers, i // n_mixers
        if mix == 0:
            zs, st_s, wb16, wc16, wh16 = gate_conv_sample(xs16, w_in_a, j, conv_a[j],
                                                          _time_major(state_conv[j]), bd)
            ys, wo16 = mm_sample(zs, w_out_a, j, d, F32, res=xs, alpha=alpha)
            zp, st_p = gate_conv_prompt(xp16, wb16, wc16, wh16, conv_a[j], batch)
            yp = mm_prompt(zp, wo16, d, 0, F32, res=xp, alpha=alpha)
            conv_p.append(st_p)
            conv_s.append(_batch_major(st_s, bd))
        elif mix == 1:
            hd, kvh = SW_HEAD_DIM, SW_KV_HEADS
            kw = kvh * hd
            grp = d // kw
            qkv_s, w16 = mm_sample(xs16, w_qkv_b, j, d + 2 * kw, F32)
            qp = mm_prompt(xp16, w16, d, 0, BF16)
            kp = mm_prompt(xp16, w16, kw, d, F32)
            vp = mm_prompt(xp16, w16, kw, d + kw, F32)
            op = swa_prompt(qp, kp, vp, sinks_b[j], batch)
            qs = qkv_s[:, :d].astype(BF16).reshape(steps, bd, kvh, grp, hd)
            qs = qs.transpose(1, 2, 3, 0, 4).reshape(bd, kvh, grp * steps, hd)
            kn = _batch_major(qkv_s[:, d:d + kw], bd)
            vn = _batch_major(qkv_s[:, d + kw:], bd)
            sink_col = jnp.repeat(sinks_b[j].reshape(kvh, grp), steps, axis=1)[..., None]
            os_ = swa_sample(qs, cache_win_k, cache_win_v, j, _pad_rows(kn, 8), _pad_rows(vn, 8),
                             sink_col, steps)
            os_ = os_.reshape(bd, kvh, grp, steps, hd).transpose(3, 0, 1, 2, 4).reshape(steps * bd, d)
            ys, wo16 = mm_sample(os_.astype(BF16), w_o_b, j, d, F32, res=xs, alpha=alpha)
            yp = mm_prompt(op, wo16, d, 0, F32, res=xp, alpha=alpha)
            keep = min(WINDOW, seq)
            wk_p.append(kp.reshape(batch, seq, kw)[:, seq - keep:].reshape(batch, keep, kvh, hd))
            wv_p.append(vp.reshape(batch, seq, kw)[:, seq - keep:].reshape(batch, keep, kvh, hd))
            wb = cache_win_k.shape[2]
            wk_s.append(jnp.concatenate([cache_win_k[j], kn.reshape(bd, steps, kvh, hd)], axis=1)[:, -wb:])
            wv_s.append(jnp.concatenate([cache_win_v[j], vn.reshape(bd, steps, kvh, hd)], axis=1)[:, -wb:])
        else:
            hd, kvh = FOX_HEAD_DIM, FOX_KV_HEADS
            kw = kvh * hd
            heads = d // hd
            grp = heads // kvh
            wf = w_qkvf_c[j, :, d + 2 * kw:]
            qkv_s, w16 = mm_sample(xs16, w_qkvf_c, j, d + 2 * kw, F32)
            lf_s, c_s = fox_logf_sample(xs16, wf, b_f_c[j], bd)
            qp = mm_prompt(xp16, w16, d, 0, BF16)
            kp = mm_prompt(xp16, w16, kw, d, F32, head_dim=hd)
            vp = mm_prompt(xp16, w16, kw, d + kw, F32, head_dim=hd)
            lf_p, c_p, ct_p = fox_logf_prompt(xp16, wf, b_f_c[j], batch)
            tk = _pick(seq, 512, 256, 128)
            c_k = ct_p.reshape(batch, kvh, grp, seq // tk, tk).transpose(0, 1, 3, 2, 4)
            op = fox_prompt(qp, kp, vp, c_p, c_k, batch)
            qs = _batch_major(qkv_s[:, :d].astype(BF16), bd).reshape(bd, steps * heads, hd)
            kn = _batch_major(qkv_s[:, d:d + kw], bd)
            vn = _batch_major(qkv_s[:, d + kw:], bd)
            c_b = _batch_major(c_s, bd)
            c_col = c_b.reshape(bd, steps * heads, 1)
            c_new_t = jnp.pad(c_b.transpose(0, 2, 1), ((0, 0), (0, 0), (0, PAGE_SIZE - steps)))
            os_ = fox_sample(page_table, qs, c_col, c_new_t, _pad_rows(kn, 8), _pad_rows(vn, 8),
                             cache_k, cache_v, cache_logf, j, steps)
            os_ = _time_major(os_.reshape(bd, steps, d))
            ys, wo16 = mm_sample(os_, w_o_c, j, d, F32, res=xs, alpha=alpha)
            yp = mm_prompt(op, wo16, d, 0, F32, res=xp, alpha=alpha)
            fk_p.append(kp.reshape(batch, seq, kvh, hd))
            fv_p.append(vp.reshape(batch, seq, kvh, hd))
            fl_p.append(lf_p.reshape(batch, seq, heads))
            fk_s.append(kn.reshape(bd, steps, kvh, hd))
            fv_s.append(vn.reshape(bd, steps, kvh, hd))
            fl_s.append(_batch_major(lf_s, bd))
        xs, xs16 = layer_norm_rows(ys, ln1_g[i], ln1_b[i])
        xp, xp16 = layer_norm_rows(yp, ln1_g[i], ln1_b[i])

        prev_ffn = _time_major(state_ffn[i])
        if up16 is None:
            a_s, sg_s, su_s, wg16, wu16 = ffn_up_sample(xs16, w_up, i, conv_f[i], prev_ffn, bd)
            u_off = 0
        else:
            a_s, sg_s, su_s = ffn_up_sample(xs16, up16, None, conv_f[i], prev_ffn, bd)
            wg16 = wu16 = up16
            u_off = f // _pick(f, 256, 128)
        a_p, sg_p, su_p, wd16, *nxt = ffn_up_prompt(xp16, wg16, wu16, u_off, conv_f[i], batch,
                                                   w_down, w_up, i)
        up16 = nxt[0] if nxt else None
        ys = mm_prompt(a_s, wd16, d, 0, F32, res=xs, alpha=alpha)
        yp = mm_prompt(a_p, wd16, d, 0, F32, res=xp, alpha=alpha)
        ffn_p.append(jnp.concatenate([sg_p, su_p], axis=-1))
        ffn_s.append(_batch_major(jnp.concatenate([sg_s, su_s], axis=-1), bd))
        xs, xs16 = layer_norm_rows(ys, ln2_g[i], ln2_b[i])
        xp, xp16 = layer_norm_rows(yp, ln2_g[i], ln2_b[i])

    return (xp.reshape(batch, seq, d), _batch_major(xs, bd),
            jnp.stack(conv_p), jnp.stack(conv_s), jnp.stack(wk_p), jnp.stack(wv_p),
            jnp.stack(wk_s), jnp.stack(wv_s), jnp.stack(fk_p), jnp.stack(fv_p), jnp.stack(fl_p),
            jnp.stack(fk_s), jnp.stack(fv_s), jnp.stack(fl_s), jnp.stack(ffn_p), jnp.stack(ffn_s))
```

```python
import functools
import math

import jax
import jax.numpy as jnp
from jax import lax
from jax.experimental import pallas as pl
from jax.experimental.pallas import tpu as pltpu

F32 = jnp.float32
BF16 = jnp.bfloat16

CONV_TAPS = 3
CONV_PREV = CONV_TAPS - 1
CONV_HALO = 8
SW_HEAD_DIM = 64
SW_KV_HEADS = 8
WINDOW = 128
FOX_HEAD_DIM = 128
FOX_KV_HEADS = 8
PAGE_SIZE = 128
LN_EPS = 1e-5
NEG_INF = -1e30
LOG2_E = math.log2(math.e)

V7X_VMEM_BYTES = 64 * 1024 * 1024
VMEM_LIMIT = V7X_VMEM_BYTES - 8 * 1024 * 1024
LANE = 128
BF16_SUBLANES = 16


def _params(semantics):
    return pltpu.CompilerParams(dimension_semantics=semantics, vmem_limit_bytes=VMEM_LIMIT)


def _dot(a, b):
    return jnp.dot(a, b, preferred_element_type=F32)


def _dot_nt(a, b):
    return lax.dot_general(a, b, (((1,), (1,)), ((), ())), preferred_element_type=F32)


def _split_scale(scale):
    pre = 2.0 ** math.floor(math.log2(scale))
    return pre, scale / pre


def _pick(n, *cands):
    for c in cands:
        if n % c == 0:
            return c
    raise ValueError(f"no tile for {n} among {cands}")


def _conv_rows(u, prev, cw, shift):
    rows = u.shape[0]
    full = jnp.concatenate([prev, u], axis=0)
    y = cw[0:1, :] * full[0:rows] + cw[1:2, :] * full[shift:shift + rows] + cw[2:3, :] * u
    return y, full[rows:rows + CONV_PREV * shift]


def _conv_block(u, halo, cw):
    full = jnp.concatenate([halo, u], axis=0)
    y = cw[0:1, :] * pltpu.roll(full, 2, 0) + cw[1:2, :] * pltpu.roll(full, 1, 0) + cw[2:3, :] * full
    return y[CONV_HALO:]


def _gate_conv_core(x, wb, wc, wh, cw, prev, shift):
    bg = _dot(x, wb)
    c = _dot(x, wc)
    h = _dot(x, wh)
    y, tail = _conv_rows(c * h, prev, cw, shift)
    return (bg * y).astype(BF16), tail


def _gate_conv_p_kernel(x_ref, wb_ref, wc_ref, wh_ref, cw_ref, z_ref, st_ref, *, rb):
    wb, wc, wh, cw = wb_ref[...], wc_ref[...], wh_ref[...], cw_ref[...]
    halo = jnp.zeros((CONV_HALO, wb.shape[1]), F32)
    for r in range(x_ref.shape[0] // rb):
        x = x_ref[r * rb:(r + 1) * rb, :]
        u = _dot(x, wc) * _dot(x, wh)
        z_ref[r * rb:(r + 1) * rb, :] = (_dot(x, wb) * _conv_block(u, halo, cw)).astype(BF16)
        halo = u[rb - CONV_HALO:rb]
    st_ref[0] = halo[CONV_HALO - CONV_PREV:CONV_HALO]


def _gate_conv_s_kernel(x_ref, wb_ref, wc_ref, wh_ref, cw_ref, prev_ref,
                        z_ref, st_ref, wb16_ref, wc16_ref, wh16_ref, *, shift):
    wb = wb_ref[...].astype(BF16)
    wc = wc_ref[...].astype(BF16)
    wh = wh_ref[...].astype(BF16)
    wb16_ref[...] = wb
    wc16_ref[...] = wc
    wh16_ref[...] = wh
    z, tail = _gate_conv_core(x_ref[...], wb, wc, wh, cw_ref[...], prev_ref[...], shift)
    z_ref[...] = z
    st_ref[...] = tail


def gate_conv_sample(x16, w_in, layer, cw, prev, shift):
    rows, d = x16.shape
    c = w_in.shape[2] // 3
    tn = _pick(c, 256, 128)
    nc = c // tn
    w_spec = lambda sec: pl.BlockSpec((None, d, tn), lambda n, sec=sec: (layer, 0, sec * nc + n))
    col = lambda r: pl.BlockSpec((r, tn), lambda n: (0, n))
    return pl.pallas_call(
        functools.partial(_gate_conv_s_kernel, shift=shift),
        grid=(nc,),
        in_specs=[pl.BlockSpec((rows, d), lambda n: (0, 0)), w_spec(0), w_spec(1), w_spec(2),
                  col(CONV_TAPS), col(CONV_PREV * shift)],
        out_specs=[col(rows), col(CONV_PREV * shift), col(d), col(d), col(d)],
        out_shape=[jax.ShapeDtypeStruct((rows, c), BF16),
                   jax.ShapeDtypeStruct((CONV_PREV * shift, c), F32),
                   jax.ShapeDtypeStruct((d, c), BF16),
                   jax.ShapeDtypeStruct((d, c), BF16),
                   jax.ShapeDtypeStruct((d, c), BF16)],
        compiler_params=_params(("arbitrary",)),
        name="gate_conv_sample",
    )(x16, w_in, w_in, w_in, cw, prev)


def gate_conv_prompt(x16, wb16, wc16, wh16, cw, batch):
    m, d = x16.shape
    s = m // batch
    c = wb16.shape[1]
    tn = _pick(c, 256, 128)
    w_spec = pl.BlockSpec((d, tn), lambda b, n: (0, n))
    return pl.pallas_call(
        functools.partial(_gate_conv_p_kernel, rb=_pick(s, 512, 256, 128)),
        grid=(batch, c // tn),
        in_specs=[pl.BlockSpec((s, d), lambda b, n: (b, 0)),
                  w_spec, w_spec, w_spec,
                  pl.BlockSpec((CONV_TAPS, tn), lambda b, n: (0, n))],
        out_specs=[pl.BlockSpec((s, tn), lambda b, n: (b, n)),
                   pl.BlockSpec((1, CONV_PREV, tn), lambda b, n: (b, 0, n))],
        out_shape=[jax.ShapeDtypeStruct((m, c), BF16),
                   jax.ShapeDtypeStruct((batch, CONV_PREV, c), F32)],
        compiler_params=_params(("arbitrary", "arbitrary")),
        name="gate_conv_prompt",
    )(x16, wb16, wc16, wh16, cw)


def _ffn_up_core(x, wg, wu, cwg, cwu, prev_g, prev_u, shift):
    cg, tail_g = _conv_rows(_dot(x, wg), prev_g, cwg, shift)
    cu, tail_u = _conv_rows(_dot(x, wu), prev_u, cwu, shift)
    return _silu_gate(cg, cu), tail_g, tail_u


def _silu_gate(cg, cu):
    half = 0.5 * cg
    return ((half * jnp.tanh(half) + half) * cu).astype(BF16)


def _ffn_up_p_kernel(*refs, rb, has_next):
    if has_next:
        (x_ref, wg_ref, wu_ref, cwg_ref, cwu_ref, dn_ref, up_ref,
         a_ref, sg_ref, su_ref, dn16_ref, up16_ref) = refs
        up16_ref[...] = up_ref[...].astype(BF16)
    else:
        x_ref, wg_ref, wu_ref, cwg_ref, cwu_ref, dn_ref, a_ref, sg_ref, su_ref, dn16_ref = refs
    dn16_ref[...] = dn_ref[...].astype(BF16)
    wg, wu, cwg, cwu = wg_ref[...], wu_ref[...], cwg_ref[...], cwu_ref[...]
    halo_g = jnp.zeros((CONV_HALO, wg.shape[1]), F32)
    halo_u = halo_g
    for r in range(x_ref.shape[0] // rb):
        x = x_ref[r * rb:(r + 1) * rb, :]
        hg = _dot(x, wg)
        hu = _dot(x, wu)
        a_ref[r * rb:(r + 1) * rb, :] = _silu_gate(_conv_block(hg, halo_g, cwg),
                                                   _conv_block(hu, halo_u, cwu))
        halo_g = hg[rb - CONV_HALO:rb]
        halo_u = hu[rb - CONV_HALO:rb]
    sg_ref[0] = halo_g[CONV_HALO - CONV_PREV:CONV_HALO]
    su_ref[0] = halo_u[CONV_HALO - CONV_PREV:CONV_HALO]


def _ffn_up_s_kernel(x_ref, wg_ref, wu_ref, cwg_ref, cwu_ref, pg_ref, pu_ref,
                     a_ref, sg_ref, su_ref, *w16_refs, shift):
    wg = wg_ref[...].astype(BF16)
    wu = wu_ref[...].astype(BF16)
    if w16_refs:
        w16_refs[0][...] = wg
        w16_refs[1][...] = wu
    act, tg, tu = _ffn_up_core(x_ref[...], wg, wu, cwg_ref[...], cwu_ref[...],
                               pg_ref[...], pu_ref[...], shift)
    a_ref[...] = act
    sg_ref[...] = tg
    su_ref[...] = tu


def ffn_up_sample(x16, w_up, layer, cw, prev, shift):
    rows, d = x16.shape
    emit = layer is not None
    f = w_up.shape[-1] // 2
    tn = _pick(f, 256, 128)
    nf = f // tn
    sec = lambda r, k: pl.BlockSpec((r, tn), lambda n, k=k: (0, k * nf + n))
    if emit:
        wsec = lambda k: pl.BlockSpec((None, d, tn), lambda n, k=k: (layer, 0, k * nf + n))
    else:
        wsec = lambda k: sec(d, k)
    col = lambda r: pl.BlockSpec((r, tn), lambda n: (0, n))
    pr = CONV_PREV * shift
    w16_shape = jax.ShapeDtypeStruct((d, f), BF16)
    return pl.pallas_call(
        functools.partial(_ffn_up_s_kernel, shift=shift),
        grid=(nf,),
        in_specs=[pl.BlockSpec((rows, d), lambda n: (0, 0)), wsec(0), wsec(1),
                  sec(CONV_TAPS, 0), sec(CONV_TAPS, 1), sec(pr, 0), sec(pr, 1)],
        out_specs=[col(rows), col(pr), col(pr)] + ([col(d), col(d)] if emit else []),
        out_shape=[jax.ShapeDtypeStruct((rows, f), BF16),
                   jax.ShapeDtypeStruct((pr, f), F32),
                   jax.ShapeDtypeStruct((pr, f), F32)] + ([w16_shape, w16_shape] if emit else []),
        compiler_params=_params(("arbitrary",)),
        name="ffn_up_sample",
    )(x16, w_up, w_up, cw, cw, prev, prev)


def ffn_up_prompt(x16, wg16, wu16, u_off, cw, batch, w_down, w_up, layer):
    m, d = x16.shape
    s = m // batch
    f = w_down.shape[1]
    tn = _pick(f, 256, 128)
    nf = f // tn
    steps = batch * nf
    has_next = layer + 1 < w_up.shape[0]
    dn_rows, up_cols = f // steps, 2 * f // steps
    assert dn_rows * steps == f and dn_rows % BF16_SUBLANES == 0 and up_cols % LANE == 0
    step = lambda b, n: b * nf + n
    st_spec = pl.BlockSpec((1, CONV_PREV, tn), lambda b, n: (b, 0, n))
    in_specs = [pl.BlockSpec((s, d), lambda b, n: (b, 0)),
                pl.BlockSpec((d, tn), lambda b, n: (0, n)),
                pl.BlockSpec((d, tn), lambda b, n: (0, u_off + n)),
                pl.BlockSpec((CONV_TAPS, tn), lambda b, n: (0, n)),
                pl.BlockSpec((CONV_TAPS, tn), lambda b, n: (0, nf + n)),
                pl.BlockSpec((None, dn_rows, d), lambda b, n: (layer, step(b, n), 0))]
    out_specs = [pl.BlockSpec((s, tn), lambda b, n: (b, n)), st_spec, st_spec,
                 pl.BlockSpec((dn_rows, d), lambda b, n: (step(b, n), 0))]
    out_shape = [jax.ShapeDtypeStruct((m, f), BF16),
                 jax.ShapeDtypeStruct((batch, CONV_PREV, f), F32),
                 jax.ShapeDtypeStruct((batch, CONV_PREV, f), F32),
                 jax.ShapeDtypeStruct((f, d), BF16)]
    args = [x16, wg16, wu16, cw, cw, w_down]
    if has_next:
        in_specs.append(pl.BlockSpec((None, d, up_cols), lambda b, n: (layer + 1, 0, step(b, n))))
        out_specs.append(pl.BlockSpec((d, up_cols), lambda b, n: (0, step(b, n))))
        out_shape.append(jax.ShapeDtypeStruct((d, 2 * f), BF16))
        args.append(w_up)
    return pl.pallas_call(
        functools.partial(_ffn_up_p_kernel, rb=_pick(s, 512, 256, 128), has_next=has_next),
        grid=(batch, nf),
        in_specs=in_specs,
        out_specs=out_specs,
        out_shape=out_shape,
        compiler_params=_params(("arbitrary", "arbitrary")),
        name="ffn_up_prompt",
    )(*args)


def _mm_p_kernel(*refs, alpha, has_res):
    if has_res:
        x_ref, w_ref, r_ref, o_ref = refs
    else:
        x_ref, w_ref, o_ref = refs
    acc = _dot(x_ref[...], w_ref[...])
    if has_res:
        acc = alpha * r_ref[...] + acc
    if len(o_ref.shape) == 2:
        o_ref[...] = acc.astype(o_ref.dtype)
    else:
        _, heads, hd = o_ref.shape
        for h in range(heads):
            o_ref[:, h, :] = acc[:, h * hd:(h + 1) * hd].astype(o_ref.dtype)


def _mm_s_kernel(*refs, alpha, has_res, w_transposed):
    if has_res:
        x_ref, w_ref, r_ref, o_ref, w16_ref = refs
    else:
        x_ref, w_ref, o_ref, w16_ref = refs
    w = w_ref[...]
    if w_transposed:
        w = w.T
    w = w.astype(BF16)
    w16_ref[...] = w
    acc = _dot(x_ref[...], w)
    if has_res:
        acc = alpha * r_ref[...] + acc
    o_ref[...] = acc.astype(o_ref.dtype)


def mm_sample(x16, w, layer, n_out, out_dtype, res=None, alpha=None, w_transposed=False):
    rows, k = x16.shape
    tn = _pick(n_out, 512, 256, 128) if k <= 4096 else _pick(n_out, 256, 128)
    col = lambda r: pl.BlockSpec((r, tn), lambda n: (0, n))
    if w_transposed:
        w_spec = pl.BlockSpec((None, tn, k), lambda n: (layer, n, 0))
    else:
        w_spec = pl.BlockSpec((None, k, tn), lambda n: (layer, 0, n))
    in_specs = [pl.BlockSpec((rows, k), lambda n: (0, 0)), w_spec]
    args = [x16, w]
    if res is not None:
        in_specs.append(col(rows))
        args.append(res)
    return pl.pallas_call(
        functools.partial(_mm_s_kernel, alpha=alpha, has_res=res is not None, w_transposed=w_transposed),
        grid=(n_out // tn,),
        in_specs=in_specs,
        out_specs=[col(rows), col(k)],
        out_shape=[jax.ShapeDtypeStruct((rows, n_out), out_dtype),
                   jax.ShapeDtypeStruct((k, n_out), BF16)],
        compiler_params=_params(("arbitrary",)),
        name="mm_sample",
    )(*args)


def mm_prompt(x16, w16, n_out, col_off, out_dtype, res=None, alpha=None, head_dim=None):
    m, k = x16.shape
    if k <= 4096:
        tm, tn = _pick(m, 1024, 512, 256, 128), _pick(n_out, 1024, 512, 256, 128)
    else:
        tm, tn = _pick(m, 512, 256, 128), _pick(n_out, 512, 256, 128)
    assert col_off % tn == 0
    off = col_off // tn
    in_specs = [pl.BlockSpec((tm, k), lambda i, n: (i, 0)),
                pl.BlockSpec((k, tn), lambda i, n: (0, off + n))]
    args = [x16, w16]
    if res is not None:
        in_specs.append(pl.BlockSpec((tm, tn), lambda i, n: (i, n)))
        args.append(res)
    if head_dim is None:
        out_spec = pl.BlockSpec((tm, tn), lambda i, n: (i, n))
        out_shape = jax.ShapeDtypeStruct((m, n_out), out_dtype)
    else:
        assert tn == n_out
        out_spec = pl.BlockSpec((tm, n_out // head_dim, head_dim), lambda i, n: (i, 0, 0))
        out_shape = jax.ShapeDtypeStruct((m, n_out // head_dim, head_dim), out_dtype)
    return pl.pallas_call(
        functools.partial(_mm_p_kernel, alpha=alpha, has_res=res is not None),
        grid=(m // tm, n_out // tn),
        in_specs=in_specs,
        out_specs=out_spec,
        out_shape=out_shape,
        compiler_params=_params(("arbitrary", "arbitrary")),
        name="mm_prompt",
    )(*args)


def _ln_kernel(y_ref, g_ref, b_ref, xf_ref, xb_ref):
    y = y_ref[...]
    mu = jnp.mean(y, axis=-1, keepdims=True)
    dev = y - mu
    var = jnp.mean(dev * dev, axis=-1, keepdims=True)
    out = dev * lax.rsqrt(var + LN_EPS) * g_ref[...] + b_ref[...]
    xf_ref[...] = out
    xb_ref[...] = out.astype(BF16)


def layer_norm_rows(y, g, b):
    m, d = y.shape
    tr = _pick(m, 512, 256, 128)
    row = pl.BlockSpec((tr, d), lambda i: (i, 0))
    vec = pl.BlockSpec((1, d), lambda i: (0, 0))
    return pl.pallas_call(
        _ln_kernel,
        grid=(m // tr,),
        in_specs=[row, vec, vec],
        out_specs=[row, row],
        out_shape=[jax.ShapeDtypeStruct((m, d), F32), jax.ShapeDtypeStruct((m, d), BF16)],
        compiler_params=_params(("arbitrary",)),
        name="layer_norm",
    )(y, g.reshape(1, d), b.reshape(1, d))


def _sink_softmax_pv(s, sink, v16):
    m = jnp.maximum(jnp.max(s, axis=-1, keepdims=True), sink)
    p = jnp.exp(s - m)
    den = jnp.sum(p, axis=-1, keepdims=True) + jnp.exp(sink - m)
    return _dot((p / den).astype(BF16), v16)


def _swa_p_kernel(q_ref, kp_ref, ko_ref, vp_ref, vo_ref, sink_ref, o_ref, *, grp, pre, rest):
    blk = q_ref.shape[0]
    rows = grp * blk
    first = pl.program_id(1) == 0
    kcat = jnp.concatenate([kp_ref[...], ko_ref[...]], axis=0).astype(BF16)
    vcat = jnp.concatenate([vp_ref[...], vo_ref[...]], axis=0).astype(BF16)
    tl = lax.broadcasted_iota(jnp.int32, (rows, 2 * blk), 0) % blk
    j = lax.broadcasted_iota(jnp.int32, (rows, 2 * blk), 1)
    mask = (j >= tl) & (j <= tl + WINDOW) & ((j >= blk) | jnp.logical_not(first))
    hd = SW_HEAD_DIM
    unit = (lax.broadcasted_iota(jnp.int32, (2 * blk, hd), 1) == 0).astype(BF16)
    for h in range(SW_KV_HEADS):
        heads = range(h * grp, (h + 1) * grp)
        q = jnp.concatenate([q_ref[:, a * hd:(a + 1) * hd] for a in heads], axis=0) * pre
        sink = jnp.concatenate([jnp.full((blk, 1), sink_ref[a], F32) for a in heads], axis=0)
        s = _dot_nt(q, kcat[:, h * hd:(h + 1) * hd])
        if rest != 1.0:
            s = s * rest
        s = jnp.where(mask, s, NEG_INF)
        m = jnp.maximum(jnp.max(s, axis=-1, keepdims=True), sink)
        vh = jnp.concatenate([vcat[:, h * hd:(h + 1) * hd], unit], axis=1)
        ov = _dot(jnp.exp(s - m).astype(BF16), vh)
        den = ov[:, hd:hd + 1] + jnp.exp(sink - m)
        o = (ov[:, :hd] / den).astype(BF16)
        for g, a in enumerate(heads):
            o_ref[:, a * hd:(a + 1) * hd] = o[g * blk:(g + 1) * blk]


def swa_prompt(q16, k, v, sinks, batch):
    m, qw = q16.shape
    s = m // batch
    blk = WINDOW
    nb = s // blk
    kw = SW_KV_HEADS * SW_HEAD_DIM
    grp = qw // kw
    own = pl.BlockSpec((blk, kw), lambda b, i: (b * nb + i, 0))
    prev = pl.BlockSpec((blk, kw), lambda b, i: (b * nb + jnp.maximum(i - 1, 0), 0))
    pre, rest = _split_scale(SW_HEAD_DIM ** -0.5)
    return pl.pallas_call(
        functools.partial(_swa_p_kernel, grp=grp, pre=pre, rest=rest),
        grid=(batch, nb),
        in_specs=[pl.BlockSpec((blk, qw), lambda b, i: (b * nb + i, 0)),
                  prev, own, prev, own,
                  pl.BlockSpec(memory_space=pltpu.SMEM)],
        out_specs=pl.BlockSpec((blk, qw), lambda b, i: (b * nb + i, 0)),
        out_shape=jax.ShapeDtypeStruct((m, qw), BF16),
        compiler_params=_params(("arbitrary", "arbitrary")),
        name="swa_prompt",
    )(q16, k, k, v, v, sinks)


def _swa_s_kernel(q_ref, ck_ref, cv_ref, kn_ref, vn_ref, sink_ref, o_ref, *, steps, scale):
    wb, kvh, hd = ck_ref.shape
    pad = jnp.zeros((wb - kn_ref.shape[1], hd), F32)
    rows = q_ref.shape[2]
    t = lax.broadcasted_iota(jnp.int32, (rows, 2 * wb), 0) % steps
    j = lax.broadcasted_iota(jnp.int32, (rows, 2 * wb), 1)
    mask = (j >= t + wb - WINDOW) & (j <= t + wb)

    def keys(cache_ref, new_ref, h):
        cached = cache_ref.reshape(wb * kvh, hd)[pl.ds(h, wb, stride=kvh), :]
        return jnp.concatenate([cached, new_ref[0, :, h * hd:(h + 1) * hd], pad], axis=0).astype(BF16)

    for h in range(kvh):
        s = _dot_nt(q_ref[0, h], keys(ck_ref, kn_ref, h)) * scale
        s = jnp.where(mask, s, NEG_INF)
        o_ref[0, h] = _sink_softmax_pv(s, sink_ref[h], keys(cv_ref, vn_ref, h))


def swa_sample(q16, cache_k, cache_v, layer, k_new, v_new, sink_col, steps):
    bd, kvh, rows, hd = q16.shape
    wb = cache_k.shape[2]
    per_b = lambda a: pl.BlockSpec((1,) + a.shape[1:], lambda b: (b,) + (0,) * (a.ndim - 1))
    cache_spec = pl.BlockSpec((None, None, wb, kvh, hd), lambda b: (layer, b, 0, 0, 0))
    return pl.pallas_call(
        functools.partial(_swa_s_kernel, steps=steps, scale=SW_HEAD_DIM ** -0.5),
        grid=(bd,),
        in_specs=[per_b(q16), cache_spec, cache_spec, per_b(k_new), per_b(v_new),
                  pl.BlockSpec(sink_col.shape, lambda b: (0, 0, 0))],
        out_specs=pl.BlockSpec((1, kvh, rows, hd), lambda b: (b, 0, 0, 0)),
        out_shape=jax.ShapeDtypeStruct((bd, kvh, rows, hd), F32),
        compiler_params=_params(("arbitrary",)),
        name="swa_sample",
    )(q16, cache_k, cache_v, k_new, v_new, sink_col)


def _log_sigmoid(z):
    return jnp.minimum(z, 0.0) - jnp.log1p(jnp.exp(-jnp.abs(z)))


def _fox_logf_p_kernel(x_ref, wf_ref, bf_ref, lf_ref, c_ref, ct_ref, *, blk):
    lf = _log_sigmoid(_dot(x_ref[...], wf_ref[...].astype(BF16)) + bf_ref[...])
    lf_ref[...] = lf
    r = lax.broadcasted_iota(jnp.int32, (blk, blk), 0)
    c = lax.broadcasted_iota(jnp.int32, (blk, blk), 1)
    upto_rows = (c <= r).astype(F32)
    upto_cols = (r <= c).astype(F32)
    carry = jnp.zeros((1, lf.shape[1]), F32)
    carry_t = jnp.zeros((lf.shape[1], 1), F32)
    for i in range(lf.shape[0] // blk):
        part = lf[i * blk:(i + 1) * blk]
        cb = jnp.dot(upto_rows, part, precision=lax.Precision.HIGHEST,
                     preferred_element_type=F32) + carry
        c_ref[i * blk:(i + 1) * blk, :] = cb
        carry = cb[blk - 1:blk, :]
        cbt = lax.dot_general(part, upto_cols, (((0,), (0,)), ((), ())),
                              precision=lax.Precision.HIGHEST, preferred_element_type=F32) + carry_t
        ct_ref[0, :, i * blk:(i + 1) * blk] = cbt
        carry_t = cbt[:, blk - 1:blk]


def fox_logf_prompt(x16, wf, bf, batch):
    m, d = x16.shape
    s = m // batch
    h = wf.shape[1]
    row = pl.BlockSpec((s, h), lambda b: (b, 0))
    return pl.pallas_call(
        functools.partial(_fox_logf_p_kernel, blk=_pick(s, 256, 128)),
        grid=(batch,),
        in_specs=[pl.BlockSpec((s, d), lambda b: (b, 0)),
                  pl.BlockSpec((d, h), lambda b: (0, 0)),
                  pl.BlockSpec((1, h), lambda b: (0, 0))],
        out_specs=[row, row, pl.BlockSpec((1, h, s), lambda b: (b, 0, 0))],
        out_shape=[jax.ShapeDtypeStruct((m, h), F32), jax.ShapeDtypeStruct((m, h), F32),
                   jax.ShapeDtypeStruct((batch, h, s), F32)],
        compiler_params=_params(("arbitrary",)),
        name="fox_logf_prompt",
    )(x16, wf, bf.reshape(1, h))


def _fox_logf_s_kernel(x_ref, wf_ref, bf_ref, lf_ref, c_ref, *, shift):
    lf = _log_sigmoid(_dot(x_ref[...], wf_ref[...].astype(BF16)) + bf_ref[...])
    lf_ref[...] = lf
    run = lf[0:shift]
    c_ref[0:shift, :] = run
    for t in range(1, lf.shape[0] // shift):
        run = run + lf[t * shift:(t + 1) * shift]
        c_ref[t * shift:(t + 1) * shift, :] = run


def fox_logf_sample(x16, wf, bf, shift):
    rows, d = x16.shape
    h = wf.shape[1]
    full = lambda a, b: pl.BlockSpec((a, b), lambda i: (0, 0))
    return pl.pallas_call(
        functools.partial(_fox_logf_s_kernel, shift=shift),
        grid=(1,),
        in_specs=[full(rows, d), full(d, h), full(1, h)],
        out_specs=[full(rows, h), full(rows, h)],
        out_shape=[jax.ShapeDtypeStruct((rows, h), F32), jax.ShapeDtypeStruct((rows, h), F32)],
        compiler_params=_params(("arbitrary",)),
        name="fox_logf_sample",
    )(x16, wf, bf.reshape(1, h))


def _fox_p_kernel(q_ref, k_ref, v_ref, c_ref, ck_ref, o_ref, k16_ref, v16_ref, cq_ref, m_ref, acc_ref,
                  *, grp, pre, rest):
    h = pl.program_id(1)
    qi = pl.program_id(2)
    tq = q_ref.shape[0]
    tk = tq
    hd = FOX_HEAD_DIM
    rows = grp * tq
    c_exp = rest * LOG2_E

    @pl.when(qi == 0)
    def _():
        seq, kvh = k_ref.shape[0], k_ref.shape[1]
        rows_h = pl.ds(h, seq, stride=kvh)
        k16_ref[...] = k_ref.reshape(seq * kvh, hd)[rows_h, :].astype(BF16)
        unit = (lax.broadcasted_iota(jnp.int32, (seq, hd), 1) == 0).astype(BF16)
        v16_ref[...] = jnp.concatenate([v_ref.reshape(seq * kvh, hd)[rows_h, :].astype(BF16), unit], axis=1)

    q = jnp.concatenate([q_ref[:, g * hd:(g + 1) * hd] for g in range(grp)], axis=0) * pre
    c_blk = c_ref[...]
    lane = lax.broadcasted_iota(jnp.int32, c_blk.shape, 1)
    for g in range(grp):
        col = jnp.sum(jnp.where(lane == h * grp + g, c_blk, 0.0), axis=-1, keepdims=True)
        cq_ref[g * tq:(g + 1) * tq, :] = jnp.broadcast_to(col * (1.0 / rest), (tq, hd))
    m_ref[...] = jnp.full(m_ref.shape, NEG_INF, F32)
    acc_ref[...] = jnp.zeros(acc_ref.shape, F32)

    def update(j, diagonal):
        off = pl.multiple_of(j * tk, tk)
        z = _dot_nt(q, k16_ref[pl.ds(off, tk), :])
        cq = cq_ref[...]
        t = jnp.concatenate([z[:, i * hd:(i + 1) * hd] + cq for i in range(tk // hd)], axis=1)
        ck = ck_ref[0, 0, j] * (1.0 / rest)
        t = (t.reshape(grp, tq, tk) - ck[:, None, :]).reshape(rows, tk)
        if diagonal:
            r = lax.broadcasted_iota(jnp.int32, (rows, tk), 0) % tq
            c = lax.broadcasted_iota(jnp.int32, (rows, tk), 1)
            t = jnp.where(c <= r, t, NEG_INF)
        m_old = m_ref[...]
        m_new = jnp.maximum(m_old, jnp.max(t, axis=-1, keepdims=True))
        a = jnp.exp2((m_old - m_new) * c_exp)
        p = jnp.exp2((t - m_new) * c_exp)
        acc_ref[...] = a * acc_ref[...] + _dot(p.astype(BF16), v16_ref[pl.ds(off, tk), :])
        m_ref[...] = m_new

    def below_diagonal(j, carry):
        update(j, False)
        return carry

    lax.fori_loop(0, qi, below_diagonal, 0)
    update(qi, True)
    acc = acc_ref[...]
    out = acc[:, :hd] / acc[:, hd:hd + 1]
    for g in range(grp):
        o_ref[:, g * hd:(g + 1) * hd] = out[g * tq:(g + 1) * tq].astype(BF16)


def fox_prompt(q16, k, v, c, c_k, batch):
    m, qw = q16.shape
    heads = c.shape[1]
    s = m // batch
    hd, kvh = FOX_HEAD_DIM, FOX_KV_HEADS
    grp = qw // (kvh * hd)
    nk, tq = c_k.shape[2], c_k.shape[4]
    nq = s // tq
    pre, rest = _split_scale(hd ** -0.5)
    kv_spec = pl.BlockSpec((s, kvh, hd), lambda b, h, qi: (b, 0, 0), pipeline_mode=pl.Buffered(1))
    return pl.pallas_call(
        functools.partial(_fox_p_kernel, grp=grp, pre=pre, rest=rest),
        grid=(batch, kvh, nq),
        in_specs=[pl.BlockSpec((tq, grp * hd), lambda b, h, qi: (b * nq + qi, h)),
                  kv_spec, kv_spec,
                  pl.BlockSpec((tq, heads), lambda b, h, qi: (b * nq + qi, 0)),
                  pl.BlockSpec((1, 1, nk, grp, tq), lambda b, h, qi: (b, h, 0, 0, 0))],
        out_specs=pl.BlockSpec((tq, grp * hd), lambda b, h, qi: (b * nq + qi, h)),
        out_shape=jax.ShapeDtypeStruct((m, qw), BF16),
        scratch_shapes=[pltpu.VMEM((s, hd), BF16), pltpu.VMEM((s, 2 * hd), BF16),
                        pltpu.VMEM((grp * tq, hd), F32), pltpu.VMEM((grp * tq, 1), F32),
                        pltpu.VMEM((grp * tq, 2 * hd), F32)],
        compiler_params=_params(("arbitrary", "arbitrary", "arbitrary")),
        name="fox_prompt",
    )(q16, k, v, c, c_k)


def _fox_s_kernel(pt_ref, q_ref, ccol_ref, cnt_ref, kn_ref, vn_ref, *rest, pages, steps, grp, scale):
    k_refs = rest[0:pages]
    v_refs = rest[pages:2 * pages]
    lf_refs = rest[2 * pages:3 * pages]
    o_ref, qbd_ref, m_ref, l_ref, acc_ref, carry_ref = rest[3 * pages:]
    p = pl.program_id(1)
    rows, hd = q_ref.shape[1], q_ref.shape[2]
    heads = rows // steps
    kvh = heads // grp
    psz = cnt_ref.shape[2]
    row = lax.broadcasted_iota(jnp.int32, (rows, hd), 0)
    row_kvh = (row % heads) // grp

    def page_update(k16, v16, neg_ck, mask):
        s = _dot_nt(qbd_ref[...], k16) * scale
        s = s + (ccol_ref[0] + jnp.concatenate([neg_ck] * steps, axis=0))
        if mask is not None:
            s = jnp.where(mask, s, NEG_INF)
        m_old = m_ref[...]
        m_new = jnp.maximum(m_old, jnp.max(s, axis=-1, keepdims=True))
        a = jnp.exp(m_old - m_new)
        pr = jnp.exp(s - m_new)
        l_ref[...] = a * l_ref[...] + jnp.sum(pr, axis=-1, keepdims=True)
        acc_ref[...] = a * acc_ref[...] + _dot(pr.astype(BF16), v16)
        m_ref[...] = m_new

    @pl.when(p == 0)
    def _():
        q = q_ref[0]
        for h in range(kvh):
            qbd_ref[:, h * hd:(h + 1) * hd] = jnp.where(row_kvh == h, q, jnp.zeros_like(q))
        m_ref[...] = jnp.full(m_ref.shape, NEG_INF, F32)
        l_ref[...] = jnp.zeros(l_ref.shape, F32)
        acc_ref[...] = jnp.zeros(acc_ref.shape, F32)
        carry_ref[...] = jnp.zeros(carry_ref.shape, F32)
        pad = jnp.zeros((psz - kn_ref.shape[1], kn_ref.shape[2]), F32)
        k16 = jnp.concatenate([kn_ref[0], pad], axis=0).astype(BF16)
        v16 = jnp.concatenate([vn_ref[0], pad], axis=0).astype(BF16)
        t = lax.broadcasted_iota(jnp.int32, (rows, psz), 0) // heads
        j = lax.broadcasted_iota(jnp.int32, (rows, psz), 1)
        page_update(k16, v16, -cnt_ref[0], j <= t)

    def flat16(ref):
        rows2d = ref.reshape(psz * kvh, hd)
        return jnp.concatenate([rows2d[pl.ds(h, psz, stride=kvh), :] for h in range(kvh)],
                               axis=1).astype(BF16)

    @pl.when(p > 0)
    def _():
        r = lax.broadcasted_iota(jnp.int32, (psz, 2 * psz), 0)
        c = lax.broadcasted_iota(jnp.int32, (psz, 2 * psz), 1)
        later_and_total = ((r > c) | (c >= psz)).astype(F32)
        carry = carry_ref[...]
        parts = []
        for i in range(pages):
            w = jnp.dot(lf_refs[i][...], later_and_total, precision=lax.Precision.HIGHEST,
                        preferred_element_type=F32)
            parts.append(w[:, :psz] + carry)
            carry = carry + w[:, psz:psz + 1]
        carry_ref[...] = carry
        k16 = jnp.concatenate([flat16(k_refs[i]) for i in range(pages)], axis=0)
        v16 = jnp.concatenate([flat16(v_refs[i]) for i in range(pages)], axis=0)
        page_update(k16, v16, jnp.concatenate(parts, axis=1), None)

    @pl.when(p == pl.num_programs(1) - 1)
    def _():
        out = jnp.zeros((rows, hd), F32)
        for h in range(kvh):
            out = out + jnp.where(row_kvh == h, acc_ref[:, h * hd:(h + 1) * hd], 0.0)
        o_ref[0] = (out / l_ref[...]).astype(BF16)


def fox_sample(page_table, q16, c_col, c_new_t, k_new, v_new, pool_k, pool_v, pool_lf, layer, steps):
    bd, rows, hd = q16.shape
    heads = rows // steps
    psz, kvh = pool_k.shape[2], pool_k.shape[3]
    kw = kvh * hd
    n_pages = page_table.shape[1]
    pages = _pick(n_pages, 8, 4, 2, 1)
    n_steps = n_pages // pages + 1

    def page_idx(i, tail):
        return lambda b, p, pt: (layer, pt[b, n_pages - 1 - (jnp.maximum(p - 1, 0) * pages + i)]) + tail

    per_b = lambda a: pl.BlockSpec((1,) + a.shape[1:], lambda b, p, pt: (b,) + (0,) * (a.ndim - 1))
    kv_specs = lambda: [pl.BlockSpec((None, None, psz, kvh, hd), page_idx(i, (0, 0, 0))) for i in range(pages)]
    lf_specs = [pl.BlockSpec((None, None, heads, psz), page_idx(i, (0, 0))) for i in range(pages)]
    grid_spec = pltpu.PrefetchScalarGridSpec(
        num_scalar_prefetch=1,
        grid=(bd, n_steps),
        in_specs=[per_b(q16), per_b(c_col), per_b(c_new_t), per_b(k_new), per_b(v_new)]
        + kv_specs() + kv_specs() + lf_specs,
        out_specs=pl.BlockSpec((1, rows, hd), lambda b, p, pt: (b, 0, 0)),
        scratch_shapes=[pltpu.VMEM((rows, kw), BF16), pltpu.VMEM((rows, 1), F32),
                        pltpu.VMEM((rows, 1), F32), pltpu.VMEM((rows, kw), F32),
                        pltpu.VMEM((heads, 1), F32)],
    )
    return pl.pallas_call(
        functools.partial(_fox_s_kernel, pages=pages, steps=steps, grp=heads // FOX_KV_HEADS,
                          scale=FOX_HEAD_DIM ** -0.5),
        grid_spec=grid_spec,
        out_shape=jax.ShapeDtypeStruct((bd, rows, hd), BF16),
        compiler_params=_params(("arbitrary", "arbitrary")),
        name="fox_sample",
    )(page_table, q16, c_col, c_new_t, k_new, v_new,
      *([pool_k] * pages), *([pool_v] * pages), *([pool_lf] * pages))


def _time_major(a):
    return jnp.swapaxes(a, 0, 1).reshape((a.shape[0] * a.shape[1],) + a.shape[2:])


def _batch_major(a, bd):
    return jnp.swapaxes(a.reshape((a.shape[0] // bd, bd) + a.shape[1:]), 0, 1)


def _pad_rows(a, rows):
    return jnp.pad(a, ((0, 0), (0, rows - a.shape[1]), (0, 0)))


def kernel(x_prompt, x_sample, state_conv, cache_win_k, cache_win_v, cache_k, cache_v, cache_logf,
           state_ffn, page_table, w_in_a, conv_a, w_out_a, w_qkv_b, sinks_b, w_o_b, w_qkvf_c, b_f_c,
           w_o_c, ln1_g, ln1_b, w_up, conv_f, w_down, ln2_g, ln2_b):
    batch, seq, d = x_prompt.shape
    bd, steps, _ = x_sample.shape
    depth = ln1_g.shape[0]
    alpha = (2 * depth) ** 0.25
    n_mixers = 3
    f = w_down.shape[1]

    xp = x_prompt.reshape(batch * seq, d)
    xp16 = xp.astype(BF16)
    xs = _time_major(x_sample)
    xs16 = xs.astype(BF16)

    up16 = None
    conv_p, conv_s, wk_p, wv_p, wk_s, wv_s = [], [], [], [], [], []
    fk_p, fv_p, fl_p, fk_s, fv_s, fl_s = [], [], [], [], [], []
    ffn_p, ffn_s = [], []

    for i in range(depth):
        mix, j = i % n_mixers, i // n_mixers
        if mix == 0:
            zs, st_s, wb16, wc16, wh16 = gate_conv_sample(xs16, w_in_a, j, conv_a[j],
                                                          _time_major(state_conv[j]), bd)
            ys, wo16 = mm_sample(zs, w_out_a, j, d, F32, res=xs, alpha=alpha)
            zp, st_p = gate_conv_prompt(xp16, wb16, wc16, wh16, conv_a[j], batch)
            yp = mm_prompt(zp, wo16, d, 0, F32, res=xp, alpha=alpha)
            conv_p.append(st_p)
            conv_s.append(_batch_major(st_s, bd))
        elif mix == 1:
            hd, kvh = SW_HEAD_DIM, SW_KV_HEADS
            kw = kvh * hd
            grp = d // kw
            qkv_s, w16 = mm_sample(xs16, w_qkv_b, j, d + 2 * kw, F32)
            qp = mm_prompt(xp16, w16, d, 0, BF16)
            kp = mm_prompt(xp16, w16, kw, d, F32)
            vp = mm_prompt(xp16, w16, kw, d + kw, F32)
            op = swa_prompt(qp, kp, vp, sinks_b[j], batch)
            qs = qkv_s[:, :d].astype(BF16).reshape(steps, bd, kvh, grp, hd)
            qs = qs.transpose(1, 2, 3, 0, 4).reshape(bd, kvh, grp * steps, hd)
            kn = _batch_major(qkv_s[:, d:d + kw], bd)
            vn = _batch_major(qkv_s[:, d + kw:], bd)
            sink_col = jnp.repeat(sinks_b[j].reshape(kvh, grp), steps, axis=1)[..., None]
            os_ = swa_sample(qs, cache_win_k, cache_win_v, j, _pad_rows(kn, 8), _pad_rows(vn, 8),
                             sink_col, steps)
            os_ = os_.reshape(bd, kvh, grp, steps, hd).transpose(3, 0, 1, 2, 4).reshape(steps * bd, d)
            ys, wo16 = mm_sample(os_.astype(BF16), w_o_b, j, d, F32, res=xs, alpha=alpha)
            yp = mm_prompt(op, wo16, d, 0, F32, res=xp, alpha=alpha)
            keep = min(WINDOW, seq)
            wk_p.append(kp.reshape(batch, seq, kw)[:, seq - keep:].reshape(batch, keep, kvh, hd))
            wv_p.append(vp.reshape(batch, seq, kw)[:, seq - keep:].reshape(batch, keep, kvh, hd))
            wb = cache_win_k.shape[2]
            wk_s.append(jnp.concatenate([cache_win_k[j], kn.reshape(bd, steps, kvh, hd)], axis=1)[:, -wb:])
            wv_s.append(jnp.concatenate([cache_win_v[j], vn.reshape(bd, steps, kvh, hd)], axis=1)[:, -wb:])
        else:
            hd, kvh = FOX_HEAD_DIM, FOX_KV_HEADS
            kw = kvh * hd
            heads = d // hd
            grp = heads // kvh
            w_t = jnp.swapaxes(w_qkvf_c, 1, 2)
            wf = jnp.swapaxes(w_t[j, d + 2 * kw:, :], 0, 1)
            qkv_s, w16 = mm_sample(xs16, w_t, j, d + 2 * kw, F32, w_transposed=True)
            lf_s, c_s = fox_logf_sample(xs16, wf, b_f_c[j], bd)
            qp = mm_prompt(xp16, w16, d, 0, BF16)
            kp = mm_prompt(xp16, w16, kw, d, F32, head_dim=hd)
            vp = mm_prompt(xp16, w16, kw, d + kw, F32, head_dim=hd)
            lf_p, c_p, ct_p = fox_logf_prompt(xp16, wf, b_f_c[j], batch)
            tk = _pick(seq, 512, 256, 128)
            c_k = ct_p.reshape(batch, kvh, grp, seq // tk, tk).transpose(0, 1, 3, 2, 4)
            op = fox_prompt(qp, kp, vp, c_p, c_k, batch)
            qs = _batch_major(qkv_s[:, :d].astype(BF16), bd).reshape(bd, steps * heads, hd)
            kn = _batch_major(qkv_s[:, d:d + kw], bd)
            vn = _batch_major(qkv_s[:, d + kw:], bd)
            c_b = _batch_major(c_s, bd)
            c_col = c_b.reshape(bd, steps * heads, 1)
            c_new_t = jnp.pad(c_b.transpose(0, 2, 1), ((0, 0), (0, 0), (0, PAGE_SIZE - steps)))
            os_ = fox_sample(page_table, qs, c_col, c_new_t, _pad_rows(kn, 8), _pad_rows(vn, 8),
                             cache_k, cache_v, jnp.swapaxes(cache_logf, 2, 3), j, steps)
            os_ = _time_major(os_.reshape(bd, steps, d))
            ys, wo16 = mm_sample(os_, w_o_c, j, d, F32, res=xs, alpha=alpha)
            yp = mm_prompt(op, wo16, d, 0, F32, res=xp, alpha=alpha)
            fk_p.append(kp.reshape(batch, seq, kvh, hd))
            fv_p.append(vp.reshape(batch, seq, kvh, hd))
            fl_p.append(lf_p.reshape(batch, seq, heads))
            fk_s.append(kn.reshape(bd, steps, kvh, hd))
            fv_s.append(vn.reshape(bd, steps, kvh, hd))
            fl_s.append(_batch_major(lf_s, bd))
        xs, xs16 = layer_norm_rows(ys, ln1_g[i], ln1_b[i])
        xp, xp16 = layer_norm_rows(yp, ln1_g[i], ln1_b[i])

        prev_ffn = _time_major(state_ffn[i])
        if up16 is None:
            a_s, sg_s, su_s, wg16, wu16 = ffn_up_sample(xs16, w_up, i, conv_f[i], prev_ffn, bd)
            u_off = 0
        else:
            a_s, sg_s, su_s = ffn_up_sample(xs16, up16, None, conv_f[i], prev_ffn, bd)
            wg16 = wu16 = up16
            u_off = f // _pick(f, 256, 128)
        a_p, sg_p, su_p, wd16, *nxt = ffn_up_prompt(xp16, wg16, wu16, u_off, conv_f[i], batch,
                                                   w_down, w_up, i)
        up16 = nxt[0] if nxt else None
        ys = mm_prompt(a_s, wd16, d, 0, F32, res=xs, alpha=alpha)
        yp = mm_prompt(a_p, wd16, d, 0, F32, res=xp, alpha=alpha)
        ffn_p.append(jnp.concatenate([sg_p, su_p], axis=-1))
        ffn_s.append(_batch_major(jnp.concatenate([sg_s, su_s], axis=-1), bd))
        xs, xs16 = layer_norm_rows(ys, ln2_g[i], ln2_b[i])
        xp, xp16 = layer_norm_rows(yp, ln2_g[i], ln2_b[i])

    return (xp.reshape(batch, seq, d), _batch_major(xs, bd),
            jnp.stack(conv_p), jnp.stack(conv_s), jnp.stack(wk_p), jnp.stack(wv_p),
            jnp.stack(wk_s), jnp.stack(wv_s), jnp.stack(fk_p), jnp.stack(fv_p), jnp.stack(fl_p),
            jnp.stack(fk_s), jnp.stack(fv_s), jnp.stack(fl_s), jnp.stack(ffn_p), jnp.stack(ffn_s))
```

```python
import functools
import math

import jax
import jax.numpy as jnp
from jax import lax
from jax.experimental import pallas as pl
from jax.experimental.pallas import tpu as pltpu

F32 = jnp.float32
BF16 = jnp.bfloat16

CONV_TAPS = 3
CONV_PREV = CONV_TAPS - 1
CONV_HALO = 8
SW_HEAD_DIM = 64
SW_KV_HEADS = 8
WINDOW = 128
FOX_HEAD_DIM = 128
FOX_KV_HEADS = 8
PAGE_SIZE = 128
LN_EPS = 1e-5
NEG_INF = -1e30
LOG2_E = math.log2(math.e)

V7X_VMEM_BYTES = 64 * 1024 * 1024
VMEM_LIMIT = V7X_VMEM_BYTES - 8 * 1024 * 1024
LANE = 128
BF16_SUBLANES = 16


def _params(semantics):
    return pltpu.CompilerParams(dimension_semantics=semantics, vmem_limit_bytes=VMEM_LIMIT)


def _dot(a, b):
    return jnp.dot(a, b, preferred_element_type=F32)


def _dot_nt(a, b):
    return lax.dot_general(a, b, (((1,), (1,)), ((), ())), preferred_element_type=F32)


def _split_scale(scale):
    pre = 2.0 ** math.floor(math.log2(scale))
    return pre, scale / pre


def _pick(n, *cands):
    for c in cands:
        if n % c == 0:
            return c
    raise ValueError(f"no tile for {n} among {cands}")


def _conv_rows(u, prev, cw, shift):
    rows = u.shape[0]
    full = jnp.concatenate([prev, u], axis=0)
    y = cw[0:1, :] * full[0:rows] + cw[1:2, :] * full[shift:shift + rows] + cw[2:3, :] * u
    return y, full[rows:rows + CONV_PREV * shift]


def _conv_block(u, halo, cw):
    full = jnp.concatenate([halo, u], axis=0)
    y = cw[0:1, :] * pltpu.roll(full, 2, 0) + cw[1:2, :] * pltpu.roll(full, 1, 0) + cw[2:3, :] * full
    return y[CONV_HALO:]


def _gate_conv_core(x, wb, wc, wh, cw, prev, shift):
    bg = _dot(x, wb)
    c = _dot(x, wc)
    h = _dot(x, wh)
    y, tail = _conv_rows(c * h, prev, cw, shift)
    return (bg * y).astype(BF16), tail


def _gate_conv_p_kernel(x_ref, wb_ref, wc_ref, wh_ref, cw_ref, z_ref, st_ref, *, rb):
    wb, wc, wh, cw = wb_ref[...], wc_ref[...], wh_ref[...], cw_ref[...]
    halo = jnp.zeros((CONV_HALO, wb.shape[1]), F32)
    for r in range(x_ref.shape[0] // rb):
        x = x_ref[r * rb:(r + 1) * rb, :]
        u = _dot(x, wc) * _dot(x, wh)
        z_ref[r * rb:(r + 1) * rb, :] = (_dot(x, wb) * _conv_block(u, halo, cw)).astype(BF16)
        halo = u[rb - CONV_HALO:rb]
    st_ref[0] = halo[CONV_HALO - CONV_PREV:CONV_HALO]


def _gate_conv_s_kernel(x_ref, wb_ref, wc_ref, wh_ref, cw_ref, prev_ref,
                        z_ref, st_ref, wb16_ref, wc16_ref, wh16_ref, *, shift):
    wb = wb_ref[...].astype(BF16)
    wc = wc_ref[...].astype(BF16)
    wh = wh_ref[...].astype(BF16)
    wb16_ref[...] = wb
    wc16_ref[...] = wc
    wh16_ref[...] = wh
    z, tail = _gate_conv_core(x_ref[...], wb, wc, wh, cw_ref[...], prev_ref[...], shift)
    z_ref[...] = z
    st_ref[...] = tail


def gate_conv_sample(x16, w_in, layer, cw, prev, shift):
    rows, d = x16.shape
    c = w_in.shape[2] // 3
    tn = _pick(c, 256, 128)
    nc = c // tn
    w_spec = lambda sec: pl.BlockSpec((None, d, tn), lambda n, sec=sec: (layer, 0, sec * nc + n))
    col = lambda r: pl.BlockSpec((r, tn), lambda n: (0, n))
    return pl.pallas_call(
        functools.partial(_gate_conv_s_kernel, shift=shift),
        grid=(nc,),
        in_specs=[pl.BlockSpec((rows, d), lambda n: (0, 0)), w_spec(0), w_spec(1), w_spec(2),
                  col(CONV_TAPS), col(CONV_PREV * shift)],
        out_specs=[col(rows), col(CONV_PREV * shift), col(d), col(d), col(d)],
        out_shape=[jax.ShapeDtypeStruct((rows, c), BF16),
                   jax.ShapeDtypeStruct((CONV_PREV * shift, c), F32),
                   jax.ShapeDtypeStruct((d, c), BF16),
                   jax.ShapeDtypeStruct((d, c), BF16),
                   jax.ShapeDtypeStruct((d, c), BF16)],
        compiler_params=_params(("arbitrary",)),
        name="gate_conv_sample",
    )(x16, w_in, w_in, w_in, cw, prev)


def gate_conv_prompt(x16, wb16, wc16, wh16, cw, batch):
    m, d = x16.shape
    s = m // batch
    c = wb16.shape[1]
    tn = _pick(c, 256, 128)
    w_spec = pl.BlockSpec((d, tn), lambda b, n: (0, n))
    return pl.pallas_call(
        functools.partial(_gate_conv_p_kernel, rb=_pick(s, 512, 256, 128)),
        grid=(batch, c // tn),
        in_specs=[pl.BlockSpec((s, d), lambda b, n: (b, 0)),
                  w_spec, w_spec, w_spec,
                  pl.BlockSpec((CONV_TAPS, tn), lambda b, n: (0, n))],
        out_specs=[pl.BlockSpec((s, tn), lambda b, n: (b, n)),
                   pl.BlockSpec((1, CONV_PREV, tn), lambda b, n: (b, 0, n))],
        out_shape=[jax.ShapeDtypeStruct((m, c), BF16),
                   jax.ShapeDtypeStruct((batch, CONV_PREV, c), F32)],
        compiler_params=_params(("arbitrary", "arbitrary")),
        name="gate_conv_prompt",
    )(x16, wb16, wc16, wh16, cw)


def _ffn_up_core(x, wg, wu, cwg, cwu, prev_g, prev_u, shift):
    cg, tail_g = _conv_rows(_dot(x, wg), prev_g, cwg, shift)
    cu, tail_u = _conv_rows(_dot(x, wu), prev_u, cwu, shift)
    return _silu_gate(cg, cu), tail_g, tail_u


def _silu_gate(cg, cu):
    half = 0.5 * cg
    return ((half * jnp.tanh(half) + half) * cu).astype(BF16)


def _ffn_up_p_kernel(*refs, rb, has_next):
    if has_next:
        (x_ref, wg_ref, wu_ref, cwg_ref, cwu_ref, dn_ref, up_ref,
         a_ref, sg_ref, su_ref, dn16_ref, up16_ref) = refs
        up16_ref[...] = up_ref[...].astype(BF16)
    else:
        x_ref, wg_ref, wu_ref, cwg_ref, cwu_ref, dn_ref, a_ref, sg_ref, su_ref, dn16_ref = refs
    dn16_ref[...] = dn_ref[...].astype(BF16)
    wg, wu, cwg, cwu = wg_ref[...], wu_ref[...], cwg_ref[...], cwu_ref[...]
    halo_g = jnp.zeros((CONV_HALO, wg.shape[1]), F32)
    halo_u = halo_g
    for r in range(x_ref.shape[0] // rb):
        x = x_ref[r * rb:(r + 1) * rb, :]
        hg = _dot(x, wg)
        hu = _dot(x, wu)
        a_ref[r * rb:(r + 1) * rb, :] = _silu_gate(_conv_block(hg, halo_g, cwg),
                                                   _conv_block(hu, halo_u, cwu))
        halo_g = hg[rb - CONV_HALO:rb]
        halo_u = hu[rb - CONV_HALO:rb]
    sg_ref[0] = halo_g[CONV_HALO - CONV_PREV:CONV_HALO]
    su_ref[0] = halo_u[CONV_HALO - CONV_PREV:CONV_HALO]


def _ffn_up_s_kernel(x_ref, wg_ref, wu_ref, cwg_ref, cwu_ref, pg_ref, pu_ref,
                     a_ref, sg_ref, su_ref, *w16_refs, shift):
    wg = wg_ref[...].astype(BF16)
    wu = wu_ref[...].astype(BF16)
    if w16_refs:
        w16_refs[0][...] = wg
        w16_refs[1][...] = wu
    act, tg, tu = _ffn_up_core(x_ref[...], wg, wu, cwg_ref[...], cwu_ref[...],
                               pg_ref[...], pu_ref[...], shift)
    a_ref[...] = act
    sg_ref[...] = tg
    su_ref[...] = tu


def ffn_up_sample(x16, w_up, layer, cw, prev, shift):
    rows, d = x16.shape
    emit = layer is not None
    f = w_up.shape[-1] // 2
    tn = _pick(f, 256, 128)
    nf = f // tn
    sec = lambda r, k: pl.BlockSpec((r, tn), lambda n, k=k: (0, k * nf + n))
    if emit:
        wsec = lambda k: pl.BlockSpec((None, d, tn), lambda n, k=k: (layer, 0, k * nf + n))
    else:
        wsec = lambda k: sec(d, k)
    col = lambda r: pl.BlockSpec((r, tn), lambda n: (0, n))
    pr = CONV_PREV * shift
    w16_shape = jax.ShapeDtypeStruct((d, f), BF16)
    return pl.pallas_call(
        functools.partial(_ffn_up_s_kernel, shift=shift),
        grid=(nf,),
        in_specs=[pl.BlockSpec((rows, d), lambda n: (0, 0)), wsec(0), wsec(1),
                  sec(CONV_TAPS, 0), sec(CONV_TAPS, 1), sec(pr, 0), sec(pr, 1)],
        out_specs=[col(rows), col(pr), col(pr)] + ([col(d), col(d)] if emit else []),
        out_shape=[jax.ShapeDtypeStruct((rows, f), BF16),
                   jax.ShapeDtypeStruct((pr, f), F32),
                   jax.ShapeDtypeStruct((pr, f), F32)] + ([w16_shape, w16_shape] if emit else []),
        compiler_params=_params(("arbitrary",)),
        name="ffn_up_sample",
    )(x16, w_up, w_up, cw, cw, prev, prev)


def ffn_up_prompt(x16, wg16, wu16, u_off, cw, batch, w_down, w_up, layer):
    m, d = x16.shape
    s = m // batch
    f = w_down.shape[1]
    tn = _pick(f, 256, 128)
    nf = f // tn
    steps = batch * nf
    has_next = layer + 1 < w_up.shape[0]
    dn_rows, up_cols = f // steps, 2 * f // steps
    assert dn_rows * steps == f and dn_rows % BF16_SUBLANES == 0 and up_cols % LANE == 0
    step = lambda b, n: b * nf + n
    st_spec = pl.BlockSpec((1, CONV_PREV, tn), lambda b, n: (b, 0, n))
    in_specs = [pl.BlockSpec((s, d), lambda b, n: (b, 0)),
                pl.BlockSpec((d, tn), lambda b, n: (0, n)),
                pl.BlockSpec((d, tn), lambda b, n: (0, u_off + n)),
                pl.BlockSpec((CONV_TAPS, tn), lambda b, n: (0, n)),
                pl.BlockSpec((CONV_TAPS, tn), lambda b, n: (0, nf + n)),
                pl.BlockSpec((None, dn_rows, d), lambda b, n: (layer, step(b, n), 0))]
    out_specs = [pl.BlockSpec((s, tn), lambda b, n: (b, n)), st_spec, st_spec,
                 pl.BlockSpec((dn_rows, d), lambda b, n: (step(b, n), 0))]
    out_shape = [jax.ShapeDtypeStruct((m, f), BF16),
                 jax.ShapeDtypeStruct((batch, CONV_PREV, f), F32),
                 jax.ShapeDtypeStruct((batch, CONV_PREV, f), F32),
                 jax.ShapeDtypeStruct((f, d), BF16)]
    args = [x16, wg16, wu16, cw, cw, w_down]
    if has_next:
        in_specs.append(pl.BlockSpec((None, d, up_cols), lambda b, n: (layer + 1, 0, step(b, n))))
        out_specs.append(pl.BlockSpec((d, up_cols), lambda b, n: (0, step(b, n))))
        out_shape.append(jax.ShapeDtypeStruct((d, 2 * f), BF16))
        args.append(w_up)
    return pl.pallas_call(
        functools.partial(_ffn_up_p_kernel, rb=_pick(s, 512, 256, 128), has_next=has_next),
        grid=(batch, nf),
        in_specs=in_specs,
        out_specs=out_specs,
        out_shape=out_shape,
        compiler_params=_params(("arbitrary", "arbitrary")),
        name="ffn_up_prompt",
    )(*args)


def _mm_p_kernel(*refs, alpha, has_res):
    if has_res:
        x_ref, w_ref, r_ref, o_ref = refs
    else:
        x_ref, w_ref, o_ref = refs
    acc = _dot(x_ref[...], w_ref[...])
    if has_res:
        acc = alpha * r_ref[...] + acc
    o_ref[...] = acc.astype(o_ref.dtype)


def _mm_s_kernel(*refs, alpha, has_res):
    if has_res:
        x_ref, w_ref, r_ref, o_ref, w16_ref = refs
    else:
        x_ref, w_ref, o_ref, w16_ref = refs
    w = w_ref[...].astype(BF16)
    w16_ref[...] = w
    acc = _dot(x_ref[...], w)
    if has_res:
        acc = alpha * r_ref[...] + acc
    o_ref[...] = acc.astype(o_ref.dtype)


def mm_sample(x16, w, layer, n_out, out_dtype, res=None, alpha=None):
    rows, k = x16.shape
    tn = _pick(n_out, 512, 256, 128) if k <= 4096 else _pick(n_out, 256, 128)
    col = lambda r: pl.BlockSpec((r, tn), lambda n: (0, n))
    in_specs = [pl.BlockSpec((rows, k), lambda n: (0, 0)),
                pl.BlockSpec((None, k, tn), lambda n: (layer, 0, n))]
    args = [x16, w]
    if res is not None:
        in_specs.append(col(rows))
        args.append(res)
    return pl.pallas_call(
        functools.partial(_mm_s_kernel, alpha=alpha, has_res=res is not None),
        grid=(n_out // tn,),
        in_specs=in_specs,
        out_specs=[col(rows), col(k)],
        out_shape=[jax.ShapeDtypeStruct((rows, n_out), out_dtype),
                   jax.ShapeDtypeStruct((k, n_out), BF16)],
        compiler_params=_params(("arbitrary",)),
        name="mm_sample",
    )(*args)


def mm_prompt(x16, w16, n_out, col_off, out_dtype, res=None, alpha=None):
    m, k = x16.shape
    if k <= 4096:
        tm, tn = _pick(m, 1024, 512, 256, 128), _pick(n_out, 1024, 512, 256, 128)
    else:
        tm, tn = _pick(m, 512, 256, 128), _pick(n_out, 512, 256, 128)
    assert col_off % tn == 0
    off = col_off // tn
    in_specs = [pl.BlockSpec((tm, k), lambda i, n: (i, 0)),
                pl.BlockSpec((k, tn), lambda i, n: (0, off + n))]
    args = [x16, w16]
    if res is not None:
        in_specs.append(pl.BlockSpec((tm, tn), lambda i, n: (i, n)))
        args.append(res)
    return pl.pallas_call(
        functools.partial(_mm_p_kernel, alpha=alpha, has_res=res is not None),
        grid=(m // tm, n_out // tn),
        in_specs=in_specs,
        out_specs=pl.BlockSpec((tm, tn), lambda i, n: (i, n)),
        out_shape=jax.ShapeDtypeStruct((m, n_out), out_dtype),
        compiler_params=_params(("arbitrary", "arbitrary")),
        name="mm_prompt",
    )(*args)


def _ln_kernel(y_ref, g_ref, b_ref, xf_ref, xb_ref):
    y = y_ref[...]
    mu = jnp.mean(y, axis=-1, keepdims=True)
    dev = y - mu
    var = jnp.mean(dev * dev, axis=-1, keepdims=True)
    out = dev * lax.rsqrt(var + LN_EPS) * g_ref[...] + b_ref[...]
    xf_ref[...] = out
    xb_ref[...] = out.astype(BF16)


def layer_norm_rows(y, g, b):
    m, d = y.shape
    tr = _pick(m, 512, 256, 128)
    row = pl.BlockSpec((tr, d), lambda i: (i, 0))
    vec = pl.BlockSpec((1, d), lambda i: (0, 0))
    return pl.pallas_call(
        _ln_kernel,
        grid=(m // tr,),
        in_specs=[row, vec, vec],
        out_specs=[row, row],
        out_shape=[jax.ShapeDtypeStruct((m, d), F32), jax.ShapeDtypeStruct((m, d), BF16)],
        compiler_params=_params(("arbitrary",)),
        name="layer_norm",
    )(y, g.reshape(1, d), b.reshape(1, d))


def _sink_softmax_pv(s, sink, v16):
    m = jnp.maximum(jnp.max(s, axis=-1, keepdims=True), sink)
    p = jnp.exp(s - m)
    den = jnp.sum(p, axis=-1, keepdims=True) + jnp.exp(sink - m)
    return _dot((p / den).astype(BF16), v16)


def _swa_p_kernel(q_ref, kp_ref, ko_ref, vp_ref, vo_ref, sink_ref, o_ref, *, grp, pre, rest):
    blk = q_ref.shape[0]
    rows = grp * blk
    first = pl.program_id(1) == 0
    kcat = jnp.concatenate([kp_ref[...], ko_ref[...]], axis=0).astype(BF16)
    vcat = jnp.concatenate([vp_ref[...], vo_ref[...]], axis=0).astype(BF16)
    tl = lax.broadcasted_iota(jnp.int32, (rows, 2 * blk), 0) % blk
    j = lax.broadcasted_iota(jnp.int32, (rows, 2 * blk), 1)
    mask = (j >= tl) & (j <= tl + WINDOW) & ((j >= blk) | jnp.logical_not(first))
    hd = SW_HEAD_DIM
    unit = (lax.broadcasted_iota(jnp.int32, (2 * blk, hd), 1) == 0).astype(BF16)
    for h in range(SW_KV_HEADS):
        heads = range(h * grp, (h + 1) * grp)
        q = jnp.concatenate([q_ref[:, a * hd:(a + 1) * hd] for a in heads], axis=0) * pre
        sink = jnp.concatenate([jnp.full((blk, 1), sink_ref[a], F32) for a in heads], axis=0)
        s = _dot_nt(q, kcat[:, h * hd:(h + 1) * hd])
        if rest != 1.0:
            s = s * rest
        s = jnp.where(mask, s, NEG_INF)
        m = jnp.maximum(jnp.max(s, axis=-1, keepdims=True), sink)
        vh = jnp.concatenate([vcat[:, h * hd:(h + 1) * hd], unit], axis=1)
        ov = _dot(jnp.exp(s - m).astype(BF16), vh)
        den = ov[:, hd:hd + 1] + jnp.exp(sink - m)
        o = (ov[:, :hd] / den).astype(BF16)
        for g, a in enumerate(heads):
            o_ref[:, a * hd:(a + 1) * hd] = o[g * blk:(g + 1) * blk]


def swa_prompt(q16, k, v, sinks, batch):
    m, qw = q16.shape
    s = m // batch
    blk = WINDOW
    nb = s // blk
    kw = SW_KV_HEADS * SW_HEAD_DIM
    grp = qw // kw
    own = pl.BlockSpec((blk, kw), lambda b, i: (b * nb + i, 0))
    prev = pl.BlockSpec((blk, kw), lambda b, i: (b * nb + jnp.maximum(i - 1, 0), 0))
    pre, rest = _split_scale(SW_HEAD_DIM ** -0.5)
    return pl.pallas_call(
        functools.partial(_swa_p_kernel, grp=grp, pre=pre, rest=rest),
        grid=(batch, nb),
        in_specs=[pl.BlockSpec((blk, qw), lambda b, i: (b * nb + i, 0)),
                  prev, own, prev, own,
                  pl.BlockSpec(memory_space=pltpu.SMEM)],
        out_specs=pl.BlockSpec((blk, qw), lambda b, i: (b * nb + i, 0)),
        out_shape=jax.ShapeDtypeStruct((m, qw), BF16),
        compiler_params=_params(("arbitrary", "arbitrary")),
        name="swa_prompt",
    )(q16, k, k, v, v, sinks)


def _swa_s_kernel(q_ref, ck_ref, cv_ref, kn_ref, vn_ref, sink_ref, o_ref, *, steps, scale):
    wb, kvh, hd = ck_ref.shape
    pad = jnp.zeros((wb - kn_ref.shape[1], hd), F32)
    rows = q_ref.shape[2]
    t = lax.broadcasted_iota(jnp.int32, (rows, 2 * wb), 0) % steps
    j = lax.broadcasted_iota(jnp.int32, (rows, 2 * wb), 1)
    mask = (j >= t + wb - WINDOW) & (j <= t + wb)

    def keys(cache_ref, new_ref, h):
        cached = cache_ref.reshape(wb * kvh, hd)[pl.ds(h, wb, stride=kvh), :]
        return jnp.concatenate([cached, new_ref[0, :, h * hd:(h + 1) * hd], pad], axis=0).astype(BF16)

    for h in range(kvh):
        s = _dot_nt(q_ref[0, h], keys(ck_ref, kn_ref, h)) * scale
        s = jnp.where(mask, s, NEG_INF)
        o_ref[0, h] = _sink_softmax_pv(s, sink_ref[h], keys(cv_ref, vn_ref, h))


def swa_sample(q16, cache_k, cache_v, layer, k_new, v_new, sink_col, steps):
    bd, kvh, rows, hd = q16.shape
    wb = cache_k.shape[2]
    per_b = lambda a: pl.BlockSpec((1,) + a.shape[1:], lambda b: (b,) + (0,) * (a.ndim - 1))
    cache_spec = pl.BlockSpec((None, None, wb, kvh, hd), lambda b: (layer, b, 0, 0, 0))
    return pl.pallas_call(
        functools.partial(_swa_s_kernel, steps=steps, scale=SW_HEAD_DIM ** -0.5),
        grid=(bd,),
        in_specs=[per_b(q16), cache_spec, cache_spec, per_b(k_new), per_b(v_new),
                  pl.BlockSpec(sink_col.shape, lambda b: (0, 0, 0))],
        out_specs=pl.BlockSpec((1, kvh, rows, hd), lambda b: (b, 0, 0, 0)),
        out_shape=jax.ShapeDtypeStruct((bd, kvh, rows, hd), F32),
        compiler_params=_params(("arbitrary",)),
        name="swa_sample",
    )(q16, cache_k, cache_v, k_new, v_new, sink_col)


def _log_sigmoid(z):
    return jnp.minimum(z, 0.0) - jnp.log1p(jnp.exp(-jnp.abs(z)))


def _fox_logf_p_kernel(x_ref, wf_ref, bf_ref, lf_ref, c_ref, ct_ref, *, blk):
    lf = _log_sigmoid(_dot(x_ref[...], wf_ref[...].astype(BF16)) + bf_ref[...])
    lf_ref[...] = lf
    r = lax.broadcasted_iota(jnp.int32, (blk, blk), 0)
    c = lax.broadcasted_iota(jnp.int32, (blk, blk), 1)
    upto_rows = (c <= r).astype(F32)
    upto_cols = (r <= c).astype(F32)
    carry = jnp.zeros((1, lf.shape[1]), F32)
    carry_t = jnp.zeros((lf.shape[1], 1), F32)
    for i in range(lf.shape[0] // blk):
        part = lf[i * blk:(i + 1) * blk]
        cb = jnp.dot(upto_rows, part, precision=lax.Precision.HIGHEST,
                     preferred_element_type=F32) + carry
        c_ref[i * blk:(i + 1) * blk, :] = cb
        carry = cb[blk - 1:blk, :]
        cbt = lax.dot_general(part, upto_cols, (((0,), (0,)), ((), ())),
                              precision=lax.Precision.HIGHEST, preferred_element_type=F32) + carry_t
        ct_ref[0, :, i * blk:(i + 1) * blk] = cbt
        carry_t = cbt[:, blk - 1:blk]


def fox_logf_prompt(x16, wf, bf, batch):
    m, d = x16.shape
    s = m // batch
    h = wf.shape[1]
    row = pl.BlockSpec((s, h), lambda b: (b, 0))
    return pl.pallas_call(
        functools.partial(_fox_logf_p_kernel, blk=_pick(s, 256, 128)),
        grid=(batch,),
        in_specs=[pl.BlockSpec((s, d), lambda b: (b, 0)),
                  pl.BlockSpec((d, h), lambda b: (0, 0)),
                  pl.BlockSpec((1, h), lambda b: (0, 0))],
        out_specs=[row, row, pl.BlockSpec((1, h, s), lambda b: (b, 0, 0))],
        out_shape=[jax.ShapeDtypeStruct((m, h), F32), jax.ShapeDtypeStruct((m, h), F32),
                   jax.ShapeDtypeStruct((batch, h, s), F32)],
        compiler_params=_params(("arbitrary",)),
        name="fox_logf_prompt",
    )(x16, wf, bf.reshape(1, h))


def _fox_logf_s_kernel(x_ref, wf_ref, bf_ref, lf_ref, c_ref, *, shift):
    lf = _log_sigmoid(_dot(x_ref[...], wf_ref[...].astype(BF16)) + bf_ref[...])
    lf_ref[...] = lf
    run = lf[0:shift]
    c_ref[0:shift, :] = run
    for t in range(1, lf.shape[0] // shift):
        run = run + lf[t * shift:(t + 1) * shift]
        c_ref[t * shift:(t + 1) * shift, :] = run


def fox_logf_sample(x16, wf, bf, shift):
    rows, d = x16.shape
    h = wf.shape[1]
    full = lambda a, b: pl.BlockSpec((a, b), lambda i: (0, 0))
    return pl.pallas_call(
        functools.partial(_fox_logf_s_kernel, shift=shift),
        grid=(1,),
        in_specs=[full(rows, d), full(d, h), full(1, h)],
        out_specs=[full(rows, h), full(rows, h)],
        out_shape=[jax.ShapeDtypeStruct((rows, h), F32), jax.ShapeDtypeStruct((rows, h), F32)],
        compiler_params=_params(("arbitrary",)),
        name="fox_logf_sample",
    )(x16, wf, bf.reshape(1, h))


def _fox_p_kernel(q_ref, k_ref, v_ref, c_ref, ck_ref, o_ref, k16_ref, v16_ref, cq_ref, m_ref, acc_ref,
                  *, grp, pre, rest):
    h = pl.program_id(1)
    qi = pl.program_id(2)
    tq = q_ref.shape[0]
    tk = tq
    hd = FOX_HEAD_DIM
    rows = grp * tq
    c_exp = rest * LOG2_E

    @pl.when(qi == 0)
    def _():
        unit = (lax.broadcasted_iota(jnp.int32, v_ref.shape, 1) == 0).astype(BF16)
        k16_ref[...] = k_ref[...].astype(BF16)
        v16_ref[...] = jnp.concatenate([v_ref[...].astype(BF16), unit], axis=1)

    q = jnp.concatenate([q_ref[:, g * hd:(g + 1) * hd] for g in range(grp)], axis=0) * pre
    c_blk = c_ref[...]
    lane = lax.broadcasted_iota(jnp.int32, c_blk.shape, 1)
    for g in range(grp):
        col = jnp.sum(jnp.where(lane == h * grp + g, c_blk, 0.0), axis=-1, keepdims=True)
        cq_ref[g * tq:(g + 1) * tq, :] = jnp.broadcast_to(col * (1.0 / rest), (tq, hd))
    m_ref[...] = jnp.full(m_ref.shape, NEG_INF, F32)
    acc_ref[...] = jnp.zeros(acc_ref.shape, F32)

    def update(j, diagonal):
        off = pl.multiple_of(j * tk, tk)
        z = _dot_nt(q, k16_ref[pl.ds(off, tk), :])
        cq = cq_ref[...]
        t = jnp.concatenate([z[:, i * hd:(i + 1) * hd] + cq for i in range(tk // hd)], axis=1)
        ck = ck_ref[0, 0, j] * (1.0 / rest)
        t = (t.reshape(grp, tq, tk) - ck[:, None, :]).reshape(rows, tk)
        if diagonal:
            r = lax.broadcasted_iota(jnp.int32, (rows, tk), 0) % tq
            c = lax.broadcasted_iota(jnp.int32, (rows, tk), 1)
            t = jnp.where(c <= r, t, NEG_INF)
        m_old = m_ref[...]
        m_new = jnp.maximum(m_old, jnp.max(t, axis=-1, keepdims=True))
        a = jnp.exp2((m_old - m_new) * c_exp)
        p = jnp.exp2((t - m_new) * c_exp)
        acc_ref[...] = a * acc_ref[...] + _dot(p.astype(BF16), v16_ref[pl.ds(off, tk), :])
        m_ref[...] = m_new

    def below_diagonal(j, carry):
        update(j, False)
        return carry

    lax.fori_loop(0, qi, below_diagonal, 0)
    update(qi, True)
    acc = acc_ref[...]
    out = acc[:, :hd] / acc[:, hd:hd + 1]
    for g in range(grp):
        o_ref[:, g * hd:(g + 1) * hd] = out[g * tq:(g + 1) * tq].astype(BF16)


def fox_prompt(q16, k, v, c, c_k, batch):
    m, qw = q16.shape
    heads = c.shape[1]
    s = m // batch
    hd, kvh = FOX_HEAD_DIM, FOX_KV_HEADS
    grp = qw // (kvh * hd)
    nk, tq = c_k.shape[2], c_k.shape[4]
    nq = s // tq
    pre, rest = _split_scale(hd ** -0.5)
    kv_spec = pl.BlockSpec((s, hd), lambda b, h, qi: (b, h))
    return pl.pallas_call(
        functools.partial(_fox_p_kernel, grp=grp, pre=pre, rest=rest),
        grid=(batch, kvh, nq),
        in_specs=[pl.BlockSpec((tq, grp * hd), lambda b, h, qi: (b * nq + qi, h)),
                  kv_spec, kv_spec,
                  pl.BlockSpec((tq, heads), lambda b, h, qi: (b * nq + qi, 0)),
                  pl.BlockSpec((1, 1, nk, grp, tq), lambda b, h, qi: (b, h, 0, 0, 0))],
        out_specs=pl.BlockSpec((tq, grp * hd), lambda b, h, qi: (b * nq + qi, h)),
        out_shape=jax.ShapeDtypeStruct((m, qw), BF16),
        scratch_shapes=[pltpu.VMEM((s, hd), BF16), pltpu.VMEM((s, 2 * hd), BF16),
                        pltpu.VMEM((grp * tq, hd), F32), pltpu.VMEM((grp * tq, 1), F32),
                        pltpu.VMEM((grp * tq, 2 * hd), F32)],
        compiler_params=_params(("arbitrary", "arbitrary", "arbitrary")),
        name="fox_prompt",
    )(q16, k, v, c, c_k)


def _fox_s_kernel(pt_ref, q_ref, ccol_ref, cnt_ref, kn_ref, vn_ref, *rest, pages, steps, grp, scale):
    k_refs = rest[0:pages]
    v_refs = rest[pages:2 * pages]
    lf_refs = rest[2 * pages:3 * pages]
    o_ref, qbd_ref, m_ref, l_ref, acc_ref, carry_ref = rest[3 * pages:]
    p = pl.program_id(1)
    rows, hd = q_ref.shape[1], q_ref.shape[2]
    heads = rows // steps
    kvh = heads // grp
    psz = cnt_ref.shape[2]
    row = lax.broadcasted_iota(jnp.int32, (rows, hd), 0)
    row_kvh = (row % heads) // grp

    def page_update(k16, v16, neg_ck, mask):
        s = _dot_nt(qbd_ref[...], k16) * scale
        s = s + (ccol_ref[0] + jnp.concatenate([neg_ck] * steps, axis=0))
        if mask is not None:
            s = jnp.where(mask, s, NEG_INF)
        m_old = m_ref[...]
        m_new = jnp.maximum(m_old, jnp.max(s, axis=-1, keepdims=True))
        a = jnp.exp(m_old - m_new)
        pr = jnp.exp(s - m_new)
        l_ref[...] = a * l_ref[...] + jnp.sum(pr, axis=-1, keepdims=True)
        acc_ref[...] = a * acc_ref[...] + _dot(pr.astype(BF16), v16)
        m_ref[...] = m_new

    @pl.when(p == 0)
    def _():
        q = q_ref[0]
        for h in range(kvh):
            qbd_ref[:, h * hd:(h + 1) * hd] = jnp.where(row_kvh == h, q, jnp.zeros_like(q))
        m_ref[...] = jnp.full(m_ref.shape, NEG_INF, F32)
        l_ref[...] = jnp.zeros(l_ref.shape, F32)
        acc_ref[...] = jnp.zeros(acc_ref.shape, F32)
        carry_ref[...] = jnp.zeros(carry_ref.shape, F32)
        pad = jnp.zeros((psz - kn_ref.shape[1], kn_ref.shape[2]), F32)
        k16 = jnp.concatenate([kn_ref[0], pad], axis=0).astype(BF16)
        v16 = jnp.concatenate([vn_ref[0], pad], axis=0).astype(BF16)
        t = lax.broadcasted_iota(jnp.int32, (rows, psz), 0) // heads
        j = lax.broadcasted_iota(jnp.int32, (rows, psz), 1)
        page_update(k16, v16, -cnt_ref[0], j <= t)

    def flat16(ref):
        rows2d = ref.reshape(psz * kvh, hd)
        return jnp.concatenate([rows2d[pl.ds(h, psz, stride=kvh), :] for h in range(kvh)],
                               axis=1).astype(BF16)

    @pl.when(p > 0)
    def _():
        r = lax.broadcasted_iota(jnp.int32, (psz, 2 * psz), 0)
        c = lax.broadcasted_iota(jnp.int32, (psz, 2 * psz), 1)
        later_and_total = ((r > c) | (c >= psz)).astype(F32)
        carry = carry_ref[...]
        parts = []
        for i in range(pages):
            w = jnp.dot(lf_refs[i][...], later_and_total, precision=lax.Precision.HIGHEST,
                        preferred_element_type=F32)
            parts.append(w[:, :psz] + carry)
            carry = carry + w[:, psz:psz + 1]
        carry_ref[...] = carry
        k16 = jnp.concatenate([flat16(k_refs[i]) for i in range(pages)], axis=0)
        v16 = jnp.concatenate([flat16(v_refs[i]) for i in range(pages)], axis=0)
        page_update(k16, v16, jnp.concatenate(parts, axis=1), None)

    @pl.when(p == pl.num_programs(1) - 1)
    def _():
        out = jnp.zeros((rows, hd), F32)
        for h in range(kvh):
            out = out + jnp.where(row_kvh == h, acc_ref[:, h * hd:(h + 1) * hd], 0.0)
        o_ref[0] = (out / l_ref[...]).astype(BF16)


def fox_sample(page_table, q16, c_col, c_new_t, k_new, v_new, pool_k, pool_v, pool_lf, layer, steps):
    bd, rows, hd = q16.shape
    heads = rows // steps
    psz, kvh = pool_k.shape[2], pool_k.shape[3]
    kw = kvh * hd
    n_pages = page_table.shape[1]
    pages = _pick(n_pages, 8, 4, 2, 1)
    n_steps = n_pages // pages + 1

    def page_idx(i, tail):
        return lambda b, p, pt: (layer, pt[b, n_pages - 1 - (jnp.maximum(p - 1, 0) * pages + i)]) + tail

    per_b = lambda a: pl.BlockSpec((1,) + a.shape[1:], lambda b, p, pt: (b,) + (0,) * (a.ndim - 1))
    kv_specs = lambda: [pl.BlockSpec((None, None, psz, kvh, hd), page_idx(i, (0, 0, 0))) for i in range(pages)]
    lf_specs = [pl.BlockSpec((None, None, heads, psz), page_idx(i, (0, 0))) for i in range(pages)]
    grid_spec = pltpu.PrefetchScalarGridSpec(
        num_scalar_prefetch=1,
        grid=(bd, n_steps),
        in_specs=[per_b(q16), per_b(c_col), per_b(c_new_t), per_b(k_new), per_b(v_new)]
        + kv_specs() + kv_specs() + lf_specs,
        out_specs=pl.BlockSpec((1, rows, hd), lambda b, p, pt: (b, 0, 0)),
        scratch_shapes=[pltpu.VMEM((rows, kw), BF16), pltpu.VMEM((rows, 1), F32),
                        pltpu.VMEM((rows, 1), F32), pltpu.VMEM((rows, kw), F32),
                        pltpu.VMEM((heads, 1), F32)],
    )
    return pl.pallas_call(
        functools.partial(_fox_s_kernel, pages=pages, steps=steps, grp=heads // FOX_KV_HEADS,
                          scale=FOX_HEAD_DIM ** -0.5),
        grid_spec=grid_spec,
        out_shape=jax.ShapeDtypeStruct((bd, rows, hd), BF16),
        compiler_params=_params(("arbitrary", "arbitrary")),
        name="fox_sample",
    )(page_table, q16, c_col, c_new_t, k_new, v_new,
      *([pool_k] * pages), *([pool_v] * pages), *([pool_lf] * pages))


def _time_major(a):
    return jnp.swapaxes(a, 0, 1).reshape((a.shape[0] * a.shape[1],) + a.shape[2:])


def _batch_major(a, bd):
    return jnp.swapaxes(a.reshape((a.shape[0] // bd, bd) + a.shape[1:]), 0, 1)


def _pad_rows(a, rows):
    return jnp.pad(a, ((0, 0), (0, rows - a.shape[1]), (0, 0)))


def kernel(x_prompt, x_sample, state_conv, cache_win_k, cache_win_v, cache_k, cache_v, cache_logf,
           state_ffn, page_table, w_in_a, conv_a, w_out_a, w_qkv_b, sinks_b, w_o_b, w_qkvf_c, b_f_c,
           w_o_c, ln1_g, ln1_b, w_up, conv_f, w_down, ln2_g, ln2_b):
    batch, seq, d = x_prompt.shape
    bd, steps, _ = x_sample.shape
    depth = ln1_g.shape[0]
    alpha = (2 * depth) ** 0.25
    n_mixers = 3
    f = w_down.shape[1]

    xp = x_prompt.reshape(batch * seq, d)
    xp16 = xp.astype(BF16)
    xs = _time_major(x_sample)
    xs16 = xs.astype(BF16)

    up16 = None
    conv_p, conv_s, wk_p, wv_p, wk_s, wv_s = [], [], [], [], [], []
    fk_p, fv_p, fl_p, fk_s, fv_s, fl_s = [], [], [], [], [], []
    ffn_p, ffn_s = [], []

    for i in range(depth):
        mix, j = i % n_mixers, i // n_mixers
        if mix == 0:
            zs, st_s, wb16, wc16, wh16 = gate_conv_sample(xs16, w_in_a, j, conv_a[j],
                                                          _time_major(state_conv[j]), bd)
            ys, wo16 = mm_sample(zs, w_out_a, j, d, F32, res=xs, alpha=alpha)
            zp, st_p = gate_conv_prompt(xp16, wb16, wc16, wh16, conv_a[j], batch)
            yp = mm_prompt(zp, wo16, d, 0, F32, res=xp, alpha=alpha)
            conv_p.append(st_p)
            conv_s.append(_batch_major(st_s, bd))
        elif mix == 1:
            hd, kvh = SW_HEAD_DIM, SW_KV_HEADS
            kw = kvh * hd
            grp = d // kw
            qkv_s, w16 = mm_sample(xs16, w_qkv_b, j, d + 2 * kw, F32)
            qp = mm_prompt(xp16, w16, d, 0, BF16)
            kp = mm_prompt(xp16, w16, kw, d, F32)
            vp = mm_prompt(xp16, w16, kw, d + kw, F32)
            op = swa_prompt(qp, kp, vp, sinks_b[j], batch)
            qs = qkv_s[:, :d].astype(BF16).reshape(steps, bd, kvh, grp, hd)
            qs = qs.transpose(1, 2, 3, 0, 4).reshape(bd, kvh, grp * steps, hd)
            kn = _batch_major(qkv_s[:, d:d + kw], bd)
            vn = _batch_major(qkv_s[:, d + kw:], bd)
            sink_col = jnp.repeat(sinks_b[j].reshape(kvh, grp), steps, axis=1)[..., None]
            os_ = swa_sample(qs, cache_win_k, cache_win_v, j, _pad_rows(kn, 8), _pad_rows(vn, 8),
                             sink_col, steps)
            os_ = os_.reshape(bd, kvh, grp, steps, hd).transpose(3, 0, 1, 2, 4).reshape(steps * bd, d)
            ys, wo16 = mm_sample(os_.astype(BF16), w_o_b, j, d, F32, res=xs, alpha=alpha)
            yp = mm_prompt(op, wo16, d, 0, F32, res=xp, alpha=alpha)
            keep = min(WINDOW, seq)
            wk_p.append(kp.reshape(batch, seq, kw)[:, seq - keep:].reshape(batch, keep, kvh, hd))
            wv_p.append(vp.reshape(batch, seq, kw)[:, seq - keep:].reshape(batch, keep, kvh, hd))
            wb = cache_win_k.shape[2]
            wk_s.append(jnp.concatenate([cache_win_k[j], kn.reshape(bd, steps, kvh, hd)], axis=1)[:, -wb:])
            wv_s.append(jnp.concatenate([cache_win_v[j], vn.reshape(bd, steps, kvh, hd)], axis=1)[:, -wb:])
        else:
            hd, kvh = FOX_HEAD_DIM, FOX_KV_HEADS
            kw = kvh * hd
            heads = d // hd
            grp = heads // kvh
            wf = w_qkvf_c[j, :, d + 2 * kw:]
            qkv_s, w16 = mm_sample(xs16, w_qkvf_c, j, d + 2 * kw, F32)
            lf_s, c_s = fox_logf_sample(xs16, wf, b_f_c[j], bd)
            qp = mm_prompt(xp16, w16, d, 0, BF16)
            kp = mm_prompt(xp16, w16, kw, d, F32)
            vp = mm_prompt(xp16, w16, kw, d + kw, F32)
            lf_p, c_p, ct_p = fox_logf_prompt(xp16, wf, b_f_c[j], batch)
            tk = _pick(seq, 512, 256, 128)
            c_k = ct_p.reshape(batch, kvh, grp, seq // tk, tk).transpose(0, 1, 3, 2, 4)
            op = fox_prompt(qp, kp, vp, c_p, c_k, batch)
            qs = _batch_major(qkv_s[:, :d].astype(BF16), bd).reshape(bd, steps * heads, hd)
            kn = _batch_major(qkv_s[:, d:d + kw], bd)
            vn = _batch_major(qkv_s[:, d + kw:], bd)
            c_b = _batch_major(c_s, bd)
            c_col = c_b.reshape(bd, steps * heads, 1)
            c_new_t = jnp.pad(c_b.transpose(0, 2, 1), ((0, 0), (0, 0), (0, PAGE_SIZE - steps)))
            os_ = fox_sample(page_table, qs, c_col, c_new_t, _pad_rows(kn, 8), _pad_rows(vn, 8),
                             cache_k, cache_v,
                             jnp.swapaxes(cache_logf, 2, 3), j, steps)
            os_ = _time_major(os_.reshape(bd, steps, d))
            ys, wo16 = mm_sample(os_, w_o_c, j, d, F32, res=xs, alpha=alpha)
            yp = mm_prompt(op, wo16, d, 0, F32, res=xp, alpha=alpha)
            fk_p.append(kp.reshape(batch, seq, kvh, hd))
            fv_p.append(vp.reshape(batch, seq, kvh, hd))
            fl_p.append(lf_p.reshape(batch, seq, heads))
            fk_s.append(kn.reshape(bd, steps, kvh, hd))
            fv_s.append(vn.reshape(bd, steps, kvh, hd))
            fl_s.append(_batch_major(lf_s, bd))
        xs, xs16 = layer_norm_rows(ys, ln1_g[i], ln1_b[i])
        xp, xp16 = layer_norm_rows(yp, ln1_g[i], ln1_b[i])

        prev_ffn = _time_major(state_ffn[i])
        if up16 is None:
            a_s, sg_s, su_s, wg16, wu16 = ffn_up_sample(xs16, w_up, i, conv_f[i], prev_ffn, bd)
            u_off = 0
        else:
            a_s, sg_s, su_s = ffn_up_sample(xs16, up16, None, conv_f[i], prev_ffn, bd)
            wg16 = wu16 = up16
            u_off = f // _pick(f, 256, 128)
        a_p, sg_p, su_p, wd16, *nxt = ffn_up_prompt(xp16, wg16, wu16, u_off, conv_f[i], batch,
                                                   w_down, w_up, i)
        up16 = nxt[0] if nxt else None
        ys = mm_prompt(a_s, wd16, d, 0, F32, res=xs, alpha=alpha)
        yp = mm_prompt(a_p, wd16, d, 0, F32, res=xp, alpha=alpha)
        ffn_p.append(jnp.concatenate([sg_p, su_p], axis=-1))
        ffn_s.append(_batch_major(jnp.concatenate([sg_s, su_s], axis=-1), bd))
        xs, xs16 = layer_norm_rows(ys, ln2_g[i], ln2_b[i])
        xp, xp16 = layer_norm_rows(yp, ln2_g[i], ln2_b[i])

    return (xp.reshape(batch, seq, d), _batch_major(xs, bd),
            jnp.stack(conv_p), jnp.stack(conv_s), jnp.stack(wk_p), jnp.stack(wv_p),
            jnp.stack(wk_s), jnp.stack(wv_s), jnp.stack(fk_p), jnp.stack(fv_p), jnp.stack(fl_p),
            jnp.stack(fk_s), jnp.stack(fv_s), jnp.stack(fl_s), jnp.stack(ffn_p), jnp.stack(ffn_s))
```

```python
import functools
import math

import jax
import jax.numpy as jnp
from jax import lax
from jax.experimental import pallas as pl
from jax.experimental.pallas import tpu as pltpu

F32 = jnp.float32
BF16 = jnp.bfloat16

CONV_TAPS = 3
CONV_PREV = CONV_TAPS - 1
CONV_HALO = 8
SW_HEAD_DIM = 64
SW_KV_HEADS = 8
WINDOW = 128
FOX_HEAD_DIM = 128
FOX_KV_HEADS = 8
PAGE_SIZE = 128
LN_EPS = 1e-5
NEG_INF = -1e30
LOG2_E = math.log2(math.e)

V7X_VMEM_BYTES = 64 * 1024 * 1024
VMEM_LIMIT = V7X_VMEM_BYTES - 8 * 1024 * 1024
LANE = 128
BF16_SUBLANES = 16


def _params(semantics):
    return pltpu.CompilerParams(dimension_semantics=semantics, vmem_limit_bytes=VMEM_LIMIT)


def _dot(a, b):
    return jnp.dot(a, b, preferred_element_type=F32)


def _dot_nt(a, b):
    return lax.dot_general(a, b, (((1,), (1,)), ((), ())), preferred_element_type=F32)


def _split_scale(scale):
    pre = 2.0 ** math.floor(math.log2(scale))
    return pre, scale / pre


def _pick(n, *cands):
    for c in cands:
        if n % c == 0:
            return c
    raise ValueError(f"no tile for {n} among {cands}")


def _conv_rows(u, prev, cw, shift):
    rows = u.shape[0]
    full = jnp.concatenate([prev, u], axis=0)
    y = cw[0:1, :] * full[0:rows] + cw[1:2, :] * full[shift:shift + rows] + cw[2:3, :] * u
    return y, full[rows:rows + CONV_PREV * shift]


def _conv_block(u, halo, cw):
    full = jnp.concatenate([halo, u], axis=0)
    y = cw[0:1, :] * pltpu.roll(full, 2, 0) + cw[1:2, :] * pltpu.roll(full, 1, 0) + cw[2:3, :] * full
    return y[CONV_HALO:]


def _gate_conv_core(x, wb, wc, wh, cw, prev, shift):
    bg = _dot(x, wb)
    c = _dot(x, wc)
    h = _dot(x, wh)
    y, tail = _conv_rows(c * h, prev, cw, shift)
    return (bg * y).astype(BF16), tail


def _gate_conv_p_kernel(x_ref, wb_ref, wc_ref, wh_ref, cw_ref, z_ref, st_ref, *, rb):
    wb, wc, wh, cw = wb_ref[...], wc_ref[...], wh_ref[...], cw_ref[...]
    halo = jnp.zeros((CONV_HALO, wb.shape[1]), F32)
    for r in range(x_ref.shape[0] // rb):
        x = x_ref[r * rb:(r + 1) * rb, :]
        u = _dot(x, wc) * _dot(x, wh)
        z_ref[r * rb:(r + 1) * rb, :] = (_dot(x, wb) * _conv_block(u, halo, cw)).astype(BF16)
        halo = u[rb - CONV_HALO:rb]
    st_ref[0] = halo[CONV_HALO - CONV_PREV:CONV_HALO]


def _gate_conv_s_kernel(x_ref, wb_ref, wc_ref, wh_ref, cw_ref, prev_ref,
                        z_ref, st_ref, wb16_ref, wc16_ref, wh16_ref, *, shift):
    wb = wb_ref[...].astype(BF16)
    wc = wc_ref[...].astype(BF16)
    wh = wh_ref[...].astype(BF16)
    wb16_ref[...] = wb
    wc16_ref[...] = wc
    wh16_ref[...] = wh
    z, tail = _gate_conv_core(x_ref[...], wb, wc, wh, cw_ref[...], prev_ref[...], shift)
    z_ref[...] = z
    st_ref[...] = tail


def gate_conv_sample(x16, w_in, layer, cw, prev, shift):
    rows, d = x16.shape
    c = w_in.shape[2] // 3
    tn = _pick(c, 256, 128)
    nc = c // tn
    w_spec = lambda sec: pl.BlockSpec((None, d, tn), lambda n, sec=sec: (layer, 0, sec * nc + n))
    col = lambda r: pl.BlockSpec((r, tn), lambda n: (0, n))
    return pl.pallas_call(
        functools.partial(_gate_conv_s_kernel, shift=shift),
        grid=(nc,),
        in_specs=[pl.BlockSpec((rows, d), lambda n: (0, 0)), w_spec(0), w_spec(1), w_spec(2),
                  col(CONV_TAPS), col(CONV_PREV * shift)],
        out_specs=[col(rows), col(CONV_PREV * shift), col(d), col(d), col(d)],
        out_shape=[jax.ShapeDtypeStruct((rows, c), BF16),
                   jax.ShapeDtypeStruct((CONV_PREV * shift, c), F32),
                   jax.ShapeDtypeStruct((d, c), BF16),
                   jax.ShapeDtypeStruct((d, c), BF16),
                   jax.ShapeDtypeStruct((d, c), BF16)],
        compiler_params=_params(("arbitrary",)),
        name="gate_conv_sample",
    )(x16, w_in, w_in, w_in, cw, prev)


def gate_conv_prompt(x16, wb16, wc16, wh16, cw, batch):
    m, d = x16.shape
    s = m // batch
    c = wb16.shape[1]
    tn = _pick(c, 256, 128)
    w_spec = pl.BlockSpec((d, tn), lambda b, n: (0, n))
    return pl.pallas_call(
        functools.partial(_gate_conv_p_kernel, rb=_pick(s, 512, 256, 128)),
        grid=(batch, c // tn),
        in_specs=[pl.BlockSpec((s, d), lambda b, n: (b, 0)),
                  w_spec, w_spec, w_spec,
                  pl.BlockSpec((CONV_TAPS, tn), lambda b, n: (0, n))],
        out_specs=[pl.BlockSpec((s, tn), lambda b, n: (b, n)),
                   pl.BlockSpec((1, CONV_PREV, tn), lambda b, n: (b, 0, n))],
        out_shape=[jax.ShapeDtypeStruct((m, c), BF16),
                   jax.ShapeDtypeStruct((batch, CONV_PREV, c), F32)],
        compiler_params=_params(("arbitrary", "arbitrary")),
        name="gate_conv_prompt",
    )(x16, wb16, wc16, wh16, cw)


def _ffn_up_core(x, wg, wu, cwg, cwu, prev_g, prev_u, shift):
    cg, tail_g = _conv_rows(_dot(x, wg), prev_g, cwg, shift)
    cu, tail_u = _conv_rows(_dot(x, wu), prev_u, cwu, shift)
    return _silu_gate(cg, cu), tail_g, tail_u


def _silu_gate(cg, cu):
    half = 0.5 * cg
    return ((half * jnp.tanh(half) + half) * cu).astype(BF16)


def _ffn_up_p_kernel(*refs, rb, has_next):
    if has_next:
        (x_ref, wg_ref, wu_ref, cwg_ref, cwu_ref, dn_ref, up_ref,
         a_ref, sg_ref, su_ref, dn16_ref, up16_ref) = refs
        up16_ref[...] = up_ref[...].astype(BF16)
    else:
        x_ref, wg_ref, wu_ref, cwg_ref, cwu_ref, dn_ref, a_ref, sg_ref, su_ref, dn16_ref = refs
    dn16_ref[...] = dn_ref[...].astype(BF16)
    wg, wu, cwg, cwu = wg_ref[...], wu_ref[...], cwg_ref[...], cwu_ref[...]
    halo_g = jnp.zeros((CONV_HALO, wg.shape[1]), F32)
    halo_u = halo_g
    for r in range(x_ref.shape[0] // rb):
        x = x_ref[r * rb:(r + 1) * rb, :]
        hg = _dot(x, wg)
        hu = _dot(x, wu)
        a_ref[r * rb:(r + 1) * rb, :] = _silu_gate(_conv_block(hg, halo_g, cwg),
                                                   _conv_block(hu, halo_u, cwu))
        halo_g = hg[rb - CONV_HALO:rb]
        halo_u = hu[rb - CONV_HALO:rb]
    sg_ref[0] = halo_g[CONV_HALO - CONV_PREV:CONV_HALO]
    su_ref[0] = halo_u[CONV_HALO - CONV_PREV:CONV_HALO]


def _ffn_up_s_kernel(x_ref, wg_ref, wu_ref, cwg_ref, cwu_ref, pg_ref, pu_ref,
                     a_ref, sg_ref, su_ref, *w16_refs, shift):
    wg = wg_ref[...].astype(BF16)
    wu = wu_ref[...].astype(BF16)
    if w16_refs:
        w16_refs[0][...] = wg
        w16_refs[1][...] = wu
    act, tg, tu = _ffn_up_core(x_ref[...], wg, wu, cwg_ref[...], cwu_ref[...],
                               pg_ref[...], pu_ref[...], shift)
    a_ref[...] = act
    sg_ref[...] = tg
    su_ref[...] = tu


def ffn_up_sample(x16, w_up, layer, cw, prev, shift):
    rows, d = x16.shape
    emit = layer is not None
    f = w_up.shape[-1] // 2
    tn = _pick(f, 256, 128)
    nf = f // tn
    sec = lambda r, k: pl.BlockSpec((r, tn), lambda n, k=k: (0, k * nf + n))
    if emit:
        wsec = lambda k: pl.BlockSpec((None, d, tn), lambda n, k=k: (layer, 0, k * nf + n))
    else:
        wsec = lambda k: sec(d, k)
    col = lambda r: pl.BlockSpec((r, tn), lambda n: (0, n))
    pr = CONV_PREV * shift
    w16_shape = jax.ShapeDtypeStruct((d, f), BF16)
    return pl.pallas_call(
        functools.partial(_ffn_up_s_kernel, shift=shift),
        grid=(nf,),
        in_specs=[pl.BlockSpec((rows, d), lambda n: (0, 0)), wsec(0), wsec(1),
                  sec(CONV_TAPS, 0), sec(CONV_TAPS, 1), sec(pr, 0), sec(pr, 1)],
        out_specs=[col(rows), col(pr), col(pr)] + ([col(d), col(d)] if emit else []),
        out_shape=[jax.ShapeDtypeStruct((rows, f), BF16),
                   jax.ShapeDtypeStruct((pr, f), F32),
                   jax.ShapeDtypeStruct((pr, f), F32)] + ([w16_shape, w16_shape] if emit else []),
        compiler_params=_params(("arbitrary",)),
        name="ffn_up_sample",
    )(x16, w_up, w_up, cw, cw, prev, prev)


def ffn_up_prompt(x16, wg16, wu16, u_off, cw, batch, w_down, w_up, layer):
    m, d = x16.shape
    s = m // batch
    f = w_down.shape[1]
    tn = _pick(f, 256, 128)
    nf = f // tn
    steps = batch * nf
    has_next = layer + 1 < w_up.shape[0]
    dn_rows, up_cols = f // steps, 2 * f // steps
    assert dn_rows * steps == f and dn_rows % BF16_SUBLANES == 0 and up_cols % LANE == 0
    step = lambda b, n: b * nf + n
    st_spec = pl.BlockSpec((1, CONV_PREV, tn), lambda b, n: (b, 0, n))
    in_specs = [pl.BlockSpec((s, d), lambda b, n: (b, 0)),
                pl.BlockSpec((d, tn), lambda b, n: (0, n)),
                pl.BlockSpec((d, tn), lambda b, n: (0, u_off + n)),
                pl.BlockSpec((CONV_TAPS, tn), lambda b, n: (0, n)),
                pl.BlockSpec((CONV_TAPS, tn), lambda b, n: (0, nf + n)),
                pl.BlockSpec((None, dn_rows, d), lambda b, n: (layer, step(b, n), 0))]
    out_specs = [pl.BlockSpec((s, tn), lambda b, n: (b, n)), st_spec, st_spec,
                 pl.BlockSpec((dn_rows, d), lambda b, n: (step(b, n), 0))]
    out_shape = [jax.ShapeDtypeStruct((m, f), BF16),
                 jax.ShapeDtypeStruct((batch, CONV_PREV, f), F32),
                 jax.ShapeDtypeStruct((batch, CONV_PREV, f), F32),
                 jax.ShapeDtypeStruct((f, d), BF16)]
    args = [x16, wg16, wu16, cw, cw, w_down]
    if has_next:
        in_specs.append(pl.BlockSpec((None, d, up_cols), lambda b, n: (layer + 1, 0, step(b, n))))
        out_specs.append(pl.BlockSpec((d, up_cols), lambda b, n: (0, step(b, n))))
        out_shape.append(jax.ShapeDtypeStruct((d, 2 * f), BF16))
        args.append(w_up)
    return pl.pallas_call(
        functools.partial(_ffn_up_p_kernel, rb=_pick(s, 512, 256, 128), has_next=has_next),
        grid=(batch, nf),
        in_specs=in_specs,
        out_specs=out_specs,
        out_shape=out_shape,
        compiler_params=_params(("arbitrary", "arbitrary")),
        name="ffn_up_prompt",
    )(*args)


def _mm_p_kernel(*refs, alpha, has_res):
    if has_res:
        x_ref, w_ref, r_ref, o_ref = refs
    else:
        x_ref, w_ref, o_ref = refs
    acc = _dot(x_ref[...], w_ref[...])
    if has_res:
        acc = alpha * r_ref[...] + acc
    o_ref[...] = acc.astype(o_ref.dtype)


def _mm_s_kernel(*refs, alpha, has_res):
    if has_res:
        x_ref, w_ref, r_ref, o_ref, w16_ref = refs
    else:
        x_ref, w_ref, o_ref, w16_ref = refs
    w = w_ref[...].astype(BF16)
    w16_ref[...] = w
    acc = _dot(x_ref[...], w)
    if has_res:
        acc = alpha * r_ref[...] + acc
    o_ref[...] = acc.astype(o_ref.dtype)


def mm_sample(x16, w, layer, n_out, out_dtype, res=None, alpha=None):
    rows, k = x16.shape
    tn = _pick(n_out, 512, 256, 128) if k <= 4096 else _pick(n_out, 256, 128)
    col = lambda r: pl.BlockSpec((r, tn), lambda n: (0, n))
    in_specs = [pl.BlockSpec((rows, k), lambda n: (0, 0)),
                pl.BlockSpec((None, k, tn), lambda n: (layer, 0, n))]
    args = [x16, w]
    if res is not None:
        in_specs.append(col(rows))
        args.append(res)
    return pl.pallas_call(
        functools.partial(_mm_s_kernel, alpha=alpha, has_res=res is not None),
        grid=(n_out // tn,),
        in_specs=in_specs,
        out_specs=[col(rows), col(k)],
        out_shape=[jax.ShapeDtypeStruct((rows, n_out), out_dtype),
                   jax.ShapeDtypeStruct((k, n_out), BF16)],
        compiler_params=_params(("arbitrary",)),
        name="mm_sample",
    )(*args)


def mm_prompt(x16, w16, n_out, col_off, out_dtype, res=None, alpha=None):
    m, k = x16.shape
    if k <= 4096:
        tm, tn = _pick(m, 1024, 512, 256, 128), _pick(n_out, 1024, 512, 256, 128)
    else:
        tm, tn = _pick(m, 512, 256, 128), _pick(n_out, 512, 256, 128)
    assert col_off % tn == 0
    off = col_off // tn
    in_specs = [pl.BlockSpec((tm, k), lambda i, n: (i, 0)),
                pl.BlockSpec((k, tn), lambda i, n: (0, off + n))]
    args = [x16, w16]
    if res is not None:
        in_specs.append(pl.BlockSpec((tm, tn), lambda i, n: (i, n)))
        args.append(res)
    return pl.pallas_call(
        functools.partial(_mm_p_kernel, alpha=alpha, has_res=res is not None),
        grid=(m // tm, n_out // tn),
        in_specs=in_specs,
        out_specs=pl.BlockSpec((tm, tn), lambda i, n: (i, n)),
        out_shape=jax.ShapeDtypeStruct((m, n_out), out_dtype),
        compiler_params=_params(("arbitrary", "arbitrary")),
        name="mm_prompt",
    )(*args)


def _ln_kernel(y_ref, g_ref, b_ref, xf_ref, xb_ref):
    y = y_ref[...]
    mu = jnp.mean(y, axis=-1, keepdims=True)
    dev = y - mu
    var = jnp.mean(dev * dev, axis=-1, keepdims=True)
    out = dev * lax.rsqrt(var + LN_EPS) * g_ref[...] + b_ref[...]
    xf_ref[...] = out
    xb_ref[...] = out.astype(BF16)


def layer_norm_rows(y, g, b):
    m, d = y.shape
    tr = _pick(m, 512, 256, 128)
    row = pl.BlockSpec((tr, d), lambda i: (i, 0))
    vec = pl.BlockSpec((1, d), lambda i: (0, 0))
    return pl.pallas_call(
        _ln_kernel,
        grid=(m // tr,),
        in_specs=[row, vec, vec],
        out_specs=[row, row],
        out_shape=[jax.ShapeDtypeStruct((m, d), F32), jax.ShapeDtypeStruct((m, d), BF16)],
        compiler_params=_params(("arbitrary",)),
        name="layer_norm",
    )(y, g.reshape(1, d), b.reshape(1, d))


def _sink_softmax_pv(s, sink, v16):
    m = jnp.maximum(jnp.max(s, axis=-1, keepdims=True), sink)
    p = jnp.exp(s - m)
    den = jnp.sum(p, axis=-1, keepdims=True) + jnp.exp(sink - m)
    return _dot((p / den).astype(BF16), v16)


def _swa_p_kernel(q_ref, kp_ref, ko_ref, vp_ref, vo_ref, sink_ref, o_ref, *, grp, pre, rest):
    blk = q_ref.shape[0]
    rows = grp * blk
    first = pl.program_id(1) == 0
    kcat = jnp.concatenate([kp_ref[...], ko_ref[...]], axis=0).astype(BF16)
    vcat = jnp.concatenate([vp_ref[...], vo_ref[...]], axis=0).astype(BF16)
    tl = lax.broadcasted_iota(jnp.int32, (rows, 2 * blk), 0) % blk
    j = lax.broadcasted_iota(jnp.int32, (rows, 2 * blk), 1)
    mask = (j >= tl) & (j <= tl + WINDOW) & ((j >= blk) | jnp.logical_not(first))
    hd = SW_HEAD_DIM
    unit = (lax.broadcasted_iota(jnp.int32, (2 * blk, hd), 1) == 0).astype(BF16)
    for h in range(SW_KV_HEADS):
        heads = range(h * grp, (h + 1) * grp)
        q = jnp.concatenate([q_ref[:, a * hd:(a + 1) * hd] for a in heads], axis=0) * pre
        sink = jnp.concatenate([jnp.full((blk, 1), sink_ref[a], F32) for a in heads], axis=0)
        s = _dot_nt(q, kcat[:, h * hd:(h + 1) * hd])
        if rest != 1.0:
            s = s * rest
        s = jnp.where(mask, s, NEG_INF)
        m = jnp.maximum(jnp.max(s, axis=-1, keepdims=True), sink)
        vh = jnp.concatenate([vcat[:, h * hd:(h + 1) * hd], unit], axis=1)
        ov = _dot(jnp.exp(s - m).astype(BF16), vh)
        den = ov[:, hd:hd + 1] + jnp.exp(sink - m)
        o = (ov[:, :hd] / den).astype(BF16)
        for g, a in enumerate(heads):
            o_ref[:, a * hd:(a + 1) * hd] = o[g * blk:(g + 1) * blk]


def swa_prompt(q16, k, v, sinks, batch):
    m, qw = q16.shape
    s = m // batch
    blk = WINDOW
    nb = s // blk
    kw = SW_KV_HEADS * SW_HEAD_DIM
    grp = qw // kw
    own = pl.BlockSpec((blk, kw), lambda b, i: (b * nb + i, 0))
    prev = pl.BlockSpec((blk, kw), lambda b, i: (b * nb + jnp.maximum(i - 1, 0), 0))
    pre, rest = _split_scale(SW_HEAD_DIM ** -0.5)
    return pl.pallas_call(
        functools.partial(_swa_p_kernel, grp=grp, pre=pre, rest=rest),
        grid=(batch, nb),
        in_specs=[pl.BlockSpec((blk, qw), lambda b, i: (b * nb + i, 0)),
                  prev, own, prev, own,
                  pl.BlockSpec(memory_space=pltpu.SMEM)],
        out_specs=pl.BlockSpec((blk, qw), lambda b, i: (b * nb + i, 0)),
        out_shape=jax.ShapeDtypeStruct((m, qw), BF16),
        compiler_params=_params(("arbitrary", "arbitrary")),
        name="swa_prompt",
    )(q16, k, k, v, v, sinks)


def _swa_s_kernel(q_ref, ck_ref, cv_ref, kn_ref, vn_ref, sink_ref, o_ref, *, steps, scale):
    wb, kvh, hd = ck_ref.shape
    pad = jnp.zeros((wb - kn_ref.shape[1], hd), F32)
    rows = q_ref.shape[2]
    t = lax.broadcasted_iota(jnp.int32, (rows, 2 * wb), 0) % steps
    j = lax.broadcasted_iota(jnp.int32, (rows, 2 * wb), 1)
    mask = (j >= t + wb - WINDOW) & (j <= t + wb)

    def keys(cache_ref, new_ref, h):
        cached = cache_ref.reshape(wb * kvh, hd)[pl.ds(h, wb, stride=kvh), :]
        return jnp.concatenate([cached, new_ref[0, :, h * hd:(h + 1) * hd], pad], axis=0).astype(BF16)

    for h in range(kvh):
        s = _dot_nt(q_ref[0, h], keys(ck_ref, kn_ref, h)) * scale
        s = jnp.where(mask, s, NEG_INF)
        o_ref[0, h] = _sink_softmax_pv(s, sink_ref[h], keys(cv_ref, vn_ref, h))


def swa_sample(q16, cache_k, cache_v, layer, k_new, v_new, sink_col, steps):
    bd, kvh, rows, hd = q16.shape
    wb = cache_k.shape[2]
    per_b = lambda a: pl.BlockSpec((1,) + a.shape[1:], lambda b: (b,) + (0,) * (a.ndim - 1))
    cache_spec = pl.BlockSpec((None, None, wb, kvh, hd), lambda b: (layer, b, 0, 0, 0))
    return pl.pallas_call(
        functools.partial(_swa_s_kernel, steps=steps, scale=SW_HEAD_DIM ** -0.5),
        grid=(bd,),
        in_specs=[per_b(q16), cache_spec, cache_spec, per_b(k_new), per_b(v_new),
                  pl.BlockSpec(sink_col.shape, lambda b: (0, 0, 0))],
        out_specs=pl.BlockSpec((1, kvh, rows, hd), lambda b: (b, 0, 0, 0)),
        out_shape=jax.ShapeDtypeStruct((bd, kvh, rows, hd), F32),
        compiler_params=_params(("arbitrary",)),
        name="swa_sample",
    )(q16, cache_k, cache_v, k_new, v_new, sink_col)


def _log_sigmoid(z):
    return jnp.minimum(z, 0.0) - jnp.log1p(jnp.exp(-jnp.abs(z)))


def _fox_logf_p_kernel(x_ref, wf_ref, bf_ref, lf_ref, c_ref, ct_ref, *, blk):
    lf = _log_sigmoid(_dot(x_ref[...], wf_ref[...].astype(BF16)) + bf_ref[...])
    lf_ref[...] = lf
    r = lax.broadcasted_iota(jnp.int32, (blk, blk), 0)
    c = lax.broadcasted_iota(jnp.int32, (blk, blk), 1)
    upto_rows = (c <= r).astype(F32)
    upto_cols = (r <= c).astype(F32)
    carry = jnp.zeros((1, lf.shape[1]), F32)
    carry_t = jnp.zeros((lf.shape[1], 1), F32)
    for i in range(lf.shape[0] // blk):
        part = lf[i * blk:(i + 1) * blk]
        cb = jnp.dot(upto_rows, part, precision=lax.Precision.HIGHEST,
                     preferred_element_type=F32) + carry
        c_ref[i * blk:(i + 1) * blk, :] = cb
        carry = cb[blk - 1:blk, :]
        cbt = lax.dot_general(part, upto_cols, (((0,), (0,)), ((), ())),
                              precision=lax.Precision.HIGHEST, preferred_element_type=F32) + carry_t
        ct_ref[0, :, i * blk:(i + 1) * blk] = cbt
        carry_t = cbt[:, blk - 1:blk]


def fox_logf_prompt(x16, wf, bf, batch):
    m, d = x16.shape
    s = m // batch
    h = wf.shape[1]
    row = pl.BlockSpec((s, h), lambda b: (b, 0))
    return pl.pallas_call(
        functools.partial(_fox_logf_p_kernel, blk=_pick(s, 256, 128)),
        grid=(batch,),
        in_specs=[pl.BlockSpec((s, d), lambda b: (b, 0)),
                  pl.BlockSpec((d, h), lambda b: (0, 0)),
                  pl.BlockSpec((1, h), lambda b: (0, 0))],
        out_specs=[row, row, pl.BlockSpec((1, h, s), lambda b: (b, 0, 0))],
        out_shape=[jax.ShapeDtypeStruct((m, h), F32), jax.ShapeDtypeStruct((m, h), F32),
                   jax.ShapeDtypeStruct((batch, h, s), F32)],
        compiler_params=_params(("arbitrary",)),
        name="fox_logf_prompt",
    )(x16, wf, bf.reshape(1, h))


def _fox_logf_s_kernel(x_ref, wf_ref, bf_ref, lf_ref, c_ref, *, shift):
    lf = _log_sigmoid(_dot(x_ref[...], wf_ref[...].astype(BF16)) + bf_ref[...])
    lf_ref[...] = lf
    run = lf[0:shift]
    c_ref[0:shift, :] = run
    for t in range(1, lf.shape[0] // shift):
        run = run + lf[t * shift:(t + 1) * shift]
        c_ref[t * shift:(t + 1) * shift, :] = run


def fox_logf_sample(x16, wf, bf, shift):
    rows, d = x16.shape
    h = wf.shape[1]
    full = lambda a, b: pl.BlockSpec((a, b), lambda i: (0, 0))
    return pl.pallas_call(
        functools.partial(_fox_logf_s_kernel, shift=shift),
        grid=(1,),
        in_specs=[full(rows, d), full(d, h), full(1, h)],
        out_specs=[full(rows, h), full(rows, h)],
        out_shape=[jax.ShapeDtypeStruct((rows, h), F32), jax.ShapeDtypeStruct((rows, h), F32)],
        compiler_params=_params(("arbitrary",)),
        name="fox_logf_sample",
    )(x16, wf, bf.reshape(1, h))


def _fox_p_kernel(q_ref, k_ref, v_ref, c_ref, ck_ref, o_ref, k16_ref, v16_ref, cq_ref, m_ref, acc_ref,
                  *, grp, pre, rest):
    h = pl.program_id(1)
    qi = pl.program_id(2)
    tq = q_ref.shape[0]
    tk = tq
    hd = FOX_HEAD_DIM
    rows = grp * tq
    c_exp = rest * LOG2_E

    @pl.when(qi == 0)
    def _():
        unit = (lax.broadcasted_iota(jnp.int32, v_ref.shape, 1) == 0).astype(BF16)
        k16_ref[...] = k_ref[...].astype(BF16)
        v16_ref[...] = jnp.concatenate([v_ref[...].astype(BF16), unit], axis=1)

    q = jnp.concatenate([q_ref[:, g * hd:(g + 1) * hd] for g in range(grp)], axis=0) * pre
    c_blk = c_ref[...]
    lane = lax.broadcasted_iota(jnp.int32, c_blk.shape, 1)
    for g in range(grp):
        col = jnp.sum(jnp.where(lane == h * grp + g, c_blk, 0.0), axis=-1, keepdims=True)
        cq_ref[g * tq:(g + 1) * tq, :] = jnp.broadcast_to(col * (1.0 / rest), (tq, hd))
    m_ref[...] = jnp.full(m_ref.shape, NEG_INF, F32)
    acc_ref[...] = jnp.zeros(acc_ref.shape, F32)

    def update(j, diagonal):
        off = pl.multiple_of(j * tk, tk)
        z = _dot_nt(q, k16_ref[pl.ds(off, tk), :])
        cq = cq_ref[...]
        t = jnp.concatenate([z[:, i * hd:(i + 1) * hd] + cq for i in range(tk // hd)], axis=1)
        ck = ck_ref[0, 0, j] * (1.0 / rest)
        t = (t.reshape(grp, tq, tk) - ck[:, None, :]).reshape(rows, tk)
        if diagonal:
            r = lax.broadcasted_iota(jnp.int32, (rows, tk), 0) % tq
            c = lax.broadcasted_iota(jnp.int32, (rows, tk), 1)
            t = jnp.where(c <= r, t, NEG_INF)
        m_old = m_ref[...]
        m_new = jnp.maximum(m_old, jnp.max(t, axis=-1, keepdims=True))
        a = jnp.exp2((m_old - m_new) * c_exp)
        p = jnp.exp2((t - m_new) * c_exp)
        acc_ref[...] = a * acc_ref[...] + _dot(p.astype(BF16), v16_ref[pl.ds(off, tk), :])
        m_ref[...] = m_new

    def below_diagonal(j, carry):
        update(j, False)
        return carry

    lax.fori_loop(0, qi, below_diagonal, 0)
    update(qi, True)
    acc = acc_ref[...]
    out = acc[:, :hd] / acc[:, hd:hd + 1]
    for g in range(grp):
        o_ref[:, g * hd:(g + 1) * hd] = out[g * tq:(g + 1) * tq].astype(BF16)


def fox_prompt(q16, k, v, c, c_k, batch):
    m, qw = q16.shape
    heads = c.shape[1]
    s = m // batch
    hd, kvh = FOX_HEAD_DIM, FOX_KV_HEADS
    grp = qw // (kvh * hd)
    nk, tq = c_k.shape[2], c_k.shape[4]
    nq = s // tq
    pre, rest = _split_scale(hd ** -0.5)
    kv_spec = pl.BlockSpec((s, hd), lambda b, h, qi: (b, h))
    return pl.pallas_call(
        functools.partial(_fox_p_kernel, grp=grp, pre=pre, rest=rest),
        grid=(batch, kvh, nq),
        in_specs=[pl.BlockSpec((tq, grp * hd), lambda b, h, qi: (b * nq + qi, h)),
                  kv_spec, kv_spec,
                  pl.BlockSpec((tq, heads), lambda b, h, qi: (b * nq + qi, 0)),
                  pl.BlockSpec((1, 1, nk, grp, tq), lambda b, h, qi: (b, h, 0, 0, 0))],
        out_specs=pl.BlockSpec((tq, grp * hd), lambda b, h, qi: (b * nq + qi, h)),
        out_shape=jax.ShapeDtypeStruct((m, qw), BF16),
        scratch_shapes=[pltpu.VMEM((s, hd), BF16), pltpu.VMEM((s, 2 * hd), BF16),
                        pltpu.VMEM((grp * tq, hd), F32), pltpu.VMEM((grp * tq, 1), F32),
                        pltpu.VMEM((grp * tq, 2 * hd), F32)],
        compiler_params=_params(("arbitrary", "arbitrary", "arbitrary")),
        name="fox_prompt",
    )(q16, k, v, c, c_k)


def _fox_s_kernel(pt_ref, q_ref, ccol_ref, cnt_ref, kn_ref, vn_ref, *rest, pages, steps, grp, scale):
    k_refs = rest[0:pages]
    v_refs = rest[pages:2 * pages]
    lf_refs = rest[2 * pages:3 * pages]
    o_ref, qbd_ref, m_ref, l_ref, acc_ref, carry_ref = rest[3 * pages:]
    p = pl.program_id(1)
    rows, hd = q_ref.shape[1], q_ref.shape[2]
    heads = rows // steps
    kvh = heads // grp
    psz = cnt_ref.shape[2]
    row = lax.broadcasted_iota(jnp.int32, (rows, hd), 0)
    row_kvh = (row % heads) // grp

    def page_update(k16, v16, neg_ck, mask):
        s = _dot_nt(qbd_ref[...], k16) * scale
        s = s + (ccol_ref[0] + jnp.concatenate([neg_ck] * steps, axis=0))
        if mask is not None:
            s = jnp.where(mask, s, NEG_INF)
        m_old = m_ref[...]
        m_new = jnp.maximum(m_old, jnp.max(s, axis=-1, keepdims=True))
        a = jnp.exp(m_old - m_new)
        pr = jnp.exp(s - m_new)
        l_ref[...] = a * l_ref[...] + jnp.sum(pr, axis=-1, keepdims=True)
        acc_ref[...] = a * acc_ref[...] + _dot(pr.astype(BF16), v16)
        m_ref[...] = m_new

    @pl.when(p == 0)
    def _():
        q = q_ref[0]
        for h in range(kvh):
            qbd_ref[:, h * hd:(h + 1) * hd] = jnp.where(row_kvh == h, q, jnp.zeros_like(q))
        m_ref[...] = jnp.full(m_ref.shape, NEG_INF, F32)
        l_ref[...] = jnp.zeros(l_ref.shape, F32)
        acc_ref[...] = jnp.zeros(acc_ref.shape, F32)
        carry_ref[...] = jnp.zeros(carry_ref.shape, F32)
        pad = jnp.zeros((psz - kn_ref.shape[1], kn_ref.shape[2]), F32)
        k16 = jnp.concatenate([kn_ref[0], pad], axis=0).astype(BF16)
        v16 = jnp.concatenate([vn_ref[0], pad], axis=0).astype(BF16)
        t = lax.broadcasted_iota(jnp.int32, (rows, psz), 0) // heads
        j = lax.broadcasted_iota(jnp.int32, (rows, psz), 1)
        page_update(k16, v16, -cnt_ref[0], j <= t)

    def flat16(ref):
        rows2d = ref.reshape(psz * kvh, hd)
        return jnp.concatenate([rows2d[pl.ds(h, psz, stride=kvh), :] for h in range(kvh)],
                               axis=1).astype(BF16)

    @pl.when(p > 0)
    def _():
        r = lax.broadcasted_iota(jnp.int32, (psz, 2 * psz), 0)
        c = lax.broadcasted_iota(jnp.int32, (psz, 2 * psz), 1)
        later_and_total = ((r > c) | (c >= psz)).astype(F32)
        carry = carry_ref[...]
        parts = []
        for i in range(pages):
            w = jnp.dot(lf_refs[i][...], later_and_total, precision=lax.Precision.HIGHEST,
                        preferred_element_type=F32)
            parts.append(w[:, :psz] + carry)
            carry = carry + w[:, psz:psz + 1]
        carry_ref[...] = carry
        k16 = jnp.concatenate([flat16(k_refs[i]) for i in range(pages)], axis=0)
        v16 = jnp.concatenate([flat16(v_refs[i]) for i in range(pages)], axis=0)
        page_update(k16, v16, jnp.concatenate(parts, axis=1), None)

    @pl.when(p == pl.num_programs(1) - 1)
    def _():
        out = jnp.zeros((rows, hd), F32)
        for h in range(kvh):
            out = out + jnp.where(row_kvh == h, acc_ref[:, h * hd:(h + 1) * hd], 0.0)
        o_ref[0] = (out / l_ref[...]).astype(BF16)


def fox_sample(page_table, q16, c_col, c_new_t, k_new, v_new, pool_k, pool_v, pool_lf, layer, steps):
    bd, rows, hd = q16.shape
    heads = rows // steps
    psz, kvh = pool_k.shape[2], pool_k.shape[3]
    kw = kvh * hd
    n_pages = page_table.shape[1]
    pages = _pick(n_pages, 16, 8, 4, 2, 1)
    n_steps = n_pages // pages + 1

    def page_idx(i, tail):
        return lambda b, p, pt: (layer, pt[b, n_pages - 1 - (jnp.maximum(p - 1, 0) * pages + i)]) + tail

    per_b = lambda a: pl.BlockSpec((1,) + a.shape[1:], lambda b, p, pt: (b,) + (0,) * (a.ndim - 1))
    kv_specs = lambda: [pl.BlockSpec((None, None, psz, kvh, hd), page_idx(i, (0, 0, 0))) for i in range(pages)]
    lf_specs = [pl.BlockSpec((None, None, heads, psz), page_idx(i, (0, 0))) for i in range(pages)]
    grid_spec = pltpu.PrefetchScalarGridSpec(
        num_scalar_prefetch=1,
        grid=(bd, n_steps),
        in_specs=[per_b(q16), per_b(c_col), per_b(c_new_t), per_b(k_new), per_b(v_new)]
        + kv_specs() + kv_specs() + lf_specs,
        out_specs=pl.BlockSpec((1, rows, hd), lambda b, p, pt: (b, 0, 0)),
        scratch_shapes=[pltpu.VMEM((rows, kw), BF16), pltpu.VMEM((rows, 1), F32),
                        pltpu.VMEM((rows, 1), F32), pltpu.VMEM((rows, kw), F32),
                        pltpu.VMEM((heads, 1), F32)],
    )
    return pl.pallas_call(
        functools.partial(_fox_s_kernel, pages=pages, steps=steps, grp=heads // FOX_KV_HEADS,
                          scale=FOX_HEAD_DIM ** -0.5),
        grid_spec=grid_spec,
        out_shape=jax.ShapeDtypeStruct((bd, rows, hd), BF16),
        compiler_params=_params(("arbitrary", "arbitrary")),
        name="fox_sample",
    )(page_table, q16, c_col, c_new_t, k_new, v_new,
      *([pool_k] * pages), *([pool_v] * pages), *([pool_lf] * pages))


def _time_major(a):
    return jnp.swapaxes(a, 0, 1).reshape((a.shape[0] * a.shape[1],) + a.shape[2:])


def _batch_major(a, bd):
    return jnp.swapaxes(a.reshape((a.shape[0] // bd, bd) + a.shape[1:]), 0, 1)


def _pad_rows(a, rows):
    return jnp.pad(a, ((0, 0), (0, rows - a.shape[1]), (0, 0)))


def kernel(x_prompt, x_sample, state_conv, cache_win_k, cache_win_v, cache_k, cache_v, cache_logf,
           state_ffn, page_table, w_in_a, conv_a, w_out_a, w_qkv_b, sinks_b, w_o_b, w_qkvf_c, b_f_c,
           w_o_c, ln1_g, ln1_b, w_up, conv_f, w_down, ln2_g, ln2_b):
    batch, seq, d = x_prompt.shape
    bd, steps, _ = x_sample.shape
    depth = ln1_g.shape[0]
    alpha = (2 * depth) ** 0.25
    n_mixers = 3
    f = w_down.shape[1]

    xp = x_prompt.reshape(batch * seq, d)
    xp16 = xp.astype(BF16)
    xs = _time_major(x_sample)
    xs16 = xs.astype(BF16)

    up16 = None
    conv_p, conv_s, wk_p, wv_p, wk_s, wv_s = [], [], [], [], [], []
    fk_p, fv_p, fl_p, fk_s, fv_s, fl_s = [], [], [], [], [], []
    ffn_p, ffn_s = [], []

    for i in range(depth):
        mix, j = i % n_mixers, i // n_mixers
        if mix == 0:
            zs, st_s, wb16, wc16, wh16 = gate_conv_sample(xs16, w_in_a, j, conv_a[j],
                                                          _time_major(state_conv[j]), bd)
            ys, wo16 = mm_sample(zs, w_out_a, j, d, F32, res=xs, alpha=alpha)
            zp, st_p = gate_conv_prompt(xp16, wb16, wc16, wh16, conv_a[j], batch)
            yp = mm_prompt(zp, wo16, d, 0, F32, res=xp, alpha=alpha)
            conv_p.append(st_p)
            conv_s.append(_batch_major(st_s, bd))
        elif mix == 1:
            hd, kvh = SW_HEAD_DIM, SW_KV_HEADS
            kw = kvh * hd
            grp = d // kw
            qkv_s, w16 = mm_sample(xs16, w_qkv_b, j, d + 2 * kw, F32)
            qp = mm_prompt(xp16, w16, d, 0, BF16)
            kp = mm_prompt(xp16, w16, kw, d, F32)
            vp = mm_prompt(xp16, w16, kw, d + kw, F32)
            op = swa_prompt(qp, kp, vp, sinks_b[j], batch)
            qs = qkv_s[:, :d].astype(BF16).reshape(steps, bd, kvh, grp, hd)
            qs = qs.transpose(1, 2, 3, 0, 4).reshape(bd, kvh, grp * steps, hd)
            kn = _batch_major(qkv_s[:, d:d + kw], bd)
            vn = _batch_major(qkv_s[:, d + kw:], bd)
            sink_col = jnp.repeat(sinks_b[j].reshape(kvh, grp), steps, axis=1)[..., None]
            os_ = swa_sample(qs, cache_win_k, cache_win_v, j, _pad_rows(kn, 8), _pad_rows(vn, 8),
                             sink_col, steps)
            os_ = os_.reshape(bd, kvh, grp, steps, hd).transpose(3, 0, 1, 2, 4).reshape(steps * bd, d)
            ys, wo16 = mm_sample(os_.astype(BF16), w_o_b, j, d, F32, res=xs, alpha=alpha)
            yp = mm_prompt(op, wo16, d, 0, F32, res=xp, alpha=alpha)
            keep = min(WINDOW, seq)
            wk_p.append(kp.reshape(batch, seq, kw)[:, seq - keep:].reshape(batch, keep, kvh, hd))
            wv_p.append(vp.reshape(batch, seq, kw)[:, seq - keep:].reshape(batch, keep, kvh, hd))
            wb = cache_win_k.shape[2]
            wk_s.append(jnp.concatenate([cache_win_k[j], kn.reshape(bd, steps, kvh, hd)], axis=1)[:, -wb:])
            wv_s.append(jnp.concatenate([cache_win_v[j], vn.reshape(bd, steps, kvh, hd)], axis=1)[:, -wb:])
        else:
            hd, kvh = FOX_HEAD_DIM, FOX_KV_HEADS
            kw = kvh * hd
            heads = d // hd
            grp = heads // kvh
            wf = w_qkvf_c[j, :, d + 2 * kw:]
            qkv_s, w16 = mm_sample(xs16, w_qkvf_c, j, d + 2 * kw, F32)
            lf_s, c_s = fox_logf_sample(xs16, wf, b_f_c[j], bd)
            qp = mm_prompt(xp16, w16, d, 0, BF16)
            kp = mm_prompt(xp16, w16, kw, d, F32)
            vp = mm_prompt(xp16, w16, kw, d + kw, F32)
            lf_p, c_p, ct_p = fox_logf_prompt(xp16, wf, b_f_c[j], batch)
            tk = _pick(seq, 512, 256, 128)
            c_k = ct_p.reshape(batch, kvh, grp, seq // tk, tk).transpose(0, 1, 3, 2, 4)
            op = fox_prompt(qp, kp, vp, c_p, c_k, batch)
            qs = _batch_major(qkv_s[:, :d].astype(BF16), bd).reshape(bd, steps * heads, hd)
            kn = _batch_major(qkv_s[:, d:d + kw], bd)
            vn = _batch_major(qkv_s[:, d + kw:], bd)
            c_b = _batch_major(c_s, bd)
            c_col = c_b.reshape(bd, steps * heads, 1)
            c_new_t = jnp.pad(c_b.transpose(0, 2, 1), ((0, 0), (0, 0), (0, PAGE_SIZE - steps)))
            os_ = fox_sample(page_table, qs, c_col, c_new_t, _pad_rows(kn, 8), _pad_rows(vn, 8),
                             cache_k, cache_v,
                             jnp.swapaxes(cache_logf, 2, 3), j, steps)
            os_ = _time_major(os_.reshape(bd, steps, d))
            ys, wo16 = mm_sample(os_, w_o_c, j, d, F32, res=xs, alpha=alpha)
            yp = mm_prompt(op, wo16, d, 0, F32, res=xp, alpha=alpha)
            fk_p.append(kp.reshape(batch, seq, kvh, hd))
            fv_p.append(vp.reshape(batch, seq, kvh, hd))
            fl_p.append(lf_p.reshape(batch, seq, heads))
            fk_s.append(kn.reshape(bd, steps, kvh, hd))
            fv_s.append(vn.reshape(bd, steps, kvh, hd))
            fl_s.append(_batch_major(lf_s, bd))
        xs, xs16 = layer_norm_rows(ys, ln1_g[i], ln1_b[i])
        xp, xp16 = layer_norm_rows(yp, ln1_g[i], ln1_b[i])

        prev_ffn = _time_major(state_ffn[i])
        if up16 is None:
            a_s, sg_s, su_s, wg16, wu16 = ffn_up_sample(xs16, w_up, i, conv_f[i], prev_ffn, bd)
            u_off = 0
        else:
            a_s, sg_s, su_s = ffn_up_sample(xs16, up16, None, conv_f[i], prev_ffn, bd)
            wg16 = wu16 = up16
            u_off = f // _pick(f, 256, 128)
        a_p, sg_p, su_p, wd16, *nxt = ffn_up_prompt(xp16, wg16, wu16, u_off, conv_f[i], batch,
                                                   w_down, w_up, i)
        up16 = nxt[0] if nxt else None
        ys = mm_prompt(a_s, wd16, d, 0, F32, res=xs, alpha=alpha)
        yp = mm_prompt(a_p, wd16, d, 0, F32, res=xp, alpha=alpha)
        ffn_p.append(jnp.concatenate([sg_p, su_p], axis=-1))
        ffn_s.append(_batch_major(jnp.concatenate([sg_s, su_s], axis=-1), bd))
        xs, xs16 = layer_norm_rows(ys, ln2_g[i], ln2_b[i])
        xp, xp16 = layer_norm_rows(yp, ln2_g[i], ln2_b[i])

    return (xp.reshape(batch, seq, d), _batch_major(xs, bd),
            jnp.stack(conv_p), jnp.stack(conv_s), jnp.stack(wk_p), jnp.stack(wv_p),
            jnp.stack(wk_s), jnp.stack(wv_s), jnp.stack(fk_p), jnp.stack(fv_p), jnp.stack(fl_p),
            jnp.stack(fk_s), jnp.stack(fv_s), jnp.stack(fl_s), jnp.stack(ffn_p), jnp.stack(ffn_s))
```

```python
import functools
import math

import jax
import jax.numpy as jnp
from jax import lax
from jax.experimental import pallas as pl
from jax.experimental.pallas import tpu as pltpu

F32 = jnp.float32
BF16 = jnp.bfloat16

CONV_TAPS = 3
CONV_PREV = CONV_TAPS - 1
CONV_HALO = 8
SW_HEAD_DIM = 64
SW_KV_HEADS = 8
WINDOW = 128
FOX_HEAD_DIM = 128
FOX_KV_HEADS = 8
PAGE_SIZE = 128
LN_EPS = 1e-5
NEG_INF = -1e30
LOG2_E = math.log2(math.e)

V7X_VMEM_BYTES = 64 * 1024 * 1024
VMEM_LIMIT = V7X_VMEM_BYTES - 8 * 1024 * 1024
LANE = 128
BF16_SUBLANES = 16


def _params(semantics):
    return pltpu.CompilerParams(dimension_semantics=semantics, vmem_limit_bytes=VMEM_LIMIT)


def _dot(a, b):
    return jnp.dot(a, b, preferred_element_type=F32)


def _dot_nt(a, b):
    return lax.dot_general(a, b, (((1,), (1,)), ((), ())), preferred_element_type=F32)


def _split_scale(scale):
    pre = 2.0 ** math.floor(math.log2(scale))
    return pre, scale / pre


def _pick(n, *cands):
    for c in cands:
        if n % c == 0:
            return c
    raise ValueError(f"no tile for {n} among {cands}")


def _conv_rows(u, prev, cw, shift):
    rows = u.shape[0]
    full = jnp.concatenate([prev, u], axis=0)
    y = cw[0:1, :] * full[0:rows] + cw[1:2, :] * full[shift:shift + rows] + cw[2:3, :] * u
    return y, full[rows:rows + CONV_PREV * shift]


def _conv_block(u, halo, cw):
    full = jnp.concatenate([halo, u], axis=0)
    y = cw[0:1, :] * pltpu.roll(full, 2, 0) + cw[1:2, :] * pltpu.roll(full, 1, 0) + cw[2:3, :] * full
    return y[CONV_HALO:]


def _gate_conv_core(x, wb, wc, wh, cw, prev, shift):
    bg = _dot(x, wb)
    c = _dot(x, wc)
    h = _dot(x, wh)
    y, tail = _conv_rows(c * h, prev, cw, shift)
    return (bg * y).astype(BF16), tail


def _gate_conv_p_kernel(x_ref, wb_ref, wc_ref, wh_ref, cw_ref, z_ref, st_ref, *, rb):
    wb, wc, wh, cw = wb_ref[...], wc_ref[...], wh_ref[...], cw_ref[...]
    halo = jnp.zeros((CONV_HALO, wb.shape[1]), F32)
    for r in range(x_ref.shape[0] // rb):
        x = x_ref[r * rb:(r + 1) * rb, :]
        u = _dot(x, wc) * _dot(x, wh)
        z_ref[r * rb:(r + 1) * rb, :] = (_dot(x, wb) * _conv_block(u, halo, cw)).astype(BF16)
        halo = u[rb - CONV_HALO:rb]
    st_ref[0] = halo[CONV_HALO - CONV_PREV:CONV_HALO]


def _gate_conv_s_kernel(x_ref, wb_ref, wc_ref, wh_ref, cw_ref, prev_ref,
                        z_ref, st_ref, wb16_ref, wc16_ref, wh16_ref, *, shift):
    wb = wb_ref[...].astype(BF16)
    wc = wc_ref[...].astype(BF16)
    wh = wh_ref[...].astype(BF16)
    wb16_ref[...] = wb
    wc16_ref[...] = wc
    wh16_ref[...] = wh
    z, tail = _gate_conv_core(x_ref[...], wb, wc, wh, cw_ref[...], prev_ref[...], shift)
    z_ref[...] = z
    st_ref[...] = tail


def gate_conv_sample(x16, w_in, layer, cw, prev, shift):
    rows, d = x16.shape
    c = w_in.shape[2] // 3
    tn = _pick(c, 256, 128)
    nc = c // tn
    w_spec = lambda sec: pl.BlockSpec((None, d, tn), lambda n, sec=sec: (layer, 0, sec * nc + n))
    col = lambda r: pl.BlockSpec((r, tn), lambda n: (0, n))
    return pl.pallas_call(
        functools.partial(_gate_conv_s_kernel, shift=shift),
        grid=(nc,),
        in_specs=[pl.BlockSpec((rows, d), lambda n: (0, 0)), w_spec(0), w_spec(1), w_spec(2),
                  col(CONV_TAPS), col(CONV_PREV * shift)],
        out_specs=[col(rows), col(CONV_PREV * shift), col(d), col(d), col(d)],
        out_shape=[jax.ShapeDtypeStruct((rows, c), BF16),
                   jax.ShapeDtypeStruct((CONV_PREV * shift, c), F32),
                   jax.ShapeDtypeStruct((d, c), BF16),
                   jax.ShapeDtypeStruct((d, c), BF16),
                   jax.ShapeDtypeStruct((d, c), BF16)],
        compiler_params=_params(("arbitrary",)),
        name="gate_conv_sample",
    )(x16, w_in, w_in, w_in, cw, prev)


def gate_conv_prompt(x16, wb16, wc16, wh16, cw, batch):
    m, d = x16.shape
    s = m // batch
    c = wb16.shape[1]
    tn = _pick(c, 256, 128)
    w_spec = pl.BlockSpec((d, tn), lambda b, n: (0, n))
    return pl.pallas_call(
        functools.partial(_gate_conv_p_kernel, rb=_pick(s, 512, 256, 128)),
        grid=(batch, c // tn),
        in_specs=[pl.BlockSpec((s, d), lambda b, n: (b, 0)),
                  w_spec, w_spec, w_spec,
                  pl.BlockSpec((CONV_TAPS, tn), lambda b, n: (0, n))],
        out_specs=[pl.BlockSpec((s, tn), lambda b, n: (b, n)),
                   pl.BlockSpec((1, CONV_PREV, tn), lambda b, n: (b, 0, n))],
        out_shape=[jax.ShapeDtypeStruct((m, c), BF16),
                   jax.ShapeDtypeStruct((batch, CONV_PREV, c), F32)],
        compiler_params=_params(("arbitrary", "arbitrary")),
        name="gate_conv_prompt",
    )(x16, wb16, wc16, wh16, cw)


def _ffn_up_core(x, wg, wu, cwg, cwu, prev_g, prev_u, shift):
    cg, tail_g = _conv_rows(_dot(x, wg), prev_g, cwg, shift)
    cu, tail_u = _conv_rows(_dot(x, wu), prev_u, cwu, shift)
    return _silu_gate(cg, cu), tail_g, tail_u


def _silu_gate(cg, cu):
    half = 0.5 * cg
    return ((half * jnp.tanh(half) + half) * cu).astype(BF16)


def _ffn_up_p_kernel(*refs, rb, has_next):
    if has_next:
        (x_ref, wg_ref, wu_ref, cwg_ref, cwu_ref, dn_ref, up_ref,
         a_ref, sg_ref, su_ref, dn16_ref, up16_ref) = refs
        up16_ref[...] = up_ref[...].astype(BF16)
    else:
        x_ref, wg_ref, wu_ref, cwg_ref, cwu_ref, dn_ref, a_ref, sg_ref, su_ref, dn16_ref = refs
    dn16_ref[...] = dn_ref[...].astype(BF16)
    wg, wu, cwg, cwu = wg_ref[...], wu_ref[...], cwg_ref[...], cwu_ref[...]
    halo_g = jnp.zeros((CONV_HALO, wg.shape[1]), F32)
    halo_u = halo_g
    for r in range(x_ref.shape[0] // rb):
        x = x_ref[r * rb:(r + 1) * rb, :]
        hg = _dot(x, wg)
        hu = _dot(x, wu)
        a_ref[r * rb:(r + 1) * rb, :] = _silu_gate(_conv_block(hg, halo_g, cwg),
                                                   _conv_block(hu, halo_u, cwu))
        halo_g = hg[rb - CONV_HALO:rb]
        halo_u = hu[rb - CONV_HALO:rb]
    sg_ref[0] = halo_g[CONV_HALO - CONV_PREV:CONV_HALO]
    su_ref[0] = halo_u[CONV_HALO - CONV_PREV:CONV_HALO]


def _ffn_up_s_kernel(x_ref, wg_ref, wu_ref, cwg_ref, cwu_ref, pg_ref, pu_ref,
                     a_ref, sg_ref, su_ref, *w16_refs, shift):
    wg = wg_ref[...].astype(BF16)
    wu = wu_ref[...].astype(BF16)
    if w16_refs:
        w16_refs[0][...] = wg
        w16_refs[1][...] = wu
    act, tg, tu = _ffn_up_core(x_ref[...], wg, wu, cwg_ref[...], cwu_ref[...],
                               pg_ref[...], pu_ref[...], shift)
    a_ref[...] = act
    sg_ref[...] = tg
    su_ref[...] = tu


def ffn_up_sample(x16, w_up, layer, cw, prev, shift):
    rows, d = x16.shape
    emit = layer is not None
    f = w_up.shape[-1] // 2
    tn = _pick(f, 256, 128)
    nf = f // tn
    sec = lambda r, k: pl.BlockSpec((r, tn), lambda n, k=k: (0, k * nf + n))
    if emit:
        wsec = lambda k: pl.BlockSpec((None, d, tn), lambda n, k=k: (layer, 0, k * nf + n))
    else:
        wsec = lambda k: sec(d, k)
    col = lambda r: pl.BlockSpec((r, tn), lambda n: (0, n))
    pr = CONV_PREV * shift
    w16_shape = jax.ShapeDtypeStruct((d, f), BF16)
    return pl.pallas_call(
        functools.partial(_ffn_up_s_kernel, shift=shift),
        grid=(nf,),
        in_specs=[pl.BlockSpec((rows, d), lambda n: (0, 0)), wsec(0), wsec(1),
                  sec(CONV_TAPS, 0), sec(CONV_TAPS, 1), sec(pr, 0), sec(pr, 1)],
        out_specs=[col(rows), col(pr), col(pr)] + ([col(d), col(d)] if emit else []),
        out_shape=[jax.ShapeDtypeStruct((rows, f), BF16),
                   jax.ShapeDtypeStruct((pr, f), F32),
                   jax.ShapeDtypeStruct((pr, f), F32)] + ([w16_shape, w16_shape] if emit else []),
        compiler_params=_params(("arbitrary",)),
        name="ffn_up_sample",
    )(x16, w_up, w_up, cw, cw, prev, prev)


def ffn_up_prompt(x16, wg16, wu16, u_off, cw, batch, w_down, w_up, layer):
    m, d = x16.shape
    s = m // batch
    f = w_down.shape[1]
    tn = _pick(f, 256, 128)
    nf = f // tn
    steps = batch * nf
    has_next = layer + 1 < w_up.shape[0]
    dn_rows, up_cols = f // steps, 2 * f // steps
    assert dn_rows * steps == f and dn_rows % BF16_SUBLANES == 0 and up_cols % LANE == 0
    step = lambda b, n: b * nf + n
    st_spec = pl.BlockSpec((1, CONV_PREV, tn), lambda b, n: (b, 0, n))
    in_specs = [pl.BlockSpec((s, d), lambda b, n: (b, 0)),
                pl.BlockSpec((d, tn), lambda b, n: (0, n)),
                pl.BlockSpec((d, tn), lambda b, n: (0, u_off + n)),
                pl.BlockSpec((CONV_TAPS, tn), lambda b, n: (0, n)),
                pl.BlockSpec((CONV_TAPS, tn), lambda b, n: (0, nf + n)),
                pl.BlockSpec((None, dn_rows, d), lambda b, n: (layer, step(b, n), 0))]
    out_specs = [pl.BlockSpec((s, tn), lambda b, n: (b, n)), st_spec, st_spec,
                 pl.BlockSpec((dn_rows, d), lambda b, n: (step(b, n), 0))]
    out_shape = [jax.ShapeDtypeStruct((m, f), BF16),
                 jax.ShapeDtypeStruct((batch, CONV_PREV, f), F32),
                 jax.ShapeDtypeStruct((batch, CONV_PREV, f), F32),
                 jax.ShapeDtypeStruct((f, d), BF16)]
    args = [x16, wg16, wu16, cw, cw, w_down]
    if has_next:
        in_specs.append(pl.BlockSpec((None, d, up_cols), lambda b, n: (layer + 1, 0, step(b, n))))
        out_specs.append(pl.BlockSpec((d, up_cols), lambda b, n: (0, step(b, n))))
        out_shape.append(jax.ShapeDtypeStruct((d, 2 * f), BF16))
        args.append(w_up)
    return pl.pallas_call(
        functools.partial(_ffn_up_p_kernel, rb=_pick(s, 512, 256, 128), has_next=has_next),
        grid=(batch, nf),
        in_specs=in_specs,
        out_specs=out_specs,
        out_shape=out_shape,
        compiler_params=_params(("arbitrary", "arbitrary")),
        name="ffn_up_prompt",
    )(*args)


def _mm_p_kernel(*refs, alpha, has_res):
    if has_res:
        x_ref, w_ref, r_ref, o_ref = refs
    else:
        x_ref, w_ref, o_ref = refs
    acc = _dot(x_ref[...], w_ref[...])
    if has_res:
        acc = alpha * r_ref[...] + acc
    o_ref[...] = acc.astype(o_ref.dtype)


def _mm_s_kernel(*refs, alpha, has_res):
    if has_res:
        x_ref, w_ref, r_ref, o_ref, w16_ref = refs
    else:
        x_ref, w_ref, o_ref, w16_ref = refs
    w = w_ref[...].astype(BF16)
    w16_ref[...] = w
    acc = _dot(x_ref[...], w)
    if has_res:
        acc = alpha * r_ref[...] + acc
    o_ref[...] = acc.astype(o_ref.dtype)


def mm_sample(x16, w, layer, n_out, out_dtype, res=None, alpha=None):
    rows, k = x16.shape
    tn = _pick(n_out, 512, 256, 128) if k <= 4096 else _pick(n_out, 256, 128)
    col = lambda r: pl.BlockSpec((r, tn), lambda n: (0, n))
    in_specs = [pl.BlockSpec((rows, k), lambda n: (0, 0)),
                pl.BlockSpec((None, k, tn), lambda n: (layer, 0, n))]
    args = [x16, w]
    if res is not None:
        in_specs.append(col(rows))
        args.append(res)
    return pl.pallas_call(
        functools.partial(_mm_s_kernel, alpha=alpha, has_res=res is not None),
        grid=(n_out // tn,),
        in_specs=in_specs,
        out_specs=[col(rows), col(k)],
        out_shape=[jax.ShapeDtypeStruct((rows, n_out), out_dtype),
                   jax.ShapeDtypeStruct((k, n_out), BF16)],
        compiler_params=_params(("arbitrary",)),
        name="mm_sample",
    )(*args)


def mm_prompt(x16, w16, n_out, col_off, out_dtype, res=None, alpha=None):
    m, k = x16.shape
    if k <= 4096:
        tm, tn = _pick(m, 1024, 512, 256, 128), _pick(n_out, 1024, 512, 256, 128)
    else:
        tm, tn = _pick(m, 512, 256, 128), _pick(n_out, 512, 256, 128)
    assert col_off % tn == 0
    off = col_off // tn
    in_specs = [pl.BlockSpec((tm, k), lambda i, n: (i, 0)),
                pl.BlockSpec((k, tn), lambda i, n: (0, off + n))]
    args = [x16, w16]
    if res is not None:
        in_specs.append(pl.BlockSpec((tm, tn), lambda i, n: (i, n)))
        args.append(res)
    return pl.pallas_call(
        functools.partial(_mm_p_kernel, alpha=alpha, has_res=res is not None),
        grid=(m // tm, n_out // tn),
        in_specs=in_specs,
        out_specs=pl.BlockSpec((tm, tn), lambda i, n: (i, n)),
        out_shape=jax.ShapeDtypeStruct((m, n_out), out_dtype),
        compiler_params=_params(("arbitrary", "arbitrary")),
        name="mm_prompt",
    )(*args)


def _ln_kernel(y_ref, g_ref, b_ref, xf_ref, xb_ref):
    y = y_ref[...]
    mu = jnp.mean(y, axis=-1, keepdims=True)
    dev = y - mu
    var = jnp.mean(dev * dev, axis=-1, keepdims=True)
    out = dev * lax.rsqrt(var + LN_EPS) * g_ref[...] + b_ref[...]
    xf_ref[...] = out
    xb_ref[...] = out.astype(BF16)


def layer_norm_rows(y, g, b):
    m, d = y.shape
    tr = _pick(m, 512, 256, 128)
    row = pl.BlockSpec((tr, d), lambda i: (i, 0))
    vec = pl.BlockSpec((1, d), lambda i: (0, 0))
    return pl.pallas_call(
        _ln_kernel,
        grid=(m // tr,),
        in_specs=[row, vec, vec],
        out_specs=[row, row],
        out_shape=[jax.ShapeDtypeStruct((m, d), F32), jax.ShapeDtypeStruct((m, d), BF16)],
        compiler_params=_params(("arbitrary",)),
        name="layer_norm",
    )(y, g.reshape(1, d), b.reshape(1, d))


def _sink_softmax_pv(s, sink, v16):
    m = jnp.maximum(jnp.max(s, axis=-1, keepdims=True), sink)
    p = jnp.exp(s - m)
    den = jnp.sum(p, axis=-1, keepdims=True) + jnp.exp(sink - m)
    return _dot((p / den).astype(BF16), v16)


def _swa_p_kernel(q_ref, kp_ref, ko_ref, vp_ref, vo_ref, sink_ref, o_ref, *, grp, pre, rest):
    blk = q_ref.shape[0]
    rows = grp * blk
    first = pl.program_id(1) == 0
    kcat = jnp.concatenate([kp_ref[...], ko_ref[...]], axis=0).astype(BF16)
    vcat = jnp.concatenate([vp_ref[...], vo_ref[...]], axis=0).astype(BF16)
    tl = lax.broadcasted_iota(jnp.int32, (rows, 2 * blk), 0) % blk
    j = lax.broadcasted_iota(jnp.int32, (rows, 2 * blk), 1)
    mask = (j >= tl) & (j <= tl + WINDOW) & ((j >= blk) | jnp.logical_not(first))
    hd = SW_HEAD_DIM
    unit = (lax.broadcasted_iota(jnp.int32, (2 * blk, hd), 1) == 0).astype(BF16)
    for h in range(SW_KV_HEADS):
        heads = range(h * grp, (h + 1) * grp)
        q = jnp.concatenate([q_ref[:, a * hd:(a + 1) * hd] for a in heads], axis=0) * pre
        sink = jnp.concatenate([jnp.full((blk, 1), sink_ref[a], F32) for a in heads], axis=0)
        s = _dot_nt(q, kcat[:, h * hd:(h + 1) * hd])
        if rest != 1.0:
            s = s * rest
        s = jnp.where(mask, s, NEG_INF)
        m = jnp.maximum(jnp.max(s, axis=-1, keepdims=True), sink)
        vh = jnp.concatenate([vcat[:, h * hd:(h + 1) * hd], unit], axis=1)
        ov = _dot(jnp.exp(s - m).astype(BF16), vh)
        den = ov[:, hd:hd + 1] + jnp.exp(sink - m)
        o = (ov[:, :hd] / den).astype(BF16)
        for g, a in enumerate(heads):
            o_ref[:, a * hd:(a + 1) * hd] = o[g * blk:(g + 1) * blk]


def swa_prompt(q16, k, v, sinks, batch):
    m, qw = q16.shape
    s = m // batch
    blk = WINDOW
    nb = s // blk
    kw = SW_KV_HEADS * SW_HEAD_DIM
    grp = qw // kw
    own = pl.BlockSpec((blk, kw), lambda b, i: (b * nb + i, 0))
    prev = pl.BlockSpec((blk, kw), lambda b, i: (b * nb + jnp.maximum(i - 1, 0), 0))
    pre, rest = _split_scale(SW_HEAD_DIM ** -0.5)
    return pl.pallas_call(
        functools.partial(_swa_p_kernel, grp=grp, pre=pre, rest=rest),
        grid=(batch, nb),
        in_specs=[pl.BlockSpec((blk, qw), lambda b, i: (b * nb + i, 0)),
                  prev, own, prev, own,
                  pl.BlockSpec(memory_space=pltpu.SMEM)],
        out_specs=pl.BlockSpec((blk, qw), lambda b, i: (b * nb + i, 0)),
        out_shape=jax.ShapeDtypeStruct((m, qw), BF16),
        compiler_params=_params(("arbitrary", "arbitrary")),
        name="swa_prompt",
    )(q16, k, k, v, v, sinks)


def _swa_s_kernel(q_ref, ck_ref, cv_ref, kn_ref, vn_ref, sink_ref, o_ref, *, steps, scale):
    wb, kvh, hd = ck_ref.shape
    pad = jnp.zeros((wb - kn_ref.shape[1], hd), F32)
    rows = q_ref.shape[2]
    t = lax.broadcasted_iota(jnp.int32, (rows, 2 * wb), 0) % steps
    j = lax.broadcasted_iota(jnp.int32, (rows, 2 * wb), 1)
    mask = (j >= t + wb - WINDOW) & (j <= t + wb)

    def keys(cache_ref, new_ref, h):
        cached = cache_ref.reshape(wb * kvh, hd)[pl.ds(h, wb, stride=kvh), :]
        return jnp.concatenate([cached, new_ref[0, :, h * hd:(h + 1) * hd], pad], axis=0).astype(BF16)

    for h in range(kvh):
        s = _dot_nt(q_ref[0, h], keys(ck_ref, kn_ref, h)) * scale
        s = jnp.where(mask, s, NEG_INF)
        o_ref[0, h] = _sink_softmax_pv(s, sink_ref[h], keys(cv_ref, vn_ref, h))


def swa_sample(q16, cache_k, cache_v, layer, k_new, v_new, sink_col, steps):
    bd, kvh, rows, hd = q16.shape
    wb = cache_k.shape[2]
    per_b = lambda a: pl.BlockSpec((1,) + a.shape[1:], lambda b: (b,) + (0,) * (a.ndim - 1))
    cache_spec = pl.BlockSpec((None, None, wb, kvh, hd), lambda b: (layer, b, 0, 0, 0))
    return pl.pallas_call(
        functools.partial(_swa_s_kernel, steps=steps, scale=SW_HEAD_DIM ** -0.5),
        grid=(bd,),
        in_specs=[per_b(q16), cache_spec, cache_spec, per_b(k_new), per_b(v_new),
                  pl.BlockSpec(sink_col.shape, lambda b: (0, 0, 0))],
        out_specs=pl.BlockSpec((1, kvh, rows, hd), lambda b: (b, 0, 0, 0)),
        out_shape=jax.ShapeDtypeStruct((bd, kvh, rows, hd), F32),
        compiler_params=_params(("arbitrary",)),
        name="swa_sample",
    )(q16, cache_k, cache_v, k_new, v_new, sink_col)


def _log_sigmoid(z):
    return jnp.minimum(z, 0.0) - jnp.log1p(jnp.exp(-jnp.abs(z)))


def _fox_logf_p_kernel(x_ref, wf_ref, bf_ref, lf_ref, c_ref, ct_ref, *, blk):
    lf = _log_sigmoid(_dot(x_ref[...], wf_ref[...].astype(BF16)) + bf_ref[...])
    lf_ref[...] = lf
    r = lax.broadcasted_iota(jnp.int32, (blk, blk), 0)
    c = lax.broadcasted_iota(jnp.int32, (blk, blk), 1)
    upto_rows = (c <= r).astype(F32)
    upto_cols = (r <= c).astype(F32)
    carry = jnp.zeros((1, lf.shape[1]), F32)
    carry_t = jnp.zeros((lf.shape[1], 1), F32)
    for i in range(lf.shape[0] // blk):
        part = lf[i * blk:(i + 1) * blk]
        cb = jnp.dot(upto_rows, part, precision=lax.Precision.HIGHEST,
                     preferred_element_type=F32) + carry
        c_ref[i * blk:(i + 1) * blk, :] = cb
        carry = cb[blk - 1:blk, :]
        cbt = lax.dot_general(part, upto_cols, (((0,), (0,)), ((), ())),
                              precision=lax.Precision.HIGHEST, preferred_element_type=F32) + carry_t
        ct_ref[0, :, i * blk:(i + 1) * blk] = cbt
        carry_t = cbt[:, blk - 1:blk]


def fox_logf_prompt(x16, wf, bf, batch):
    m, d = x16.shape
    s = m // batch
    h = wf.shape[1]
    row = pl.BlockSpec((s, h), lambda b: (b, 0))
    return pl.pallas_call(
        functools.partial(_fox_logf_p_kernel, blk=_pick(s, 256, 128)),
        grid=(batch,),
        in_specs=[pl.BlockSpec((s, d), lambda b: (b, 0)),
                  pl.BlockSpec((d, h), lambda b: (0, 0)),
                  pl.BlockSpec((1, h), lambda b: (0, 0))],
        out_specs=[row, row, pl.BlockSpec((1, h, s), lambda b: (b, 0, 0))],
        out_shape=[jax.ShapeDtypeStruct((m, h), F32), jax.ShapeDtypeStruct((m, h), F32),
                   jax.ShapeDtypeStruct((batch, h, s), F32)],
        compiler_params=_params(("arbitrary",)),
        name="fox_logf_prompt",
    )(x16, wf, bf.reshape(1, h))


def _fox_logf_s_kernel(x_ref, wf_ref, bf_ref, lf_ref, c_ref, *, shift):
    lf = _log_sigmoid(_dot(x_ref[...], wf_ref[...].astype(BF16)) + bf_ref[...])
    lf_ref[...] = lf
    run = lf[0:shift]
    c_ref[0:shift, :] = run
    for t in range(1, lf.shape[0] // shift):
        run = run + lf[t * shift:(t + 1) * shift]
        c_ref[t * shift:(t + 1) * shift, :] = run


def fox_logf_sample(x16, wf, bf, shift):
    rows, d = x16.shape
    h = wf.shape[1]
    full = lambda a, b: pl.BlockSpec((a, b), lambda i: (0, 0))
    return pl.pallas_call(
        functools.partial(_fox_logf_s_kernel, shift=shift),
        grid=(1,),
        in_specs=[full(rows, d), full(d, h), full(1, h)],
        out_specs=[full(rows, h), full(rows, h)],
        out_shape=[jax.ShapeDtypeStruct((rows, h), F32), jax.ShapeDtypeStruct((rows, h), F32)],
        compiler_params=_params(("arbitrary",)),
        name="fox_logf_sample",
    )(x16, wf, bf.reshape(1, h))


def _fox_p_kernel(q_ref, k_ref, v_ref, c_ref, ck_ref, o_ref, k16_ref, v16_ref, cq_ref, m_ref, acc_ref,
                  *, grp, pre, rest):
    h = pl.program_id(1)
    qi = pl.program_id(2)
    tq = q_ref.shape[0]
    tk = tq
    hd = FOX_HEAD_DIM
    rows = grp * tq
    c_exp = rest * LOG2_E

    @pl.when(qi == 0)
    def _():
        unit = (lax.broadcasted_iota(jnp.int32, v_ref.shape, 1) == 0).astype(BF16)
        k16_ref[...] = k_ref[...].astype(BF16)
        v16_ref[...] = jnp.concatenate([v_ref[...].astype(BF16), unit], axis=1)

    q = jnp.concatenate([q_ref[:, g * hd:(g + 1) * hd] for g in range(grp)], axis=0) * pre
    c_blk = c_ref[...]
    lane = lax.broadcasted_iota(jnp.int32, c_blk.shape, 1)
    for g in range(grp):
        col = jnp.sum(jnp.where(lane == h * grp + g, c_blk, 0.0), axis=-1, keepdims=True)
        cq_ref[g * tq:(g + 1) * tq, :] = jnp.broadcast_to(col * (1.0 / rest), (tq, hd))
    m_ref[...] = jnp.full(m_ref.shape, NEG_INF, F32)
    acc_ref[...] = jnp.zeros(acc_ref.shape, F32)

    def update(j, diagonal):
        off = pl.multiple_of(j * tk, tk)
        k16 = k16_ref[pl.ds(off, tk), :]
        v16 = v16_ref[pl.ds(off, tk), :]
        ck_all = ck_ref[0, 0, j] * (1.0 / rest)
        half_g = max(grp // 2, 1)
        for g0 in range(0, grp, half_g):
            sl = slice(g0 * tq, (g0 + half_g) * tq)
            nrow = half_g * tq
            z = _dot_nt(q[sl], k16)
            cq = cq_ref[sl, :]
            t = jnp.concatenate([z[:, i * hd:(i + 1) * hd] + cq for i in range(tk // hd)], axis=1)
            t = (t.reshape(half_g, tq, tk) - ck_all[g0:g0 + half_g, None, :]).reshape(nrow, tk)
            if diagonal:
                r = lax.broadcasted_iota(jnp.int32, (nrow, tk), 0) % tq
                c = lax.broadcasted_iota(jnp.int32, (nrow, tk), 1)
                t = jnp.where(c <= r, t, NEG_INF)
            m_old = m_ref[sl, :]
            m_new = jnp.maximum(m_old, jnp.max(t, axis=-1, keepdims=True))
            a = jnp.exp2((m_old - m_new) * c_exp)
            p = jnp.exp2((t - m_new) * c_exp)
            acc_ref[sl, :] = a * acc_ref[sl, :] + _dot(p.astype(BF16), v16)
            m_ref[sl, :] = m_new

    def below_diagonal(j, carry):
        update(j, False)
        return carry

    lax.fori_loop(0, qi, below_diagonal, 0)
    update(qi, True)
    acc = acc_ref[...]
    out = acc[:, :hd] / acc[:, hd:hd + 1]
    for g in range(grp):
        o_ref[:, g * hd:(g + 1) * hd] = out[g * tq:(g + 1) * tq].astype(BF16)


def fox_prompt(q16, k, v, c, c_k, batch):
    m, qw = q16.shape
    heads = c.shape[1]
    s = m // batch
    hd, kvh = FOX_HEAD_DIM, FOX_KV_HEADS
    grp = qw // (kvh * hd)
    nk, tq = c_k.shape[2], c_k.shape[4]
    nq = s // tq
    pre, rest = _split_scale(hd ** -0.5)
    kv_spec = pl.BlockSpec((s, hd), lambda b, h, qi: (b, h))
    return pl.pallas_call(
        functools.partial(_fox_p_kernel, grp=grp, pre=pre, rest=rest),
        grid=(batch, kvh, nq),
        in_specs=[pl.BlockSpec((tq, grp * hd), lambda b, h, qi: (b * nq + qi, h)),
                  kv_spec, kv_spec,
                  pl.BlockSpec((tq, heads), lambda b, h, qi: (b * nq + qi, 0)),
                  pl.BlockSpec((1, 1, nk, grp, tq), lambda b, h, qi: (b, h, 0, 0, 0))],
        out_specs=pl.BlockSpec((tq, grp * hd), lambda b, h, qi: (b * nq + qi, h)),
        out_shape=jax.ShapeDtypeStruct((m, qw), BF16),
        scratch_shapes=[pltpu.VMEM((s, hd), BF16), pltpu.VMEM((s, 2 * hd), BF16),
                        pltpu.VMEM((grp * tq, hd), F32), pltpu.VMEM((grp * tq, 1), F32),
                        pltpu.VMEM((grp * tq, 2 * hd), F32)],
        compiler_params=_params(("arbitrary", "arbitrary", "arbitrary")),
        name="fox_prompt",
    )(q16, k, v, c, c_k)


def _fox_s_kernel(pt_ref, q_ref, ccol_ref, cnt_ref, kn_ref, vn_ref, *rest, pages, steps, grp, scale):
    k_refs = rest[0:pages]
    v_refs = rest[pages:2 * pages]
    lf_refs = rest[2 * pages:3 * pages]
    o_ref, qbd_ref, m_ref, l_ref, acc_ref, carry_ref = rest[3 * pages:]
    p = pl.program_id(1)
    rows, hd = q_ref.shape[1], q_ref.shape[2]
    heads = rows // steps
    kvh = heads // grp
    psz = cnt_ref.shape[2]
    row = lax.broadcasted_iota(jnp.int32, (rows, hd), 0)
    row_kvh = (row % heads) // grp

    def page_update(k16, v16, neg_ck, mask):
        s = _dot_nt(qbd_ref[...], k16) * scale
        s = s + (ccol_ref[0] + jnp.concatenate([neg_ck] * steps, axis=0))
        if mask is not None:
            s = jnp.where(mask, s, NEG_INF)
        m_old = m_ref[...]
        m_new = jnp.maximum(m_old, jnp.max(s, axis=-1, keepdims=True))
        a = jnp.exp(m_old - m_new)
        pr = jnp.exp(s - m_new)
        l_ref[...] = a * l_ref[...] + jnp.sum(pr, axis=-1, keepdims=True)
        acc_ref[...] = a * acc_ref[...] + _dot(pr.astype(BF16), v16)
        m_ref[...] = m_new

    @pl.when(p == 0)
    def _():
        q = q_ref[0]
        for h in range(kvh):
            qbd_ref[:, h * hd:(h + 1) * hd] = jnp.where(row_kvh == h, q, jnp.zeros_like(q))
        m_ref[...] = jnp.full(m_ref.shape, NEG_INF, F32)
        l_ref[...] = jnp.zeros(l_ref.shape, F32)
        acc_ref[...] = jnp.zeros(acc_ref.shape, F32)
        carry_ref[...] = jnp.zeros(carry_ref.shape, F32)
        pad = jnp.zeros((psz - kn_ref.shape[1], kn_ref.shape[2]), F32)
        k16 = jnp.concatenate([kn_ref[0], pad], axis=0).astype(BF16)
        v16 = jnp.concatenate([vn_ref[0], pad], axis=0).astype(BF16)
        t = lax.broadcasted_iota(jnp.int32, (rows, psz), 0) // heads
        j = lax.broadcasted_iota(jnp.int32, (rows, psz), 1)
        page_update(k16, v16, -cnt_ref[0], j <= t)

    def flat16(ref):
        rows2d = ref.reshape(psz * kvh, hd)
        return jnp.concatenate([rows2d[pl.ds(h, psz, stride=kvh), :] for h in range(kvh)],
                               axis=1).astype(BF16)

    @pl.when(p > 0)
    def _():
        r = lax.broadcasted_iota(jnp.int32, (psz, 2 * psz), 0)
        c = lax.broadcasted_iota(jnp.int32, (psz, 2 * psz), 1)
        later_and_total = ((r > c) | (c >= psz)).astype(F32)
        carry = carry_ref[...]
        parts = []
        for i in range(pages):
            w = jnp.dot(lf_refs[i][...], later_and_total, precision=lax.Precision.HIGHEST,
                        preferred_element_type=F32)
            parts.append(w[:, :psz] + carry)
            carry = carry + w[:, psz:psz + 1]
        carry_ref[...] = carry
        k16 = jnp.concatenate([flat16(k_refs[i]) for i in range(pages)], axis=0)
        v16 = jnp.concatenate([flat16(v_refs[i]) for i in range(pages)], axis=0)
        page_update(k16, v16, jnp.concatenate(parts, axis=1), None)

    @pl.when(p == pl.num_programs(1) - 1)
    def _():
        out = jnp.zeros((rows, hd), F32)
        for h in range(kvh):
            out = out + jnp.where(row_kvh == h, acc_ref[:, h * hd:(h + 1) * hd], 0.0)
        o_ref[0] = (out / l_ref[...]).astype(BF16)


def fox_sample(page_table, q16, c_col, c_new_t, k_new, v_new, pool_k, pool_v, pool_lf, layer, steps):
    bd, rows, hd = q16.shape
    heads = rows // steps
    psz, kvh = pool_k.shape[2], pool_k.shape[3]
    kw = kvh * hd
    n_pages = page_table.shape[1]
    pages = _pick(n_pages, 16, 8, 4, 2, 1)
    n_steps = n_pages // pages + 1

    def page_idx(i, tail):
        return lambda b, p, pt: (layer, pt[b, n_pages - 1 - (jnp.maximum(p - 1, 0) * pages + i)]) + tail

    per_b = lambda a: pl.BlockSpec((1,) + a.shape[1:], lambda b, p, pt: (b,) + (0,) * (a.ndim - 1))
    kv_specs = lambda: [pl.BlockSpec((None, None, psz, kvh, hd), page_idx(i, (0, 0, 0))) for i in range(pages)]
    lf_specs = [pl.BlockSpec((None, None, heads, psz), page_idx(i, (0, 0))) for i in range(pages)]
    grid_spec = pltpu.PrefetchScalarGridSpec(
        num_scalar_prefetch=1,
        grid=(bd, n_steps),
        in_specs=[per_b(q16), per_b(c_col), per_b(c_new_t), per_b(k_new), per_b(v_new)]
        + kv_specs() + kv_specs() + lf_specs,
        out_specs=pl.BlockSpec((1, rows, hd), lambda b, p, pt: (b, 0, 0)),
        scratch_shapes=[pltpu.VMEM((rows, kw), BF16), pltpu.VMEM((rows, 1), F32),
                        pltpu.VMEM((rows, 1), F32), pltpu.VMEM((rows, kw), F32),
                        pltpu.VMEM((heads, 1), F32)],
    )
    return pl.pallas_call(
        functools.partial(_fox_s_kernel, pages=pages, steps=steps, grp=heads // FOX_KV_HEADS,
                          scale=FOX_HEAD_DIM ** -0.5),
        grid_spec=grid_spec,
        out_shape=jax.ShapeDtypeStruct((bd, rows, hd), BF16),
        compiler_params=_params(("arbitrary", "arbitrary")),
        name="fox_sample",
    )(page_table, q16, c_col, c_new_t, k_new, v_new,
      *([pool_k] * pages), *([pool_v] * pages), *([pool_lf] * pages))


def _time_major(a):
    return jnp.swapaxes(a, 0, 1).reshape((a.shape[0] * a.shape[1],) + a.shape[2:])


def _batch_major(a, bd):
    return jnp.swapaxes(a.reshape((a.shape[0] // bd, bd) + a.shape[1:]), 0, 1)


def _pad_rows(a, rows):
    return jnp.pad(a, ((0, 0), (0, rows - a.shape[1]), (0, 0)))


def kernel(x_prompt, x_sample, state_conv, cache_win_k, cache_win_v, cache_k, cache_v, cache_logf,
           state_ffn, page_table, w_in_a, conv_a, w_out_a, w_qkv_b, sinks_b, w_o_b, w_qkvf_c, b_f_c,
           w_o_c, ln1_g, ln1_b, w_up, conv_f, w_down, ln2_g, ln2_b):
    batch, seq, d = x_prompt.shape
    bd, steps, _ = x_sample.shape
    depth = ln1_g.shape[0]
    alpha = (2 * depth) ** 0.25
    n_mixers = 3
    f = w_down.shape[1]

    xp = x_prompt.reshape(batch * seq, d)
    xp16 = xp.astype(BF16)
    xs = _time_major(x_sample)
    xs16 = xs.astype(BF16)

    up16 = None
    conv_p, conv_s, wk_p, wv_p, wk_s, wv_s = [], [], [], [], [], []
    fk_p, fv_p, fl_p, fk_s, fv_s, fl_s = [], [], [], [], [], []
    ffn_p, ffn_s = [], []

    for i in range(depth):
        mix, j = i % n_mixers, i // n_mixers
        if mix == 0:
            zs, st_s, wb16, wc16, wh16 = gate_conv_sample(xs16, w_in_a, j, conv_a[j],
                                                          _time_major(state_conv[j]), bd)
            ys, wo16 = mm_sample(zs, w_out_a, j, d, F32, res=xs, alpha=alpha)
            zp, st_p = gate_conv_prompt(xp16, wb16, wc16, wh16, conv_a[j], batch)
            yp = mm_prompt(zp, wo16, d, 0, F32, res=xp, alpha=alpha)
            conv_p.append(st_p)
            conv_s.append(_batch_major(st_s, bd))
        elif mix == 1:
            hd, kvh = SW_HEAD_DIM, SW_KV_HEADS
            kw = kvh * hd
            grp = d // kw
            qkv_s, w16 = mm_sample(xs16, w_qkv_b, j, d + 2 * kw, F32)
            qp = mm_prompt(xp16, w16, d, 0, BF16)
            kp = mm_prompt(xp16, w16, kw, d, F32)
            vp = mm_prompt(xp16, w16, kw, d + kw, F32)
            op = swa_prompt(qp, kp, vp, sinks_b[j], batch)
            qs = qkv_s[:, :d].astype(BF16).reshape(steps, bd, kvh, grp, hd)
            qs = qs.transpose(1, 2, 3, 0, 4).reshape(bd, kvh, grp * steps, hd)
            kn = _batch_major(qkv_s[:, d:d + kw], bd)
            vn = _batch_major(qkv_s[:, d + kw:], bd)
            sink_col = jnp.repeat(sinks_b[j].reshape(kvh, grp), steps, axis=1)[..., None]
            os_ = swa_sample(qs, cache_win_k, cache_win_v, j, _pad_rows(kn, 8), _pad_rows(vn, 8),
                             sink_col, steps)
            os_ = os_.reshape(bd, kvh, grp, steps, hd).transpose(3, 0, 1, 2, 4).reshape(steps * bd, d)
            ys, wo16 = mm_sample(os_.astype(BF16), w_o_b, j, d, F32, res=xs, alpha=alpha)
            yp = mm_prompt(op, wo16, d, 0, F32, res=xp, alpha=alpha)
            keep = min(WINDOW, seq)
            wk_p.append(kp.reshape(batch, seq, kw)[:, seq - keep:].reshape(batch, keep, kvh, hd))
            wv_p.append(vp.reshape(batch, seq, kw)[:, seq - keep:].reshape(batch, keep, kvh, hd))
            wb = cache_win_k.shape[2]
            wk_s.append(jnp.concatenate([cache_win_k[j], kn.reshape(bd, steps, kvh, hd)], axis=1)[:, -wb:])
            wv_s.append(jnp.concatenate([cache_win_v[j], vn.reshape(bd, steps, kvh, hd)], axis=1)[:, -wb:])
        else:
            hd, kvh = FOX_HEAD_DIM, FOX_KV_HEADS
            kw = kvh * hd
            heads = d // hd
            grp = heads // kvh
            wf = w_qkvf_c[j, :, d + 2 * kw:]
            qkv_s, w16 = mm_sample(xs16, w_qkvf_c, j, d + 2 * kw, F32)
            lf_s, c_s = fox_logf_sample(xs16, wf, b_f_c[j], bd)
            qp = mm_prompt(xp16, w16, d, 0, BF16)
            kp = mm_prompt(xp16, w16, kw, d, F32)
            vp = mm_prompt(xp16, w16, kw, d + kw, F32)
            lf_p, c_p, ct_p = fox_logf_prompt(xp16, wf, b_f_c[j], batch)
            tk = _pick(seq, 512, 256, 128)
            c_k = ct_p.reshape(batch, kvh, grp, seq // tk, tk).transpose(0, 1, 3, 2, 4)
            op = fox_prompt(qp, kp, vp, c_p, c_k, batch)
            qs = _batch_major(qkv_s[:, :d].astype(BF16), bd).reshape(bd, steps * heads, hd)
            kn = _batch_major(qkv_s[:, d:d + kw], bd)
            vn = _batch_major(qkv_s[:, d + kw:], bd)
            c_b = _batch_major(c_s, bd)
            c_col = c_b.reshape(bd, steps * heads, 1)
            c_new_t = jnp.pad(c_b.transpose(0, 2, 1), ((0, 0), (0, 0), (0, PAGE_SIZE - steps)))
            os_ = fox_sample(page_table, qs, c_col, c_new_t, _pad_rows(kn, 8), _pad_rows(vn, 8),
                             cache_k, cache_v,
                             jnp.swapaxes(cache_logf, 2, 3), j, steps)
            os_ = _time_major(os_.reshape(bd, steps, d))
            ys, wo16 = mm_sample(os_, w_o_c, j, d, F32, res=xs, alpha=alpha)
            yp = mm_prompt(op, wo16, d, 0, F32, res=xp, alpha=alpha)
            fk_p.append(kp.reshape(batch, seq, kvh, hd))
            fv_p.append(vp.reshape(batch, seq, kvh, hd))
            fl_p.append(lf_p.reshape(batch, seq, heads))
            fk_s.append(kn.reshape(bd, steps, kvh, hd))
            fv_s.append(vn.reshape(bd, steps, kvh, hd))
            fl_s.append(_batch_major(lf_s, bd))
        xs, xs16 = layer_norm_rows(ys, ln1_g[i], ln1_b[i])
        xp, xp16 = layer_norm_rows(yp, ln1_g[i], ln1_b[i])

        prev_ffn = _time_major(state_ffn[i])
        if up16 is None:
            a_s, sg_s, su_s, wg16, wu16 = ffn_up_sample(xs16, w_up, i, conv_f[i], prev_ffn, bd)
            u_off = 0
        else:
            a_s, sg_s, su_s = ffn_up_sample(xs16, up16, None, conv_f[i], prev_ffn, bd)
            wg16 = wu16 = up16
            u_off = f // _pick(f, 256, 128)
        a_p, sg_p, su_p, wd16, *nxt = ffn_up_prompt(xp16, wg16, wu16, u_off, conv_f[i], batch,
                                                   w_down, w_up, i)
        up16 = nxt[0] if nxt else None
        ys = mm_prompt(a_s, wd16, d, 0, F32, res=xs, alpha=alpha)
        yp = mm_prompt(a_p, wd16, d, 0, F32, res=xp, alpha=alpha)
        ffn_p.append(jnp.concatenate([sg_p, su_p], axis=-1))
        ffn_s.append(_batch_major(jnp.concatenate([sg_s, su_s], axis=-1), bd))
        xs, xs16 = layer_norm_rows(ys, ln2_g[i], ln2_b[i])
        xp, xp16 = layer_norm_rows(yp, ln2_g[i], ln2_b[i])

    return (xp.reshape(batch, seq, d), _batch_major(xs, bd),
            jnp.stack(conv_p), jnp.stack(conv_s), jnp.stack(wk_p), jnp.stack(wv_p),
            jnp.stack(wk_s), jnp.stack(wv_s), jnp.stack(fk_p), jnp.stack(fv_p), jnp.stack(fl_p),
            jnp.stack(fk_s), jnp.stack(fv_s), jnp.stack(fl_s), jnp.stack(ffn_p), jnp.stack(ffn_s))
```
